```python
import math
import jax
import jax.numpy as jnp
from jax import lax
import numpy as np

D_MODEL = 2048
BATCH = 4
SEQ = 4096
DEPTH = 1

HEAD_DIM = 128
ROPE_DIM = HEAD_DIM // 4
ROPE_THETA = 500000.0
BLK = 128
RMS_EPS = 1e-6
DIL_GROUPS = ((128, 1), (512, 4), (2048, 16))
N_DIL_GROUPS = len(DIL_GROUPS)
DIL_HEADS = 4
DIL_WIDTH = N_DIL_GROUPS * DIL_HEADS * HEAD_DIM
DIL_OUT = DIL_HEADS * HEAD_DIM
DIFF_HEADS = 4
DIFF_QK = DIFF_HEADS * 2 * HEAD_DIM
DIFF_OUT = DIFF_HEADS * 2 * HEAD_DIM
IN_SPLITS = (DIL_WIDTH, DIL_WIDTH, DIL_WIDTH, DIFF_QK, DIFF_QK, DIFF_OUT, D_MODEL, D_MODEL)
D_IN = sum(IN_SPLITS)
D_FF = 5632

kernel_name = 'hybrid_dilated_diff_macaron_layer'


def rms_norm(x, g):
    xf = x.astype(jnp.float32)
    y = xf * lax.rsqrt(jnp.mean(xf * xf, axis=-1, keepdims=True) + RMS_EPS)
    return (y * g.astype(jnp.float32)).astype(x.dtype)


def rope_tables(positions):
    inv = ROPE_THETA ** (-jnp.arange(0, ROPE_DIM, 2, dtype=jnp.float32) / ROPE_DIM)
    ang = positions.astype(jnp.float32)[..., None] * inv
    return jnp.cos(ang), jnp.sin(ang)


def apply_rope(x, cos, sin):
    shape = cos.shape[:2] + (1,) * (x.ndim - 3) + (cos.shape[-1],)
    c = cos.reshape(shape)
    s = sin.reshape(shape)
    half = ROPE_DIM // 2
    xr = x[..., :ROPE_DIM].astype(jnp.float32)
    x1, x2 = xr[..., :half], xr[..., half:]
    rot = jnp.concatenate([x1 * c - x2 * s, x2 * c + x1 * s], axis=-1)
    return jnp.concatenate([rot.astype(x.dtype), x[..., ROPE_DIM:]], axis=-1)


def swiglu(h, w_gate, w_up, w_down):
    return (jax.nn.silu(h @ w_gate) * (h @ w_up)) @ w_down


def lambda_init(layer):
    return 0.8 - 0.6 * math.exp(-0.3 * layer)


def dilated_group(q, k, v, window, dilation):
    B, S, H, D = q.shape
    band = window // dilation
    L = S // dilation
    pad = (-L) % BLK
    nb = (L + pad) // BLK

    def strided(t):
        return t.reshape(B, L, dilation, H, D).transpose(0, 2, 1, 3, 4)

    qb = jnp.pad(strided(q), ((0, 0), (0, 0), (0, pad), (0, 0), (0, 0))).reshape(B, dilation, nb, BLK, H, D)

    def band_blocks(t):
        t = jnp.pad(strided(t), ((0, 0), (0, 0), (BLK, pad), (0, 0), (0, 0)))
        t = t.reshape(B, dilation, nb + 1, BLK, H, D)
        return jnp.concatenate([t[:, :, :-1], t[:, :, 1:]], axis=3)

    kb = band_blocks(k)
    vb = band_blocks(v)
    s = jnp.einsum('brnqhd,brnkhd->brnhqk', qb, kb).astype(jnp.float32) * (D ** -0.5)
    qi = jnp.arange(BLK)[:, None]
    kj = jnp.arange(2 * BLK)[None, :]
    dist = qi - kj + BLK
    kpos = jnp.arange(nb)[:, None, None] * BLK + kj[None] - BLK
    valid = (dist >= 0) & (dist <= band) & (kpos >= 0)
    s = jnp.where(valid[None, None, :, None], s, -jnp.inf)
    m = jnp.max(s, axis=-1, keepdims=True)
    p = jnp.exp(s - m)
    den = jnp.sum(p, axis=-1, keepdims=True)
    o = jnp.einsum('brnhqk,brnkhd->brnqhd', (p / den).astype(v.dtype), vb)
    lse = (m + jnp.log(den))[..., 0]
    o = o.reshape(B, dilation, nb * BLK, H, D)[:, :, :L].transpose(0, 2, 1, 3, 4).reshape(B, S, H, D)
    lse = lse.transpose(0, 1, 2, 4, 3).reshape(B, dilation, nb * BLK, H)[:, :, :L]
    lse = lse.transpose(0, 2, 1, 3).reshape(B, S, H)
    return o, lse


def diff_attention(q, k, v, lam):
    B, S, H, _, D = q.shape
    nb = S // BLK
    qb = q.reshape(B, nb, BLK, H, 2, D).transpose(1, 0, 2, 3, 4, 5)
    kpos = jnp.arange(S)

    def block(args):
        qblk, i = args
        s = jnp.einsum('bqhcd,bkhcd->bhcqk', qblk, k).astype(jnp.float32) * (D ** -0.5)
        qpos = i * BLK + jnp.arange(BLK)
        s = jnp.where(kpos[None, :] <= qpos[:, None], s, -jnp.inf)
        p = jax.nn.softmax(s, axis=-1)
        a = p[:, :, 0] - lam * p[:, :, 1]
        return jnp.einsum('bhqk,bkhe->bqhe', a.astype(v.dtype), v)

    o = lax.map(block, (qb, jnp.arange(nb)))
    return o.transpose(1, 0, 2, 3, 4).reshape(B, S, H, 2 * D)


def hybrid_layer(x, cos, sin, layer, ffn1_norm, ffn1_w_gate, ffn1_w_up, ffn1_w_down, mix_norm, w_in,
                 dil_q_norm, dil_k_norm, diff_q_norm, diff_k_norm, diff_lq1, diff_lk1, diff_lq2, diff_lk2,
                 diff_subln, w_dil_branch, w_diff_branch, w_out, ffn2_norm, ffn2_w_gate, ffn2_w_up, ffn2_w_down):
    B, S, _ = x.shape
    x = x + 0.5 * swiglu(rms_norm(x, ffn1_norm), ffn1_w_gate, ffn1_w_up, ffn1_w_down)

    h = rms_norm(x, mix_norm)
    proj = h @ w_in
    idx = np.cumsum(IN_SPLITS)[:-1].tolist()
    dq, dk, dv, fq, fk, fv, g_dil, g_diff = jnp.split(proj, idx, axis=-1)

    gshape = (B, S, N_DIL_GROUPS, DIL_HEADS, HEAD_DIM)
    dq = apply_rope(rms_norm(dq.reshape(gshape), dil_q_norm), cos, sin)
    dk = apply_rope(rms_norm(dk.reshape(gshape), dil_k_norm), cos, sin)
    dv = dv.reshape(gshape)
    outs = []
    lses = []
    for g, (window, dilation) in enumerate(DIL_GROUPS):
        o_g, lse_g = dilated_group(dq[:, :, g], dk[:, :, g], dv[:, :, g], window, dilation)
        outs.append(o_g)
        lses.append(lse_g)
    mix_w = jax.nn.softmax(jnp.stack(lses, axis=0), axis=0).astype(x.dtype)
    o_dil = jnp.einsum('gbsh,gbshd->bshd', mix_w, jnp.stack(outs, axis=0)).reshape(B, S, DIL_OUT)

    fshape = (B, S, DIFF_HEADS, 2, HEAD_DIM)
    fq = apply_rope(rms_norm(fq.reshape(fshape), diff_q_norm), cos, sin)
    fk = apply_rope(rms_norm(fk.reshape(fshape), diff_k_norm), cos, sin)
    fv = fv.reshape(B, S, DIFF_HEADS, 2 * HEAD_DIM)
    lam_init = lambda_init(layer)
    lam = (jnp.exp(jnp.sum(diff_lq1.astype(jnp.float32) * diff_lk1.astype(jnp.float32)))
           - jnp.exp(jnp.sum(diff_lq2.astype(jnp.float32) * diff_lk2.astype(jnp.float32))) + lam_init)
    o_diff = diff_attention(fq, fk, fv, lam)
    o_diff = (rms_norm(o_diff, diff_subln) * (1.0 - lam_init)).reshape(B, S, DIFF_OUT)

    y = jax.nn.sigmoid(g_dil) * (o_dil @ w_dil_branch) + jax.nn.sigmoid(g_diff) * (o_diff @ w_diff_branch)
    x = x + y @ w_out

    x = x + 0.5 * swiglu(rms_norm(x, ffn2_norm), ffn2_w_gate, ffn2_w_up, ffn2_w_down)
    return x


def setup_inputs(seed: int = 0) -> dict:
    key = jax.random.key(seed)
    ks = jax.random.split(key, 32)
    f32 = jnp.float32

    def dense(k, fan_in, fan_out):
        return jax.random.normal(k, (DEPTH, fan_in, fan_out), f32) * fan_in ** -0.5

    def gain(k, n):
        return 1.0 + 0.02 * jax.random.normal(k, (DEPTH, n), f32)

    def small(k, n):
        return 0.1 * jax.random.normal(k, (DEPTH, n), f32)

    x = jax.random.normal(ks[0], (BATCH, SEQ, D_MODEL), f32)
    positions = (jnp.arange(SEQ, dtype=jnp.int32)[None, :]
                 + jax.random.randint(ks[1], (BATCH, 1), 0, 1024, dtype=jnp.int32))
    return {
        'x': x,
        'positions': positions,
        'ffn1_norm': gain(ks[2], D_MODEL),
        'ffn1_w_gate': dense(ks[3], D_MODEL, D_FF),
        'ffn1_w_up': dense(ks[4], D_MODEL, D_FF),
        'ffn1_w_down': dense(ks[5], D_FF, D_MODEL),
        'mix_norm': gain(ks[6], D_MODEL),
        'w_in': dense(ks[7], D_MODEL, D_IN),
        'dil_q_norm': gain(ks[8], HEAD_DIM),
        'dil_k_norm': gain(ks[9], HEAD_DIM),
        'diff_q_norm': gain(ks[10], HEAD_DIM),
        'diff_k_norm': gain(ks[11], HEAD_DIM),
        'diff_lq1': small(ks[12], HEAD_DIM),
        'diff_lk1': small(ks[13], HEAD_DIM),
        'diff_lq2': small(ks[14], HEAD_DIM),
        'diff_lk2': small(ks[15], HEAD_DIM),
        'diff_subln': gain(ks[16], 2 * HEAD_DIM),
        'w_dil_branch': dense(ks[17], DIL_OUT, D_MODEL),
        'w_diff_branch': dense(ks[18], DIFF_OUT, D_MODEL),
        'w_out': dense(ks[19], D_MODEL, D_MODEL),
        'ffn2_norm': gain(ks[20], D_MODEL),
        'ffn2_w_gate': dense(ks[21], D_MODEL, D_FF),
        'ffn2_w_up': dense(ks[22], D_MODEL, D_FF),
        'ffn2_w_down': dense(ks[23], D_FF, D_MODEL),
    }


def reference(x, positions, ffn1_norm, ffn1_w_gate, ffn1_w_up, ffn1_w_down, mix_norm, w_in,
              dil_q_norm, dil_k_norm, diff_q_norm, diff_k_norm, diff_lq1, diff_lk1, diff_lq2, diff_lk2,
              diff_subln, w_dil_branch, w_diff_branch, w_out, ffn2_norm, ffn2_w_gate, ffn2_w_up, ffn2_w_down):
    cos, sin = rope_tables(positions)
    for l in range(DEPTH):
        x = hybrid_layer(x, cos, sin, l, ffn1_norm[l], ffn1_w_gate[l], ffn1_w_up[l], ffn1_w_down[l],
                         mix_norm[l], w_in[l], dil_q_norm[l], dil_k_norm[l], diff_q_norm[l], diff_k_norm[l],
                         diff_lq1[l], diff_lk1[l], diff_lq2[l], diff_lk2[l], diff_subln[l],
                         w_dil_branch[l], w_diff_branch[l], w_out[l], ffn2_norm[l], ffn2_w_gate[l],
                         ffn2_w_up[l], ffn2_w_down[l])
    return x
```

```python
import functools
import math

import jax
import jax.numpy as jnp
from jax import lax
from jax.experimental import pallas as pl
from jax.experimental.pallas import tpu as pltpu

F32 = jnp.float32
BF16 = jnp.bfloat16

D_MODEL = 2048
D_FF = 5632
HEAD_DIM = 128
ROPE_DIM = HEAD_DIM // 4
ROPE_HALF = ROPE_DIM // 2
ROPE_THETA = 500000.0
RMS_EPS = 1e-6
BAND = 128
DILATIONS = (1, 4, 16)
N_DIL_GROUPS = len(DILATIONS)
DIL_HEADS = 4
DIL_GROUP_WIDTH = DIL_HEADS * HEAD_DIM
DIL_WIDTH = N_DIL_GROUPS * DIL_GROUP_WIDTH
DIFF_HEADS = 4
DIFF_HEAD_WIDTH = 2 * HEAD_DIM
DIFF_WIDTH = DIFF_HEADS * DIFF_HEAD_WIDTH
DIL_BLOCK = BAND * DILATIONS[-1]
NEG_BIG = -1e30

VMEM_LIMIT_BYTES = 56 * 1024 * 1024

FFN_TM = 512
FFN_TF = 512
PROJ_TM = 1024
PROJ2_TN = 1024
MERGE_TM = 512
DIFF_TQ = 512


def _params(*sem):
    return pltpu.CompilerParams(dimension_semantics=sem, vmem_limit_bytes=VMEM_LIMIT_BYTES)


def _rms_scale(x):
    return lax.rsqrt(jnp.mean(x * x, axis=-1, keepdims=True) + RMS_EPS)


def _dot(a, b):
    return jnp.dot(a, b, preferred_element_type=F32)


def _dot_nt(a, b):
    return lax.dot_general(a, b, (((1,), (1,)), ((), ())), preferred_element_type=F32)


def _ffn_kernel(x_ref, gin_ref, wg_ref, wu_ref, wd_ref, gout_ref, o_ref, hn_ref, h_ref):
    f = pl.program_id(1)

    @pl.when(f == 0)
    def _():
        x = x_ref[...]
        h_ref[...] = (x * _rms_scale(x) * gin_ref[...]).astype(BF16)
        o_ref[...] = x

    h = h_ref[...]
    g = _dot(h, wg_ref[...])
    u = _dot(h, wu_ref[...])
    a = (g * jax.nn.sigmoid(g)) * u * 0.5
    o_ref[...] += _dot(a.astype(BF16), wd_ref[...])

    @pl.when(f == pl.num_programs(1) - 1)
    def _():
        y = o_ref[...]
        hn_ref[...] = (y * _rms_scale(y) * gout_ref[...]).astype(BF16)


def _ffn(x, g_in, wg, wu, wd, g_out):
    T = x.shape[0]
    tm, tf = FFN_TM, FFN_TF
    return pl.pallas_call(
        _ffn_kernel,
        grid=(T // tm, D_FF // tf),
        in_specs=[
            pl.BlockSpec((tm, D_MODEL), lambda i, f: (i, 0)),
            pl.BlockSpec((1, D_MODEL), lambda i, f: (0, 0)),
            pl.BlockSpec((D_MODEL, tf), lambda i, f: (0, f)),
            pl.BlockSpec((D_MODEL, tf), lambda i, f: (0, f)),
            pl.BlockSpec((tf, D_MODEL), lambda i, f: (f, 0)),
            pl.BlockSpec((1, D_MODEL), lambda i, f: (0, 0)),
        ],
        out_specs=[
            pl.BlockSpec((tm, D_MODEL), lambda i, f: (i, 0)),
            pl.BlockSpec((tm, D_MODEL), lambda i, f: (i, 0)),
        ],
        out_shape=[
            jax.ShapeDtypeStruct((T, D_MODEL), F32),
            jax.ShapeDtypeStruct((T, D_MODEL), BF16),
        ],
        scratch_shapes=[pltpu.VMEM((tm, D_MODEL), BF16)],
        compiler_params=_params("parallel", "arbitrary"),
        name="ffn",
    )(x, g_in, wg, wu, wd, g_out)


def _norm_rope(y, gain, cos, sin_lo, sin_hi):
    y = y * _rms_scale(y) * gain
    from_hi = pltpu.roll(y, HEAD_DIM - ROPE_HALF, 1)
    from_lo = pltpu.roll(y, ROPE_HALF, 1)
    return y * cos + from_hi * sin_lo + from_lo * sin_hi


def _dil_proj_kernel(h_ref, wq_ref, wk_ref, wv_ref, gq_ref, gk_ref, cos_ref, slo_ref, shi_ref,
                     o_ref, slab_ref, *, dilation):
    h = h_ref[...]
    tm = h.shape[0]
    rows = tm // dilation
    for s, w_ref in enumerate((wq_ref, wk_ref, wv_ref)):
        y = _dot(h, w_ref[...])
        for hh in range(DIL_HEADS):
            lanes = slice(hh * HEAD_DIM, (hh + 1) * HEAD_DIM)
            yh = y[:, lanes]
            if s < 2:
                gain = (gq_ref if s == 0 else gk_ref)[...]
                yh = _norm_rope(yh, gain, cos_ref[...], slo_ref[...], shi_ref[...])
            if dilation == 1:
                o_ref[s, 0, :, lanes] = yh.astype(BF16)
            else:
                slab_ref[hh] = yh
                for c in range(dilation):
                    o_ref[s, c, :, lanes] = slab_ref[hh, pl.ds(c, rows, stride=dilation), :].astype(BF16)


def _dil_proj(h, w_in, gq, gk, cos, slo, shi, *, group, batch, seq):
    r = DILATIONS[group]
    tm = PROJ_TM
    n_i = seq // tm
    wblk = DIL_GROUP_WIDTH
    n_groups = N_DIL_GROUPS

    def w_spec(s):
        return pl.BlockSpec((D_MODEL, wblk), lambda i: (0, s * n_groups + group))

    def tok_spec(width):
        return pl.BlockSpec((tm, width), lambda i: (i, 0))

    vec = pl.BlockSpec((1, HEAD_DIM), lambda i: (0, 0))
    return pl.pallas_call(
        functools.partial(_dil_proj_kernel, dilation=r),
        grid=(batch * n_i,),
        in_specs=[tok_spec(D_MODEL), w_spec(0), w_spec(1), w_spec(2), vec, vec,
                  tok_spec(HEAD_DIM), tok_spec(HEAD_DIM), tok_spec(HEAD_DIM)],
        out_specs=pl.BlockSpec((3, None, r, tm // r, wblk), lambda i: (0, i // n_i, 0, i % n_i, 0)),
        out_shape=jax.ShapeDtypeStruct((3, batch, r, seq // r, wblk), BF16),
        scratch_shapes=[pltpu.VMEM((DIL_HEADS, tm, HEAD_DIM), F32)],
        compiler_params=_params("parallel"),
        name=f"dil_proj{group}",
    )(h, w_in, w_in, w_in, gq, gk, cos, slo, shi)


PROJ2_TILES = (D_MODEL * 2 + 3 * DIFF_WIDTH) // PROJ2_TN
PROJ2_GATE_TILES = 2 * D_MODEL // PROJ2_TN


def _proj2_kernel(h_ref, w_ref, gq_ref, gk_ref, cos_ref, slo_ref, shi_ref, o_ref):
    j = pl.program_id(1)
    y = _dot(h_ref[...], w_ref[...])

    @pl.when(j < 2)
    def _():
        gain = jnp.where(j == 0, gq_ref[...], gk_ref[...])
        for hh in range(PROJ2_TN // HEAD_DIM):
            lanes = slice(hh * HEAD_DIM, (hh + 1) * HEAD_DIM)
            o_ref[:, lanes] = _norm_rope(y[:, lanes], gain, cos_ref[...], slo_ref[...],
                                         shi_ref[...]).astype(BF16)

    @pl.when(j == 2)
    def _():
        o_ref[...] = y.astype(BF16)

    @pl.when(j > 2)
    def _():
        o_ref[...] = jax.nn.sigmoid(y).astype(BF16)


def _proj2(h, w2, gq, gk, cos, slo, shi):
    T = h.shape[0]
    tm, tn = PROJ_TM, PROJ2_TN
    n_j = PROJ2_TILES
    vec = pl.BlockSpec((1, HEAD_DIM), lambda i, j: (0, 0))
    tok = pl.BlockSpec((tm, HEAD_DIM), lambda i, j: (i, 0))
    return pl.pallas_call(
        _proj2_kernel,
        grid=(T // tm, n_j),
        in_specs=[
            pl.BlockSpec((tm, D_MODEL), lambda i, j: (i, 0)),
            pl.BlockSpec((D_MODEL, tn), lambda i, j: (0, j)),
            vec, vec, tok, tok, tok,
        ],
        out_specs=pl.BlockSpec((tm, tn), lambda i, j: (i, (j + PROJ2_GATE_TILES) % n_j)),
        out_shape=jax.ShapeDtypeStruct((T, n_j * tn), BF16),
        compiler_params=_params("parallel", "arbitrary"),
        name="proj2",
    )(h, w2, gq, gk, cos, slo, shi)


def _dil_attn_kernel(*refs):
    ins, o_ref, scr = refs[:15], refs[15], refs[16:]
    i = pl.program_id(2)
    steps = DIL_BLOCK // BAND

    row = lax.broadcasted_iota(jnp.int32, (BAND, 2 * BAND), 0)
    col = lax.broadcasted_iota(jnp.int32, (BAND, 2 * BAND), 1)

    groups = []
    for g, r in enumerate(DILATIONS):
        q_ref, kc_ref, kp_ref, vc_ref, vp_ref = ins[5 * g:5 * g + 5]
        kx_ref, vx_ref, on_ref, ln_ref = scr[4 * g:4 * g + 4]
        kx_ref[:, :BAND, :] = kp_ref[...]
        kx_ref[:, BAND:, :] = kc_ref[...]
        vx_ref[:, :BAND, :] = vp_ref[...]
        vx_ref[:, BAND:, :] = vc_ref[...]
        groups.append((r, q_ref, kx_ref, vx_ref, on_ref, ln_ref))

    def body(t, carry):
        for r, q_ref, kx_ref, vx_ref, on_ref, ln_ref in groups:
            tiles = DIL_BLOCK // r // BAND
            if tiles == steps:
                c, mb = 0, t
            elif tiles == 1:
                c, mb = t, 0
            else:
                c, mb = t % r, t // r
            m0 = mb * BAND if isinstance(mb, int) else pl.multiple_of(mb * BAND, BAND)
            q = q_ref[c, pl.ds(m0, BAND), :]
            k = kx_ref[c, pl.ds(m0, 2 * BAND), :]
            v = vx_ref[c, pl.ds(m0, 2 * BAND), :]
            s = _dot_nt(q, k)
            lo = jnp.where((i * tiles + mb) == 0, BAND, 0)
            s = jnp.where((col >= jnp.maximum(row, lo)) & (col <= row + BAND), s, NEG_BIG)
            m = jnp.max(s, axis=-1, keepdims=True)
            p = jnp.exp(s - m)
            den = jnp.sum(p, axis=-1, keepdims=True)
            o = _dot(p.astype(BF16), v) / den
            lse = m + jnp.log(den)
            dst = pl.ds(mb * (BAND * r) + c, BAND, stride=r) if r > 1 else pl.ds(m0, BAND)
            on_ref[dst, :] = o
            ln_ref[dst, :] = jnp.broadcast_to(lse, (BAND, HEAD_DIM))
        return carry

    lax.fori_loop(0, steps, body, 0)

    lses = [grp[5][...] for grp in groups]
    top = jnp.maximum(jnp.maximum(lses[0], lses[1]), lses[2])
    ws = [jnp.exp(l - top) for l in lses]
    num = ws[0] * groups[0][4][...] + ws[1] * groups[1][4][...] + ws[2] * groups[2][4][...]
    o_ref[...] = (num / (ws[0] + ws[1] + ws[2])).astype(BF16)


def _dil_attn(qkvs, *, batch, seq):
    n_i = seq // DIL_BLOCK
    in_specs, scratch = [], []
    args = []
    for g, r in enumerate(DILATIONS):
        nq = DIL_BLOCK // r

        def cur(s, nq=nq, r=r):
            return pl.BlockSpec((None, None, r, nq, HEAD_DIM), lambda b, h, i: (s, b, 0, i, h))

        def prev(s, nq=nq, r=r):
            per = nq // BAND
            return pl.BlockSpec((None, None, r, BAND, HEAD_DIM),
                                lambda b, h, i: (s, b, 0, jnp.maximum(i * per - 1, 0), h))

        in_specs += [cur(0), cur(1), prev(1), cur(2), prev(2)]
        args += [qkvs[g]] * 5
        scratch += [pltpu.VMEM((r, BAND + nq, HEAD_DIM), BF16), pltpu.VMEM((r, BAND + nq, HEAD_DIM), BF16),
                    pltpu.VMEM((DIL_BLOCK, HEAD_DIM), F32), pltpu.VMEM((DIL_BLOCK, HEAD_DIM), F32)]
    return pl.pallas_call(
        _dil_attn_kernel,
        grid=(batch, DIL_HEADS, n_i),
        in_specs=in_specs,
        out_specs=pl.BlockSpec((None, DIL_BLOCK, HEAD_DIM), lambda b, h, i: (b, i, h)),
        out_shape=jax.ShapeDtypeStruct((batch, seq, DIL_GROUP_WIDTH), BF16),
        scratch_shapes=scratch,
        compiler_params=_params("parallel", "parallel", "arbitrary"),
        name="dil_attn",
    )(*args)


def _diff_attn_kernel(lq1_ref, lk1_ref, lq2_ref, lk2_ref, q_ref, k_ref, v_ref, sub_ref, o_ref,
                      acc_ref, m_ref, l_ref, *, lam_init):
    qi = pl.program_id(2)
    tq = q_ref.shape[0]
    q = q_ref[...]

    m_ref[...] = jnp.full(m_ref.shape, NEG_BIG, F32)
    l_ref[...] = jnp.zeros(l_ref.shape, F32)
    acc_ref[...] = jnp.zeros(acc_ref.shape, F32)

    row = lax.broadcasted_iota(jnp.int32, (tq, tq), 0)
    col = lax.broadcasted_iota(jnp.int32, (tq, tq), 1)
    causal = col <= row

    def block(kb, masked):
        k0 = pl.multiple_of(kb * tq, tq)
        k = k_ref[pl.ds(k0, tq), :]
        v = v_ref[pl.ds(k0, tq), :]
        for c in range(2):
            lanes = slice(c * HEAD_DIM, (c + 1) * HEAD_DIM)
            s = _dot_nt(q[:, lanes], k[:, lanes])
            if masked:
                s = jnp.where(causal, s, NEG_BIG)
            m_prev = m_ref[c][:, :1]
            m_new = jnp.maximum(m_prev, jnp.max(s, axis=-1, keepdims=True))
            alpha = jnp.exp(m_prev - m_new)
            p = jnp.exp(s - m_new)
            l_ref[c] = jnp.broadcast_to(alpha * l_ref[c][:, :1] + jnp.sum(p, axis=-1, keepdims=True),
                                        (tq, HEAD_DIM))
            m_ref[c] = jnp.broadcast_to(m_new, (tq, HEAD_DIM))
            acc_ref[c] = alpha * acc_ref[c] + _dot(p.astype(BF16), v)

    def body(kb, carry):
        block(kb, False)
        return carry

    lax.fori_loop(0, qi, body, 0)
    block(qi, True)

    lam = (jnp.exp(jnp.sum(lq1_ref[...] * lk1_ref[...], axis=-1, keepdims=True))
           - jnp.exp(jnp.sum(lq2_ref[...] * lk2_ref[...], axis=-1, keepdims=True)) + lam_init)
    o = acc_ref[0] / l_ref[0][:, :1] - lam * (acc_ref[1] / l_ref[1][:, :1])
    o_ref[...] = (o * _rms_scale(o) * sub_ref[...] * (1.0 - lam_init)).astype(BF16)


def _diff_attn(proj2, lq1, lk1, lq2, lk2, subln, *, batch, seq, lam_init):
    tq = DIFF_TQ
    q_blk = (2 * D_MODEL) // DIFF_HEAD_WIDTH
    k_blk = q_blk + DIFF_HEADS
    v_blk = k_blk + DIFF_HEADS
    vec = pl.BlockSpec((1, HEAD_DIM), lambda b, h, i: (0, 0))
    return pl.pallas_call(
        functools.partial(_diff_attn_kernel, lam_init=lam_init),
        grid=(batch, DIFF_HEADS, seq // tq),
        in_specs=[
            vec, vec, vec, vec,
            pl.BlockSpec((None, tq, DIFF_HEAD_WIDTH), lambda b, h, i: (b, i, q_blk + h)),
            pl.BlockSpec((None, seq, DIFF_HEAD_WIDTH), lambda b, h, i: (b, 0, k_blk + h)),
            pl.BlockSpec((None, seq, DIFF_HEAD_WIDTH), lambda b, h, i: (b, 0, v_blk + h)),
            pl.BlockSpec((1, DIFF_HEAD_WIDTH), lambda b, h, i: (0, 0)),
        ],
        out_specs=pl.BlockSpec((None, tq, DIFF_HEAD_WIDTH), lambda b, h, i: (b, i, h)),
        out_shape=jax.ShapeDtypeStruct((batch, seq, DIFF_WIDTH), BF16),
        scratch_shapes=[pltpu.VMEM((2, tq, DIFF_HEAD_WIDTH), F32),
                        pltpu.VMEM((2, tq, HEAD_DIM), F32),
                        pltpu.VMEM((2, tq, HEAD_DIM), F32)],
        compiler_params=_params("parallel", "parallel", "arbitrary"),
        name="diff_attn",
    )(lq1, lk1, lq2, lk2, proj2, proj2, proj2, subln)


def _merge_kernel(x_ref, od_ref, of_ref, gd_ref, gf_ref, wa_ref, wb_ref, wo_ref, o_ref):
    y = (gd_ref[...].astype(F32) * _dot(od_ref[...], wa_ref[...])
         + gf_ref[...].astype(F32) * _dot(of_ref[...], wb_ref[...]))
    o_ref[...] = x_ref[...] + _dot(y.astype(BF16), wo_ref[...])


def _merge(x, o_dil, o_diff, proj2, wa, wb, wo):
    T = x.shape[0]
    tm = MERGE_TM

    def tok(width, blk=0):
        return pl.BlockSpec((tm, width), lambda i: (i, blk))

    def resident(shape):
        return pl.BlockSpec(shape, lambda i: (0, 0), pipeline_mode=pl.Buffered(1))

    return pl.pallas_call(
        _merge_kernel,
        grid=(T // tm,),
        in_specs=[tok(D_MODEL), tok(DIL_GROUP_WIDTH), tok(DIFF_WIDTH), tok(D_MODEL, 0), tok(D_MODEL, 1),
                  resident(wa.shape), resident(wb.shape), resident(wo.shape)],
        out_specs=tok(D_MODEL),
        out_shape=jax.ShapeDtypeStruct((T, D_MODEL), F32),
        compiler_params=_params("parallel"),
        name="merge",
    )(x, o_dil, o_diff, proj2, proj2, wa, wb, wo)


def _rope_tables(positions):
    inv = ROPE_THETA ** (-jnp.arange(0, ROPE_DIM, 2, dtype=F32) / ROPE_DIM)
    ang = positions.astype(F32).reshape(-1, 1) * inv
    cos, sin = jnp.cos(ang), jnp.sin(ang)
    T = ang.shape[0]
    rest = HEAD_DIM - ROPE_DIM
    zeros = jnp.zeros((T, ROPE_HALF), F32)
    cos_t = jnp.concatenate([cos, cos, jnp.ones((T, rest), F32)], axis=-1)
    sin_lo = jnp.concatenate([-sin, zeros, jnp.zeros((T, rest), F32)], axis=-1)
    sin_hi = jnp.concatenate([zeros, sin, jnp.zeros((T, rest), F32)], axis=-1)
    return cos_t, sin_lo, sin_hi


def _layer(x, tables, layer, ffn1_norm, ffn1_w_gate, ffn1_w_up, ffn1_w_down, mix_norm, w_in,
           dil_q_norm, dil_k_norm, diff_q_norm, diff_k_norm, diff_lq1, diff_lk1, diff_lq2, diff_lk2,
           diff_subln, w_dil_branch, w_diff_branch, w_out, ffn2_norm, ffn2_w_gate, ffn2_w_up, ffn2_w_down,
           *, batch, seq):
    cos, slo, shi = tables
    row = lambda v: v.reshape(1, -1).astype(F32)
    bf = lambda w: w.astype(BF16)
    qk_scale = HEAD_DIM ** -0.5
    lam_init = 0.8 - 0.6 * math.exp(-0.3 * layer)

    x1, h = _ffn(x, row(ffn1_norm), bf(ffn1_w_gate), bf(ffn1_w_up), bf(ffn1_w_down), row(mix_norm))

    w_in_bf = bf(w_in)
    qkvs = [_dil_proj(h, w_in_bf, row(dil_q_norm) * qk_scale, row(dil_k_norm), cos, slo, shi,
                      group=g, batch=batch, seq=seq) for g in range(N_DIL_GROUPS)]
    proj2 = _proj2(h, w_in_bf[:, 3 * DIL_WIDTH:], row(diff_q_norm) * qk_scale, row(diff_k_norm),
                   cos, slo, shi)

    o_dil = _dil_attn(qkvs, batch=batch, seq=seq)
    o_diff = _diff_attn(proj2.reshape(batch, seq, -1), row(diff_lq1), row(diff_lk1), row(diff_lq2),
                        row(diff_lk2), row(diff_subln), batch=batch, seq=seq, lam_init=lam_init)

    x2 = _merge(x1, o_dil.reshape(batch * seq, -1), o_diff.reshape(batch * seq, -1), proj2,
                bf(w_dil_branch), bf(w_diff_branch), bf(w_out))
    out, _ = _ffn(x2, row(ffn2_norm), bf(ffn2_w_gate), bf(ffn2_w_up), bf(ffn2_w_down), row(ffn2_norm))
    return out


def kernel(x, positions, ffn1_norm, ffn1_w_gate, ffn1_w_up, ffn1_w_down, mix_norm, w_in, dil_q_norm, dil_k_norm, diff_q_norm, diff_k_norm, diff_lq1, diff_lk1, diff_lq2, diff_lk2, diff_subln, w_dil_branch, w_diff_branch, w_out, ffn2_norm, ffn2_w_gate, ffn2_w_up, ffn2_w_down):
    batch, seq, d_model = x.shape
    assert d_model == D_MODEL and seq % DIL_BLOCK == 0
    weights = (ffn1_norm, ffn1_w_gate, ffn1_w_up, ffn1_w_down, mix_norm, w_in, dil_q_norm, dil_k_norm,
               diff_q_norm, diff_k_norm, diff_lq1, diff_lk1, diff_lq2, diff_lk2, diff_subln, w_dil_branch,
               w_diff_branch, w_out, ffn2_norm, ffn2_w_gate, ffn2_w_up, ffn2_w_down)
    tables = _rope_tables(positions)
    y = x.reshape(batch * seq, d_model)
    for layer in range(ffn1_norm.shape[0]):
        y = _layer(y, tables, layer, *(w[layer] for w in weights), batch=batch, seq=seq)
    return y.reshape(batch, seq, d_model)
```

```python
import functools
import math

import jax
import jax.numpy as jnp
from jax import lax
from jax.experimental import pallas as pl
from jax.experimental.pallas import tpu as pltpu

F32 = jnp.float32
BF16 = jnp.bfloat16

D_MODEL = 2048
D_FF = 5632
HEAD_DIM = 128
ROPE_DIM = HEAD_DIM // 4
ROPE_HALF = ROPE_DIM // 2
ROPE_THETA = 500000.0
ROPE_LANE_GAP = HEAD_DIM // 2
RMS_EPS = 1e-6
BAND = 128
DILATIONS = (1, 4, 16)
N_DIL_GROUPS = len(DILATIONS)
DIL_HEADS = 4
DIL_GROUP_WIDTH = DIL_HEADS * HEAD_DIM
DIL_WIDTH = N_DIL_GROUPS * DIL_GROUP_WIDTH
DIFF_HEADS = 4
DIFF_HEAD_WIDTH = 2 * HEAD_DIM
DIFF_WIDTH = DIFF_HEADS * DIFF_HEAD_WIDTH
DIL_BLOCK = BAND * DILATIONS[-1]
NEG_BIG = -1e30

VMEM_LIMIT_BYTES = 56 * 1024 * 1024

FFN_TM = 512
FFN_TF = 512
PROJ_TM = 1024
PROJ_GATE_TM = 512
PROJ_CHUNK = 256
MERGE_TM = 512
DIFF_TQ = 512
DIL_ATTN_UNROLL = 8


def _params(*sem):
    return pltpu.CompilerParams(dimension_semantics=sem, vmem_limit_bytes=VMEM_LIMIT_BYTES)


def _rms_scale(x):
    return lax.rsqrt(jnp.mean(x * x, axis=-1, keepdims=True) + RMS_EPS)


def _dot(a, b):
    return jnp.dot(a, b, preferred_element_type=F32)


def _dot_nt(a, b):
    return lax.dot_general(a, b, (((1,), (1,)), ((), ())), preferred_element_type=F32)


def _ffn_kernel(*refs, emit_norm):
    if emit_norm:
        x_ref, gin_ref, wg_ref, wu_ref, wd_ref, gout_ref, o_ref, hn_ref, h_ref = refs
    else:
        x_ref, gin_ref, wg_ref, wu_ref, wd_ref, o_ref, h_ref = refs
    f = pl.program_id(1)

    @pl.when(f == 0)
    def _():
        x = x_ref[...]
        h_ref[...] = (x * _rms_scale(x) * gin_ref[...]).astype(BF16)
        o_ref[...] = x

    h = h_ref[...]
    g = _dot(h, wg_ref[...])
    u = _dot(h, wu_ref[...])
    a = (g * jax.nn.sigmoid(g)) * u * 0.5
    o_ref[...] += _dot(a.astype(BF16), wd_ref[...])

    if emit_norm:
        @pl.when(f == pl.num_programs(1) - 1)
        def _():
            y = o_ref[...]
            hn_ref[...] = (y * _rms_scale(y) * gout_ref[...]).astype(BF16)


def _ffn(x, g_in, wg, wu, wd, g_out):
    T = x.shape[0]
    tm, tf = FFN_TM, FFN_TF
    emit_norm = g_out is not None
    tok = pl.BlockSpec((tm, D_MODEL), lambda i, f: (i, 0))
    vec = pl.BlockSpec((1, D_MODEL), lambda i, f: (0, 0))
    in_specs = [tok, vec,
                pl.BlockSpec((D_MODEL, tf), lambda i, f: (0, f)),
                pl.BlockSpec((D_MODEL, tf), lambda i, f: (0, f)),
                pl.BlockSpec((tf, D_MODEL), lambda i, f: (f, 0))]
    args = [x, g_in, wg, wu, wd]
    out_specs = [tok]
    out_shape = [jax.ShapeDtypeStruct((T, D_MODEL), F32)]
    if emit_norm:
        in_specs.append(vec)
        args.append(g_out)
        out_specs.append(tok)
        out_shape.append(jax.ShapeDtypeStruct((T, D_MODEL), BF16))
    return pl.pallas_call(
        functools.partial(_ffn_kernel, emit_norm=emit_norm),
        grid=(T // tm, D_FF // tf),
        in_specs=in_specs,
        out_specs=out_specs,
        out_shape=out_shape,
        scratch_shapes=[pltpu.VMEM((tm, D_MODEL), BF16)],
        compiler_params=_params("parallel", "arbitrary"),
        name="ffn",
    )(*args)


QUERY, KEY, PLAIN, GATE = "query", "key", "plain", "gate"


def _sigmoid(y):
    return 0.5 * jnp.tanh(0.5 * y) + 0.5


def _norm_rope(y, gain, cos, sin):
    y = y * _rms_scale(y) * gain
    return y * cos + pltpu.roll(y, ROPE_LANE_GAP, 1) * sin


def _proj_kernel(h_ref, w_ref, gain_ref, cos_ref, sin_ref, o_ref, y_ref, *slab, kinds, chunk, dilation):
    tm = h_ref.shape[0]
    n_chunks = tm // chunk

    def compute(m, slot):
        y = _dot(h_ref[pl.ds(m * chunk, chunk), :], w_ref[...])
        for hh in range(len(kinds)):
            y_ref[slot, hh] = y[:, hh * HEAD_DIM:(hh + 1) * HEAD_DIM]

    def finish(m, slot):
        rows = pl.ds(m * chunk, chunk)
        for hh, kind in enumerate(kinds):
            y = y_ref[slot, hh]
            if kind in (QUERY, KEY):
                gain = gain_ref[0:1, :] if kind == QUERY else gain_ref[1:2, :]
                y = _norm_rope(y, gain, cos_ref[rows, :], sin_ref[rows, :])
            elif kind == GATE:
                y = _sigmoid(y)
            if dilation is None:
                o_ref[rows, hh * HEAD_DIM:(hh + 1) * HEAD_DIM] = y.astype(BF16)
            else:
                sec, lanes = divmod(hh * HEAD_DIM, DIL_GROUP_WIDTH)
                lanes = slice(lanes, lanes + HEAD_DIM)
                n = chunk // dilation
                dst = pl.ds(m * n, n)
                if dilation == 1:
                    o_ref[sec, 0, dst, lanes] = y.astype(BF16)
                else:
                    slab_ref = slab[0]
                    slab_ref[hh] = y
                    for c in range(dilation):
                        o_ref[sec, c, dst, lanes] = slab_ref[hh, pl.ds(c, n, stride=dilation), :].astype(BF16)

    compute(0, 0)
    for m in range(n_chunks - 1):
        finish(m, m % 2)
        compute(m + 1, (m + 1) % 2)
    finish(n_chunks - 1, (n_chunks - 1) % 2)


def _proj(h, w, gains, cos, sin, *, kinds, tm, name, dil=None):
    T = h.shape[0]
    width = len(kinds) * HEAD_DIM
    assert w.shape == (D_MODEL, width)
    chunk = PROJ_CHUNK
    scratch = [pltpu.VMEM((2, len(kinds), chunk, HEAD_DIM), F32)]
    if dil is None:
        dilation = None
        out_spec = pl.BlockSpec((tm, width), lambda i: (i, 0))
        out_shape = jax.ShapeDtypeStruct((T, width), BF16)
    else:
        dilation, batch, seq = dil
        n_i = seq // tm
        out_spec = pl.BlockSpec((3, None, dilation, tm // dilation, DIL_GROUP_WIDTH),
                                lambda i: (0, i // n_i, 0, i % n_i, 0))
        out_shape = jax.ShapeDtypeStruct((3, batch, dilation, seq // dilation, DIL_GROUP_WIDTH), BF16)
        if dilation > 1:
            scratch.append(pltpu.VMEM((len(kinds), chunk, HEAD_DIM), F32))
    tok = lambda cols: pl.BlockSpec((tm, cols), lambda i: (i, 0))
    return pl.pallas_call(
        functools.partial(_proj_kernel, kinds=kinds, chunk=chunk, dilation=dilation),
        grid=(T // tm,),
        in_specs=[tok(D_MODEL),
                  pl.BlockSpec((D_MODEL, width), lambda i: (0, 0), pipeline_mode=pl.Buffered(1)),
                  pl.BlockSpec((2, HEAD_DIM), lambda i: (0, 0)),
                  tok(HEAD_DIM), tok(HEAD_DIM)],
        out_specs=out_spec,
        out_shape=out_shape,
        scratch_shapes=scratch,
        compiler_params=_params("parallel"),
        name=name,
    )(h, w, gains, cos, sin)


def _dil_attn_kernel(*refs):
    ins, o_ref, scr = refs[:15], refs[15], refs[16:]
    i = pl.program_id(2)
    steps = DIL_BLOCK // BAND

    row = lax.broadcasted_iota(jnp.int32, (BAND, 2 * BAND), 0)
    col = lax.broadcasted_iota(jnp.int32, (BAND, 2 * BAND), 1)
    band_bias = jnp.where((col >= row) & (col <= row + BAND), 0.0, NEG_BIG)
    start_bias = jnp.where(i == 0, jnp.where(col >= BAND, band_bias, NEG_BIG), band_bias)

    groups = []
    for g, r in enumerate(DILATIONS):
        q_ref, kc_ref, kp_ref, vc_ref, vp_ref = ins[5 * g:5 * g + 5]
        kx_ref, vx_ref, on_ref, ln_ref = scr[4 * g:4 * g + 4]
        kx_ref[:, :BAND, :] = kp_ref[...]
        kx_ref[:, BAND:, :] = kc_ref[...]
        vx_ref[:, :BAND, :HEAD_DIM] = vp_ref[...]
        vx_ref[:, BAND:, :HEAD_DIM] = vc_ref[...]
        vx_ref[:, :, HEAD_DIM:] = jnp.ones((r, vx_ref.shape[1], HEAD_DIM), BF16)
        groups.append((r, q_ref, kx_ref, vx_ref, on_ref, ln_ref))

    def body(t, carry):
        for r, q_ref, kx_ref, vx_ref, on_ref, ln_ref in groups:
            tiles = DIL_BLOCK // r // BAND
            if tiles == steps:
                c, mb = 0, t
            elif tiles == 1:
                c, mb = t, 0
            else:
                c, mb = t % r, t // r
            m0 = mb * BAND if isinstance(mb, int) else pl.multiple_of(mb * BAND, BAND)
            q = q_ref[c, pl.ds(m0, BAND), :]
            k = kx_ref[c, pl.ds(m0, 2 * BAND), :]
            v = vx_ref[c, pl.ds(m0, 2 * BAND), :]
            if isinstance(mb, int):
                bias = start_bias if mb == 0 else band_bias
            else:
                bias = jnp.where(mb == 0, start_bias, band_bias)
            s = _dot_nt(q, k) + bias
            m = jnp.max(s, axis=-1, keepdims=True)
            p = jnp.exp(s - m)
            od = _dot(p.astype(BF16), v)
            den = od[:, HEAD_DIM:]
            dst = pl.ds(mb * (BAND * r) + c, BAND, stride=r) if r > 1 else pl.ds(m0, BAND)
            on_ref[dst, :] = od[:, :HEAD_DIM] / den
            ln_ref[dst, :] = m + jnp.log(den)
        return carry

    lax.fori_loop(0, steps, body, 0, unroll=DIL_ATTN_UNROLL)

    lses = [grp[5][...] for grp in groups]
    top = jnp.maximum(jnp.maximum(lses[0], lses[1]), lses[2])
    ws = [jnp.exp(l - top) for l in lses]
    num = ws[0] * groups[0][4][...] + ws[1] * groups[1][4][...] + ws[2] * groups[2][4][...]
    o_ref[...] = (num / (ws[0] + ws[1] + ws[2])).astype(BF16)


def _dil_attn(qkvs, *, batch, seq):
    n_i = seq // DIL_BLOCK
    in_specs, scratch = [], []
    args = []
    for g, r in enumerate(DILATIONS):
        nq = DIL_BLOCK // r

        def cur(s, nq=nq, r=r):
            return pl.BlockSpec((None, None, r, nq, HEAD_DIM), lambda b, h, i: (s, b, 0, i, h))

        def prev(s, nq=nq, r=r):
            per = nq // BAND
            return pl.BlockSpec((None, None, r, BAND, HEAD_DIM),
                                lambda b, h, i: (s, b, 0, jnp.maximum(i * per - 1, 0), h))

        in_specs += [cur(0), cur(1), prev(1), cur(2), prev(2)]
        args += [qkvs[g]] * 5
        scratch += [pltpu.VMEM((r, BAND + nq, HEAD_DIM), BF16), pltpu.VMEM((r, BAND + nq, 2 * HEAD_DIM), BF16),
                    pltpu.VMEM((DIL_BLOCK, HEAD_DIM), F32), pltpu.VMEM((DIL_BLOCK, HEAD_DIM), F32)]
    return pl.pallas_call(
        _dil_attn_kernel,
        grid=(batch, DIL_HEADS, n_i),
        in_specs=in_specs,
        out_specs=pl.BlockSpec((None, DIL_BLOCK, HEAD_DIM), lambda b, h, i: (b, i, h)),
        out_shape=jax.ShapeDtypeStruct((batch, seq, DIL_GROUP_WIDTH), BF16),
        scratch_shapes=scratch,
        compiler_params=_params("parallel", "parallel", "arbitrary"),
        name="dil_attn",
    )(*args)


def _diff_attn_kernel(lq1_ref, lk1_ref, lq2_ref, lk2_ref, q_ref, k_ref, v_ref, sub_ref, o_ref,
                      vt_ref, acc_ref, *, lam_init):
    qi = pl.program_id(2)
    tq = q_ref.shape[0]

    @pl.when(qi == 0)
    def _():
        for ch in range(vt_ref.shape[0]):
            vt_ref[ch] = v_ref[ch * tq:(ch + 1) * tq, :].astype(F32).T.astype(BF16)

    q = q_ref[...]
    acc_ref[...] = jnp.zeros(acc_ref.shape, F32)
    key = lax.broadcasted_iota(jnp.int32, (tq, tq), 0)
    qry = lax.broadcasted_iota(jnp.int32, (tq, tq), 1)
    causal = key <= qry

    def block(kb, stats, masked):
        k = k_ref[pl.ds(pl.multiple_of(kb * tq, tq), tq), :]
        vt = vt_ref[kb]
        out = []
        for c in range(2):
            m_prev, l_prev = stats[2 * c], stats[2 * c + 1]
            lanes = slice(c * HEAD_DIM, (c + 1) * HEAD_DIM)
            s = _dot_nt(k[:, lanes], q[:, lanes])
            if masked:
                s = jnp.where(causal, s, NEG_BIG)
            m_new = jnp.maximum(m_prev, jnp.max(s, axis=0, keepdims=True))
            alpha = jnp.exp(m_prev - m_new)
            p = jnp.exp(s - m_new)
            out += [m_new, alpha * l_prev + jnp.sum(p, axis=0, keepdims=True)]
            acc_ref[c] = alpha * acc_ref[c] + _dot(vt, p.astype(BF16))
        return tuple(out)

    row0 = jnp.full((1, tq), NEG_BIG, F32)
    zero = jnp.zeros((1, tq), F32)
    stats = lax.fori_loop(0, qi, lambda kb, st: block(kb, st, False), (row0, zero, row0, zero))
    _, l1, _, l2 = block(qi, stats, True)

    lam = (jnp.exp(jnp.sum(lq1_ref[...] * lk1_ref[...], axis=-1, keepdims=True))
           - jnp.exp(jnp.sum(lq2_ref[...] * lk2_ref[...], axis=-1, keepdims=True)) + lam_init)
    o = (acc_ref[0] / l1 - lam * (acc_ref[1] / l2)).T
    o_ref[...] = (o * _rms_scale(o) * sub_ref[...] * (1.0 - lam_init)).astype(BF16)


def _diff_attn(qk, gv, lq1, lk1, lq2, lk2, subln, *, batch, seq, lam_init):
    tq = DIFF_TQ
    q_blk = 0
    k_blk = DIFF_HEADS
    v_blk = (2 * D_MODEL) // DIFF_HEAD_WIDTH
    vec = pl.BlockSpec((1, HEAD_DIM), lambda b, h, i: (0, 0))
    return pl.pallas_call(
        functools.partial(_diff_attn_kernel, lam_init=lam_init),
        grid=(batch, DIFF_HEADS, seq // tq),
        in_specs=[
            vec, vec, vec, vec,
            pl.BlockSpec((None, tq, DIFF_HEAD_WIDTH), lambda b, h, i: (b, i, q_blk + h)),
            pl.BlockSpec((None, seq, DIFF_HEAD_WIDTH), lambda b, h, i: (b, 0, k_blk + h)),
            pl.BlockSpec((None, seq, DIFF_HEAD_WIDTH), lambda b, h, i: (b, 0, v_blk + h)),
            pl.BlockSpec((1, DIFF_HEAD_WIDTH), lambda b, h, i: (0, 0)),
        ],
        out_specs=pl.BlockSpec((None, tq, DIFF_HEAD_WIDTH), lambda b, h, i: (b, i, h)),
        out_shape=jax.ShapeDtypeStruct((batch, seq, DIFF_WIDTH), BF16),
        scratch_shapes=[pltpu.VMEM((seq // tq, DIFF_HEAD_WIDTH, tq), BF16),
                        pltpu.VMEM((2, DIFF_HEAD_WIDTH, tq), F32)],
        compiler_params=_params("parallel", "parallel", "arbitrary"),
        name="diff_attn",
    )(lq1, lk1, lq2, lk2, qk, qk, gv, subln)


def _merge_kernel(x_ref, od_ref, of_ref, gd_ref, gf_ref, wa_ref, wb_ref, wo_ref, o_ref):
    y = (gd_ref[...].astype(F32) * _dot(od_ref[...], wa_ref[...])
         + gf_ref[...].astype(F32) * _dot(of_ref[...], wb_ref[...]))
    o_ref[...] = x_ref[...] + _dot(y.astype(BF16), wo_ref[...])


def _merge(x, o_dil, o_diff, gv, wa, wb, wo):
    T = x.shape[0]
    tm = MERGE_TM

    def tok(width, blk=0):
        return pl.BlockSpec((tm, width), lambda i: (i, blk))

    def resident(shape):
        return pl.BlockSpec(shape, lambda i: (0, 0), pipeline_mode=pl.Buffered(1))

    return pl.pallas_call(
        _merge_kernel,
        grid=(T // tm,),
        in_specs=[tok(D_MODEL), tok(DIL_GROUP_WIDTH), tok(DIFF_WIDTH), tok(D_MODEL, 0), tok(D_MODEL, 1),
                  resident(wa.shape), resident(wb.shape), resident(wo.shape)],
        out_specs=tok(D_MODEL),
        out_shape=jax.ShapeDtypeStruct((T, D_MODEL), F32),
        compiler_params=_params("parallel"),
        name="merge",
    )(x, o_dil, o_diff, gv, gv, wa, wb, wo)


def _rope_perm(w):
    return jnp.concatenate([w[..., :ROPE_HALF], w[..., ROPE_DIM:ROPE_LANE_GAP + ROPE_HALF],
                            w[..., ROPE_HALF:ROPE_DIM], w[..., ROPE_LANE_GAP + ROPE_HALF:]], axis=-1)


def _rope_tables(positions):
    inv = ROPE_THETA ** (-jnp.arange(0, ROPE_DIM, 2, dtype=F32) / ROPE_DIM)
    ang = positions.astype(F32).reshape(-1, 1) * inv
    cos, sin = jnp.cos(ang), jnp.sin(ang)
    T = ang.shape[0]
    gap = ROPE_LANE_GAP - ROPE_HALF
    ones, zeros = jnp.ones((T, gap), F32), jnp.zeros((T, gap), F32)
    cos_t = jnp.concatenate([cos, ones, cos, ones], axis=-1)
    sin_t = jnp.concatenate([-sin, zeros, sin, zeros], axis=-1)
    return cos_t, sin_t


def _layer(x, tables, layer, ffn1_norm, ffn1_w_gate, ffn1_w_up, ffn1_w_down, mix_norm, w_in,
           dil_q_norm, dil_k_norm, diff_q_norm, diff_k_norm, diff_lq1, diff_lk1, diff_lq2, diff_lk2,
           diff_subln, w_dil_branch, w_diff_branch, w_out, ffn2_norm, ffn2_w_gate, ffn2_w_up, ffn2_w_down,
           *, batch, seq):
    cos, sin = tables
    row = lambda v: v.reshape(1, -1).astype(F32)
    bf = lambda w: w.astype(BF16)
    qk_scale = HEAD_DIM ** -0.5
    lam_init = 0.8 - 0.6 * math.exp(-0.3 * layer)

    x1, h = _ffn(x, row(ffn1_norm), bf(ffn1_w_gate), bf(ffn1_w_up), bf(ffn1_w_down), row(mix_norm))

    heads = w_in.reshape(D_MODEL, -1, HEAD_DIM)
    n_dil = N_DIL_GROUPS * DIL_HEADS
    n_diff = DIFF_WIDTH // HEAD_DIM
    diff0 = 3 * n_dil
    gate0 = diff0 + 3 * n_diff
    cols = lambda *parts: bf(jnp.concatenate(parts, axis=1).reshape(D_MODEL, -1))
    gains = lambda gq, gk: jnp.stack([_rope_perm(gq) * qk_scale, _rope_perm(gk)]).astype(F32)

    dil_kinds = (QUERY,) * DIL_HEADS + (KEY,) * DIL_HEADS + (PLAIN,) * DIL_HEADS
    dil_gains = gains(dil_q_norm, dil_k_norm)
    qkvs = []
    for g, r in enumerate(DILATIONS):
        sel = lambda sec: heads[:, sec * n_dil + g * DIL_HEADS:sec * n_dil + (g + 1) * DIL_HEADS]
        w_g = cols(_rope_perm(sel(0)), _rope_perm(sel(1)), sel(2))
        qkvs.append(_proj(h, w_g, dil_gains, cos, sin, kinds=dil_kinds, tm=PROJ_TM, name=f"dil_proj{g}",
                          dil=(r, batch, seq)))

    qk = _proj(h, cols(_rope_perm(heads[:, diff0:diff0 + 2 * n_diff])), gains(diff_q_norm, diff_k_norm),
               cos, sin, kinds=(QUERY,) * n_diff + (KEY,) * n_diff, tm=PROJ_TM, name="diff_qk_proj")
    n_gate = 2 * D_MODEL // HEAD_DIM
    gv = _proj(h, cols(heads[:, gate0:], heads[:, diff0 + 2 * n_diff:gate0]), dil_gains, cos, sin,
               kinds=(GATE,) * n_gate + (PLAIN,) * n_diff, tm=PROJ_GATE_TM, name="gate_v_proj")

    o_dil = _dil_attn(qkvs, batch=batch, seq=seq)
    o_diff = _diff_attn(qk.reshape(batch, seq, -1), gv.reshape(batch, seq, -1), row(diff_lq1), row(diff_lk1),
                        row(diff_lq2), row(diff_lk2), row(diff_subln), batch=batch, seq=seq, lam_init=lam_init)

    x2 = _merge(x1, o_dil.reshape(batch * seq, -1), o_diff.reshape(batch * seq, -1), gv,
                bf(w_dil_branch), bf(w_diff_branch), bf(w_out))
    (out,) = _ffn(x2, row(ffn2_norm), bf(ffn2_w_gate), bf(ffn2_w_up), bf(ffn2_w_down), None)
    return out


def kernel(x, positions, ffn1_norm, ffn1_w_gate, ffn1_w_up, ffn1_w_down, mix_norm, w_in, dil_q_norm, dil_k_norm, diff_q_norm, diff_k_norm, diff_lq1, diff_lk1, diff_lq2, diff_lk2, diff_subln, w_dil_branch, w_diff_branch, w_out, ffn2_norm, ffn2_w_gate, ffn2_w_up, ffn2_w_down):
    batch, seq, d_model = x.shape
    assert d_model == D_MODEL and seq % DIL_BLOCK == 0
    weights = (ffn1_norm, ffn1_w_gate, ffn1_w_up, ffn1_w_down, mix_norm, w_in, dil_q_norm, dil_k_norm,
               diff_q_norm, diff_k_norm, diff_lq1, diff_lk1, diff_lq2, diff_lk2, diff_subln, w_dil_branch,
               w_diff_branch, w_out, ffn2_norm, ffn2_w_gate, ffn2_w_up, ffn2_w_down)
    tables = _rope_tables(positions)
    y = x.reshape(batch * seq, d_model)
    for layer in range(ffn1_norm.shape[0]):
        y = _layer(y, tables, layer, *(w[layer] for w in weights), batch=batch, seq=seq)
    return y.reshape(batch, seq, d_model)
```

```python
import functools
import math

import jax
import jax.numpy as jnp
from jax import lax
from jax.experimental import pallas as pl
from jax.experimental.pallas import tpu as pltpu

F32 = jnp.float32
BF16 = jnp.bfloat16

D_MODEL = 2048
D_FF = 5632
HEAD_DIM = 128
ROPE_DIM = HEAD_DIM // 4
ROPE_HALF = ROPE_DIM // 2
ROPE_THETA = 500000.0
RMS_EPS = 1e-6
BAND = 128
DILATIONS = (1, 4, 16)
N_DIL_GROUPS = len(DILATIONS)
DIL_HEADS = 4
DIL_GROUP_WIDTH = DIL_HEADS * HEAD_DIM
DIL_WIDTH = N_DIL_GROUPS * DIL_GROUP_WIDTH
DIFF_HEADS = 4
DIFF_HEAD_WIDTH = 2 * HEAD_DIM
DIFF_WIDTH = DIFF_HEADS * DIFF_HEAD_WIDTH
DIL_BLOCK = BAND * DILATIONS[-1]
NEG_BIG = -1e30

VMEM_LIMIT_BYTES = 56 * 1024 * 1024

FFN_TM = 512
FFN_TF = 512
PROJ_TM = 1024
PROJ_GATE_TM = 512
PROJ_CHUNK = 256
MERGE_TM = 512
DIFF_TQ = 1024
DIFF_TK = 512
DIL_ATTN_UNROLL = 8


def _params(*sem):
    return pltpu.CompilerParams(dimension_semantics=sem, vmem_limit_bytes=VMEM_LIMIT_BYTES)


def _rms_scale(x):
    return lax.rsqrt(jnp.mean(x * x, axis=-1, keepdims=True) + RMS_EPS)


def _dot(a, b):
    return jnp.dot(a, b, preferred_element_type=F32)


def _dot_nt(a, b):
    return lax.dot_general(a, b, (((1,), (1,)), ((), ())), preferred_element_type=F32)


def _ffn_kernel(*refs, emit_norm):
    if emit_norm:
        x_ref, gin_ref, wg_ref, wu_ref, wd_ref, gout_ref, o_ref, hn_ref, h_ref = refs
    else:
        x_ref, gin_ref, wg_ref, wu_ref, wd_ref, o_ref, h_ref = refs
    f = pl.program_id(1)

    @pl.when(f == 0)
    def _():
        x = x_ref[...]
        h_ref[...] = (x * _rms_scale(x) * gin_ref[...]).astype(BF16)
        o_ref[...] = x

    h = h_ref[...]
    g = _dot(h, wg_ref[...])
    u = _dot(h, wu_ref[...])
    a = (g * jax.nn.sigmoid(g)) * u * 0.5
    o_ref[...] += _dot(a.astype(BF16), wd_ref[...])

    if emit_norm:
        @pl.when(f == pl.num_programs(1) - 1)
        def _():
            y = o_ref[...]
            hn_ref[...] = (y * _rms_scale(y) * gout_ref[...]).astype(BF16)


def _ffn(x, g_in, wg, wu, wd, g_out):
    T = x.shape[0]
    tm, tf = FFN_TM, FFN_TF
    emit_norm = g_out is not None
    tok = pl.BlockSpec((tm, D_MODEL), lambda i, f: (i, 0))
    vec = pl.BlockSpec((1, D_MODEL), lambda i, f: (0, 0))
    in_specs = [tok, vec,
                pl.BlockSpec((D_MODEL, tf), lambda i, f: (0, f)),
                pl.BlockSpec((D_MODEL, tf), lambda i, f: (0, f)),
                pl.BlockSpec((tf, D_MODEL), lambda i, f: (f, 0))]
    args = [x, g_in, wg, wu, wd]
    out_specs = [tok]
    out_shape = [jax.ShapeDtypeStruct((T, D_MODEL), F32)]
    if emit_norm:
        in_specs.append(vec)
        args.append(g_out)
        out_specs.append(tok)
        out_shape.append(jax.ShapeDtypeStruct((T, D_MODEL), BF16))
    return pl.pallas_call(
        functools.partial(_ffn_kernel, emit_norm=emit_norm),
        grid=(T // tm, D_FF // tf),
        in_specs=in_specs,
        out_specs=out_specs,
        out_shape=out_shape,
        scratch_shapes=[pltpu.VMEM((tm, D_MODEL), BF16)],
        compiler_params=_params("parallel", "arbitrary"),
        name="ffn",
    )(*args)


QUERY, KEY, PLAIN, GATE = "query", "key", "plain", "gate"


def _sigmoid(y):
    return 0.5 * jnp.tanh(0.5 * y) + 0.5


def _norm_rope(y, gain, cos, sin):
    y = y * _rms_scale(y) * gain
    lane = lax.broadcasted_iota(jnp.int32, y.shape, 1)
    partner = jnp.where(lane < ROPE_HALF, pltpu.roll(y, HEAD_DIM - ROPE_HALF, 1), pltpu.roll(y, ROPE_HALF, 1))
    return y * cos + partner * sin


def _proj_kernel(h_ref, w_ref, gain_ref, cos_ref, sin_ref, o_ref, y_ref, *slab, kinds, chunk, dilation):
    tm = h_ref.shape[0]
    n_chunks = tm // chunk

    def compute(m, slot):
        y = _dot(h_ref[pl.ds(m * chunk, chunk), :], w_ref[...])
        for hh in range(len(kinds)):
            y_ref[slot, hh] = y[:, hh * HEAD_DIM:(hh + 1) * HEAD_DIM]

    def finish(m, slot):
        rows = pl.ds(m * chunk, chunk)
        for hh, kind in enumerate(kinds):
            y = y_ref[slot, hh]
            if kind in (QUERY, KEY):
                gain = gain_ref[0:1, :] if kind == QUERY else gain_ref[1:2, :]
                y = _norm_rope(y, gain, cos_ref[rows, :], sin_ref[rows, :])
            elif kind == GATE:
                y = _sigmoid(y)
            if dilation is None:
                o_ref[rows, hh * HEAD_DIM:(hh + 1) * HEAD_DIM] = y.astype(BF16)
            else:
                sec, lanes = divmod(hh * HEAD_DIM, DIL_GROUP_WIDTH)
                lanes = slice(lanes, lanes + HEAD_DIM)
                n = chunk // dilation
                dst = pl.ds(m * n, n)
                if dilation == 1:
                    o_ref[sec, 0, dst, lanes] = y.astype(BF16)
                else:
                    slab_ref = slab[0]
                    slab_ref[hh] = y
                    for c in range(dilation):
                        o_ref[sec, c, dst, lanes] = slab_ref[hh, pl.ds(c, n, stride=dilation), :].astype(BF16)

    compute(0, 0)
    for m in range(n_chunks - 1):
        finish(m, m % 2)
        compute(m + 1, (m + 1) % 2)
    finish(n_chunks - 1, (n_chunks - 1) % 2)


def _proj(h, w, gains, cos, sin, *, kinds, tm, name, dil=None):
    T = h.shape[0]
    width = len(kinds) * HEAD_DIM
    assert w.shape == (D_MODEL, width)
    chunk = PROJ_CHUNK
    scratch = [pltpu.VMEM((2, len(kinds), chunk, HEAD_DIM), F32)]
    if dil is None:
        dilation = None
        out_spec = pl.BlockSpec((tm, width), lambda i: (i, 0))
        out_shape = jax.ShapeDtypeStruct((T, width), BF16)
    else:
        dilation, batch, seq = dil
        n_i = seq // tm
        out_spec = pl.BlockSpec((3, None, dilation, tm // dilation, DIL_GROUP_WIDTH),
                                lambda i: (0, i // n_i, 0, i % n_i, 0))
        out_shape = jax.ShapeDtypeStruct((3, batch, dilation, seq // dilation, DIL_GROUP_WIDTH), BF16)
        if dilation > 1:
            scratch.append(pltpu.VMEM((len(kinds), chunk, HEAD_DIM), F32))
    tok = lambda cols: pl.BlockSpec((tm, cols), lambda i: (i, 0))
    return pl.pallas_call(
        functools.partial(_proj_kernel, kinds=kinds, chunk=chunk, dilation=dilation),
        grid=(T // tm,),
        in_specs=[tok(D_MODEL),
                  pl.BlockSpec((D_MODEL, width), lambda i: (0, 0), pipeline_mode=pl.Buffered(1)),
                  pl.BlockSpec((2, HEAD_DIM), lambda i: (0, 0)),
                  tok(HEAD_DIM), tok(HEAD_DIM)],
        out_specs=out_spec,
        out_shape=out_shape,
        scratch_shapes=scratch,
        compiler_params=_params("parallel"),
        name=name,
    )(h, w, gains, cos, sin)


def _dil_attn_kernel(*refs):
    ins, o_ref, scr = refs[:15], refs[15], refs[16:]
    i = pl.program_id(2)
    steps = DIL_BLOCK // BAND

    row = lax.broadcasted_iota(jnp.int32, (BAND, 2 * BAND), 0)
    col = lax.broadcasted_iota(jnp.int32, (BAND, 2 * BAND), 1)
    band_bias = jnp.where((col >= row) & (col <= row + BAND), 0.0, NEG_BIG)
    start_bias = jnp.where(i == 0, jnp.where(col >= BAND, band_bias, NEG_BIG), band_bias)

    groups = []
    for g, r in enumerate(DILATIONS):
        q_ref, kc_ref, kp_ref, vc_ref, vp_ref = ins[5 * g:5 * g + 5]
        kx_ref, vx_ref, on_ref, ln_ref = scr[4 * g:4 * g + 4]
        kx_ref[:, :BAND, :] = kp_ref[...]
        kx_ref[:, BAND:, :] = kc_ref[...]
        vx_ref[:, :BAND, :HEAD_DIM] = vp_ref[...]
        vx_ref[:, BAND:, :HEAD_DIM] = vc_ref[...]
        vx_ref[:, :, HEAD_DIM:] = jnp.ones((r, vx_ref.shape[1], HEAD_DIM), BF16)
        groups.append((r, q_ref, kx_ref, vx_ref, on_ref, ln_ref))

    def body(t, carry):
        for r, q_ref, kx_ref, vx_ref, on_ref, ln_ref in groups:
            tiles = DIL_BLOCK // r // BAND
            if tiles == steps:
                c, mb = 0, t
            elif tiles == 1:
                c, mb = t, 0
            else:
                c, mb = t % r, t // r
            m0 = mb * BAND if isinstance(mb, int) else pl.multiple_of(mb * BAND, BAND)
            q = q_ref[c, pl.ds(m0, BAND), :]
            k = kx_ref[c, pl.ds(m0, 2 * BAND), :]
            v = vx_ref[c, pl.ds(m0, 2 * BAND), :]
            if isinstance(mb, int):
                bias = start_bias if mb == 0 else band_bias
            else:
                bias = jnp.where(mb == 0, start_bias, band_bias)
            s = _dot_nt(q, k) + bias
            m = jnp.max(s, axis=-1, keepdims=True)
            p = jnp.exp(s - m)
            od = _dot(p.astype(BF16), v)
            den = od[:, HEAD_DIM:]
            dst = pl.ds(mb * (BAND * r) + c, BAND, stride=r) if r > 1 else pl.ds(m0, BAND)
            on_ref[dst, :] = od[:, :HEAD_DIM] / den
            ln_ref[dst, :] = m + jnp.log(den)
        return carry

    lax.fori_loop(0, steps, body, 0, unroll=DIL_ATTN_UNROLL)

    lses = [grp[5][...] for grp in groups]
    top = jnp.maximum(jnp.maximum(lses[0], lses[1]), lses[2])
    ws = [jnp.exp(l - top) for l in lses]
    num = ws[0] * groups[0][4][...] + ws[1] * groups[1][4][...] + ws[2] * groups[2][4][...]
    o_ref[...] = (num / (ws[0] + ws[1] + ws[2])).astype(BF16)


def _dil_attn(qkvs, *, batch, seq):
    n_i = seq // DIL_BLOCK
    in_specs, scratch = [], []
    args = []
    for g, r in enumerate(DILATIONS):
        nq = DIL_BLOCK // r

        def cur(s, nq=nq, r=r):
            return pl.BlockSpec((None, None, r, nq, HEAD_DIM), lambda b, h, i: (s, b, 0, i, h))

        def prev(s, nq=nq, r=r):
            per = nq // BAND
            return pl.BlockSpec((None, None, r, BAND, HEAD_DIM),
                                lambda b, h, i: (s, b, 0, jnp.maximum(i * per - 1, 0), h))

        in_specs += [cur(0), cur(1), prev(1), cur(2), prev(2)]
        args += [qkvs[g]] * 5
        scratch += [pltpu.VMEM((r, BAND + nq, HEAD_DIM), BF16), pltpu.VMEM((r, BAND + nq, 2 * HEAD_DIM), BF16),
                    pltpu.VMEM((DIL_BLOCK, HEAD_DIM), F32), pltpu.VMEM((DIL_BLOCK, HEAD_DIM), F32)]
    return pl.pallas_call(
        _dil_attn_kernel,
        grid=(batch, DIL_HEADS, n_i),
        in_specs=in_specs,
        out_specs=pl.BlockSpec((None, DIL_BLOCK, HEAD_DIM), lambda b, h, i: (b, i, h)),
        out_shape=jax.ShapeDtypeStruct((batch, seq, DIL_GROUP_WIDTH), BF16),
        scratch_shapes=scratch,
        compiler_params=_params("parallel", "parallel", "arbitrary"),
        name="dil_attn",
    )(*args)


def _diff_attn_kernel(lq1_ref, lk1_ref, lq2_ref, lk2_ref, q_ref, k_ref, v_ref, sub_ref, o_ref,
                      vt_ref, acc_ref, s0_ref, s1_ref, *, lam_init):
    qi = pl.program_id(2)
    tq = q_ref.shape[0]
    tk = vt_ref.shape[2]
    per_q = tq // tk

    @pl.when(qi == 0)
    def _():
        for ch in range(vt_ref.shape[0]):
            vt_ref[ch] = v_ref[ch * tk:(ch + 1) * tk, :].astype(F32).T.astype(BF16)

    q = q_ref[...]
    acc_ref[...] = jnp.zeros(acc_ref.shape, F32)

    def scores(kb, s_ref):
        k = k_ref[pl.ds(pl.multiple_of(kb * tk, tk), tk), :]
        for c in range(2):
            lanes = slice(c * HEAD_DIM, (c + 1) * HEAD_DIM)
            s_ref[c] = _dot_nt(k[:, lanes], q[:, lanes])

    def update(kb, s_ref, stats, first_query=None):
        vt = vt_ref[kb]
        out = []
        for c in range(2):
            s = s_ref[c]
            if first_query is not None:
                key = lax.broadcasted_iota(jnp.int32, (tk, tq), 0)
                qry = lax.broadcasted_iota(jnp.int32, (tk, tq), 1)
                s = jnp.where(key + first_query <= qry, s, NEG_BIG)
            m_prev, l_prev = stats[2 * c], stats[2 * c + 1]
            m_new = jnp.maximum(m_prev, jnp.max(s, axis=0, keepdims=True))
            alpha = jnp.exp(m_prev - m_new)
            p = jnp.exp(s - m_new)
            out += [m_new, alpha * l_prev + jnp.sum(p, axis=0, keepdims=True)]
            acc_ref[c] = alpha * acc_ref[c] + _dot(vt, p.astype(BF16))
        return tuple(out)

    def pair(j, stats):
        kb = per_q * j
        scores(kb + 1, s1_ref)
        stats = update(kb, s0_ref, stats)
        scores(kb + 2, s0_ref)
        return update(kb + 1, s1_ref, stats)

    row0 = jnp.full((1, tq), NEG_BIG, F32)
    zero = jnp.zeros((1, tq), F32)
    scores(0, s0_ref)
    stats = lax.fori_loop(0, qi, pair, (row0, zero, row0, zero))
    kd = per_q * qi
    scores(kd + 1, s1_ref)
    stats = update(kd, s0_ref, stats, first_query=0)
    _, l1, _, l2 = update(kd + 1, s1_ref, stats, first_query=tk)

    lam = (jnp.exp(jnp.sum(lq1_ref[...] * lk1_ref[...], axis=-1, keepdims=True))
           - jnp.exp(jnp.sum(lq2_ref[...] * lk2_ref[...], axis=-1, keepdims=True)) + lam_init)
    o = (acc_ref[0] / l1 - lam * (acc_ref[1] / l2)).T
    o_ref[...] = (o * _rms_scale(o) * sub_ref[...] * (1.0 - lam_init)).astype(BF16)


def _diff_attn(qk, vg, lq1, lk1, lq2, lk2, subln, *, batch, seq, lam_init):
    tq, tk = DIFF_TQ, DIFF_TK
    assert tq == 2 * tk
    q_blk, k_blk, v_blk = 0, DIFF_HEADS, 0
    vec = pl.BlockSpec((1, HEAD_DIM), lambda b, h, i: (0, 0))
    return pl.pallas_call(
        functools.partial(_diff_attn_kernel, lam_init=lam_init),
        grid=(batch, DIFF_HEADS, seq // tq),
        in_specs=[
            vec, vec, vec, vec,
            pl.BlockSpec((None, tq, DIFF_HEAD_WIDTH), lambda b, h, i: (b, i, q_blk + h)),
            pl.BlockSpec((None, seq, DIFF_HEAD_WIDTH), lambda b, h, i: (b, 0, k_blk + h)),
            pl.BlockSpec((None, seq, DIFF_HEAD_WIDTH), lambda b, h, i: (b, 0, v_blk + h)),
            pl.BlockSpec((1, DIFF_HEAD_WIDTH), lambda b, h, i: (0, 0)),
        ],
        out_specs=pl.BlockSpec((None, tq, DIFF_HEAD_WIDTH), lambda b, h, i: (b, i, h)),
        out_shape=jax.ShapeDtypeStruct((batch, seq, DIFF_WIDTH), BF16),
        scratch_shapes=[pltpu.VMEM((seq // tk, DIFF_HEAD_WIDTH, tk), BF16),
                        pltpu.VMEM((2, DIFF_HEAD_WIDTH, tq), F32),
                        pltpu.VMEM((2, tk, tq), F32),
                        pltpu.VMEM((2, tk, tq), F32)],
        compiler_params=_params("parallel", "parallel", "arbitrary"),
        name="diff_attn",
    )(lq1, lk1, lq2, lk2, qk, qk, vg, subln)


def _merge_kernel(x_ref, od_ref, of_ref, gd0_ref, gd1_ref, gf0_ref, gf1_ref, wa_ref, wb_ref, wo_ref, o_ref):
    gd = jnp.concatenate([gd0_ref[...], gd1_ref[...]], axis=1).astype(F32)
    gf = jnp.concatenate([gf0_ref[...], gf1_ref[...]], axis=1).astype(F32)
    y = gd * _dot(od_ref[...], wa_ref[...]) + gf * _dot(of_ref[...], wb_ref[...])
    o_ref[...] = x_ref[...] + _dot(y.astype(BF16), wo_ref[...])


def _merge(x, o_dil, o_diff, vg, wa, wb, wo):
    T = x.shape[0]
    tm = MERGE_TM
    half = D_MODEL // 2

    def tok(width, blk=0):
        return pl.BlockSpec((tm, width), lambda i: (i, blk))

    def resident(shape):
        return pl.BlockSpec(shape, lambda i: (0, 0), pipeline_mode=pl.Buffered(1))

    gate0 = DIFF_WIDTH // half
    return pl.pallas_call(
        _merge_kernel,
        grid=(T // tm,),
        in_specs=[tok(D_MODEL), tok(DIL_GROUP_WIDTH), tok(DIFF_WIDTH),
                  tok(half, gate0), tok(half, gate0 + 1), tok(half, gate0 + 2), tok(half, gate0 + 3),
                  resident(wa.shape), resident(wb.shape), resident(wo.shape)],
        out_specs=tok(D_MODEL),
        out_shape=jax.ShapeDtypeStruct((T, D_MODEL), F32),
        compiler_params=_params("parallel"),
        name="merge",
    )(x, o_dil, o_diff, vg, vg, vg, vg, wa, wb, wo)


def _rope_tables(positions):
    lane = jnp.arange(HEAD_DIM)
    freq = ROPE_THETA ** (-(2 * (lane % ROPE_HALF)).astype(F32) / ROPE_DIM)
    inv = jnp.where(lane < ROPE_DIM, freq, 0.0)
    sign = jnp.where(lane < ROPE_HALF, -1.0, jnp.where(lane < ROPE_DIM, 1.0, 0.0)).astype(F32)
    ang = positions.astype(F32).reshape(-1, 1) * inv
    return jnp.cos(ang), jnp.sin(ang) * sign


def _layer(x, tables, layer, ffn1_norm, ffn1_w_gate, ffn1_w_up, ffn1_w_down, mix_norm, w_in,
           dil_q_norm, dil_k_norm, diff_q_norm, diff_k_norm, diff_lq1, diff_lk1, diff_lq2, diff_lk2,
           diff_subln, w_dil_branch, w_diff_branch, w_out, ffn2_norm, ffn2_w_gate, ffn2_w_up, ffn2_w_down,
           *, batch, seq):
    cos, sin = tables
    row = lambda v: v.reshape(1, -1).astype(F32)
    bf = lambda w: w.astype(BF16)
    qk_scale = HEAD_DIM ** -0.5
    lam_init = 0.8 - 0.6 * math.exp(-0.3 * layer)

    x1, h = _ffn(x, row(ffn1_norm), bf(ffn1_w_gate), bf(ffn1_w_up), bf(ffn1_w_down), row(mix_norm))

    diff0 = 3 * DIL_WIDTH
    fv0 = diff0 + 2 * DIFF_WIDTH
    gains = lambda gq, gk: jnp.stack([gq * qk_scale, gk]).astype(F32)

    dil_kinds = (QUERY,) * DIL_HEADS + (KEY,) * DIL_HEADS + (PLAIN,) * DIL_HEADS
    dil_gains = gains(dil_q_norm, dil_k_norm)
    qkvs = []
    for g, r in enumerate(DILATIONS):
        sel = lambda sec: w_in[:, sec * DIL_WIDTH + g * DIL_GROUP_WIDTH:sec * DIL_WIDTH + (g + 1) * DIL_GROUP_WIDTH]
        w_g = bf(jnp.concatenate([sel(0), sel(1), sel(2)], axis=1))
        qkvs.append(_proj(h, w_g, dil_gains, cos, sin, kinds=dil_kinds, tm=PROJ_TM, name=f"dil_proj{g}",
                          dil=(r, batch, seq)))

    n_diff = DIFF_WIDTH // HEAD_DIM
    qk = _proj(h, bf(w_in[:, diff0:fv0]), gains(diff_q_norm, diff_k_norm), cos, sin,
               kinds=(QUERY,) * n_diff + (KEY,) * n_diff, tm=PROJ_TM, name="diff_qk_proj")
    n_gate = 2 * D_MODEL // HEAD_DIM
    vg = _proj(h, bf(w_in[:, fv0:]), dil_gains, cos, sin,
               kinds=(PLAIN,) * n_diff + (GATE,) * n_gate, tm=PROJ_GATE_TM, name="v_gate_proj")

    o_dil = _dil_attn(qkvs, batch=batch, seq=seq)
    o_diff = _diff_attn(qk.reshape(batch, seq, -1), vg.reshape(batch, seq, -1), row(diff_lq1), row(diff_lk1),
                        row(diff_lq2), row(diff_lk2), row(diff_subln), batch=batch, seq=seq, lam_init=lam_init)

    x2 = _merge(x1, o_dil.reshape(batch * seq, -1), o_diff.reshape(batch * seq, -1), vg,
                bf(w_dil_branch), bf(w_diff_branch), bf(w_out))
    (out,) = _ffn(x2, row(ffn2_norm), bf(ffn2_w_gate), bf(ffn2_w_up), bf(ffn2_w_down), None)
    return out


def kernel(x, positions, ffn1_norm, ffn1_w_gate, ffn1_w_up, ffn1_w_down, mix_norm, w_in, dil_q_norm, dil_k_norm, diff_q_norm, diff_k_norm, diff_lq1, diff_lk1, diff_lq2, diff_lk2, diff_subln, w_dil_branch, w_diff_branch, w_out, ffn2_norm, ffn2_w_gate, ffn2_w_up, ffn2_w_down):
    batch, seq, d_model = x.shape
    assert d_model == D_MODEL and seq % DIL_BLOCK == 0
    weights = (ffn1_norm, ffn1_w_gate, ffn1_w_up, ffn1_w_down, mix_norm, w_in, dil_q_norm, dil_k_norm,
               diff_q_norm, diff_k_norm, diff_lq1, diff_lk1, diff_lq2, diff_lk2, diff_subln, w_dil_branch,
               w_diff_branch, w_out, ffn2_norm, ffn2_w_gate, ffn2_w_up, ffn2_w_down)
    tables = _rope_tables(positions)
    y = x.reshape(batch * seq, d_model)
    for layer in range(ffn1_norm.shape[0]):
        y = _layer(y, tables, layer, *(w[layer] for w in weights), batch=batch, seq=seq)
    return y.reshape(batch, seq, d_model)
```

```python
import functools
import math

import jax
import jax.numpy as jnp
from jax import lax
from jax.experimental import pallas as pl
from jax.experimental.pallas import tpu as pltpu

F32 = jnp.float32
BF16 = jnp.bfloat16

D_MODEL = 2048
D_FF = 5632
HEAD_DIM = 128
ROPE_DIM = HEAD_DIM // 4
ROPE_HALF = ROPE_DIM // 2
ROPE_THETA = 500000.0
RMS_EPS = 1e-6
BAND = 128
DILATIONS = (1, 4, 16)
N_DIL_GROUPS = len(DILATIONS)
DIL_HEADS = 4
DIL_GROUP_WIDTH = DIL_HEADS * HEAD_DIM
DIL_WIDTH = N_DIL_GROUPS * DIL_GROUP_WIDTH
DIFF_HEADS = 4
DIFF_HEAD_WIDTH = 2 * HEAD_DIM
DIFF_WIDTH = DIFF_HEADS * DIFF_HEAD_WIDTH
DIL_BLOCK = BAND * DILATIONS[-1]
NEG_BIG = -1e30

VMEM_LIMIT_BYTES = 58 * 1024 * 1024

FFN_TM = 1024
FFN_TF = 512
PROJ_TM = 1024
PROJ_GATE_TM = 512
PROJ_CHUNK = 256
MAX_SINGLE_OP_STRIDE = 4
MERGE_TM = 512
DIFF_TQ = 1024
DIFF_TK = 512
DIL_ATTN_UNROLL = 8


def _params(*sem):
    return pltpu.CompilerParams(dimension_semantics=sem, vmem_limit_bytes=VMEM_LIMIT_BYTES)


def _rms_scale(x):
    return lax.rsqrt(jnp.mean(x * x, axis=-1, keepdims=True) + RMS_EPS)


def _dot(a, b):
    return jnp.dot(a, b, preferred_element_type=F32)


def _dot_nt(a, b):
    return lax.dot_general(a, b, (((1,), (1,)), ((), ())), preferred_element_type=F32)


def _ffn_kernel(*refs, emit_norm):
    if emit_norm:
        x_ref, gin_ref, wg_ref, wu_ref, wd_ref, gout_ref, o_ref, hn_ref, h_ref = refs
    else:
        x_ref, gin_ref, wg_ref, wu_ref, wd_ref, o_ref, h_ref = refs
    f = pl.program_id(1)
    last = pl.num_programs(1) - 1

    def step(h):
        g = _dot(h, wg_ref[...])
        u = _dot(h, wu_ref[...])
        a = (g * jax.nn.sigmoid(g)) * u * 0.5
        o_ref[...] += _dot(a.astype(BF16), wd_ref[...])

    def finish():
        if emit_norm:
            y = o_ref[...]
            hn_ref[...] = (y * _rms_scale(y) * gout_ref[...]).astype(BF16)

    @pl.when(f == 0)
    def _():
        x = x_ref[...]
        h = (x * _rms_scale(x) * gin_ref[...]).astype(BF16)
        h_ref[...] = h
        o_ref[...] = x
        step(h)

    @pl.when((f > 0) & (f < last))
    def _():
        step(h_ref[...])

    @pl.when(f == last)
    def _():
        step(h_ref[...])
        finish()


def _ffn(x, g_in, wg, wu, wd, g_out):
    T = x.shape[0]
    tm, tf = FFN_TM, FFN_TF
    emit_norm = g_out is not None
    tok = pl.BlockSpec((tm, D_MODEL), lambda i, f: (i, 0))
    vec = pl.BlockSpec((1, D_MODEL), lambda i, f: (0, 0))
    in_specs = [tok, vec,
                pl.BlockSpec((D_MODEL, tf), lambda i, f: (0, f)),
                pl.BlockSpec((D_MODEL, tf), lambda i, f: (0, f)),
                pl.BlockSpec((tf, D_MODEL), lambda i, f: (f, 0))]
    args = [x, g_in, wg, wu, wd]
    out_specs = [tok]
    out_shape = [jax.ShapeDtypeStruct((T, D_MODEL), F32)]
    if emit_norm:
        in_specs.append(vec)
        args.append(g_out)
        out_specs.append(pl.BlockSpec((tm, D_MODEL), lambda i, f: (i, 0), pipeline_mode=pl.Buffered(1)))
        out_shape.append(jax.ShapeDtypeStruct((T, D_MODEL), BF16))
    return pl.pallas_call(
        functools.partial(_ffn_kernel, emit_norm=emit_norm),
        grid=(T // tm, D_FF // tf),
        in_specs=in_specs,
        out_specs=out_specs,
        out_shape=out_shape,
        scratch_shapes=[pltpu.VMEM((tm, D_MODEL), BF16)],
        compiler_params=_params("parallel", "arbitrary"),
        name="ffn",
    )(*args)


QUERY, KEY, PLAIN, GATE = "query", "key", "plain", "gate"


def _sigmoid(y):
    return 0.5 * jnp.tanh(0.5 * y) + 0.5


def _norm_rope(y, gain, cos, sin):
    y = y * _rms_scale(y) * gain
    lane = lax.broadcasted_iota(jnp.int32, y.shape, 1)
    partner = jnp.where(lane < ROPE_HALF, pltpu.roll(y, HEAD_DIM - ROPE_HALF, 1), pltpu.roll(y, ROPE_HALF, 1))
    return y * cos + partner * sin


def _proj_kernel(h_ref, w_ref, gain_ref, cos_ref, sin_ref, o_ref, y_ref, *slab, kinds, chunk, dilation):
    tm = h_ref.shape[0]
    n_chunks = tm // chunk

    def compute(m, slot):
        y = _dot(h_ref[pl.ds(m * chunk, chunk), :], w_ref[...])
        for hh in range(len(kinds)):
            y_ref[slot, hh] = y[:, hh * HEAD_DIM:(hh + 1) * HEAD_DIM]

    def finish(m, slot):
        rows = pl.ds(m * chunk, chunk)
        for hh, kind in enumerate(kinds):
            y = y_ref[slot, hh]
            if kind in (QUERY, KEY):
                gain = gain_ref[0:1, :] if kind == QUERY else gain_ref[1:2, :]
                y = _norm_rope(y, gain, cos_ref[rows, :], sin_ref[rows, :])
            elif kind == GATE:
                y = _sigmoid(y)
            if dilation is None:
                o_ref[rows, hh * HEAD_DIM:(hh + 1) * HEAD_DIM] = y.astype(BF16)
            else:
                sec, lanes = divmod(hh * HEAD_DIM, DIL_GROUP_WIDTH)
                lanes = slice(lanes, lanes + HEAD_DIM)
                n = chunk // dilation
                dst = pl.ds(m * n, n)
                if dilation == 1:
                    o_ref[sec, 0, dst, lanes] = y.astype(BF16)
                elif dilation <= MAX_SINGLE_OP_STRIDE:
                    slab_ref = slab[0]
                    slab_ref[hh] = y
                    for c in range(dilation):
                        o_ref[sec, c, dst, lanes] = slab_ref[hh, pl.ds(c, n, stride=dilation), :].astype(BF16)
                else:
                    slab_ref, slab2_ref = slab
                    step = MAX_SINGLE_OP_STRIDE
                    assert dilation == step * step
                    n1 = chunk // step
                    slab_ref[hh] = y
                    for c_lo in range(step):
                        slab2_ref[hh, c_lo * n1:(c_lo + 1) * n1, :] = slab_ref[hh, pl.ds(c_lo, n1, stride=step), :]
                    for c_lo in range(step):
                        for c_hi in range(step):
                            o_ref[sec, c_lo + step * c_hi, dst, lanes] = slab2_ref[
                                hh, pl.ds(c_lo * n1 + c_hi, n, stride=step), :].astype(BF16)

    compute(0, 0)
    for m in range(n_chunks - 1):
        finish(m, m % 2)
        compute(m + 1, (m + 1) % 2)
    finish(n_chunks - 1, (n_chunks - 1) % 2)


def _proj(h, w, gains, cos, sin, *, kinds, tm, name, dil=None):
    T = h.shape[0]
    width = len(kinds) * HEAD_DIM
    assert w.shape == (D_MODEL, width)
    chunk = PROJ_CHUNK
    scratch = [pltpu.VMEM((2, len(kinds), chunk, HEAD_DIM), F32)]
    if dil is None:
        dilation = None
        out_spec = pl.BlockSpec((tm, width), lambda i: (i, 0))
        out_shape = jax.ShapeDtypeStruct((T, width), BF16)
    else:
        dilation, batch, seq = dil
        n_i = seq // tm
        out_spec = pl.BlockSpec((3, None, dilation, tm // dilation, DIL_GROUP_WIDTH),
                                lambda i: (0, i // n_i, 0, i % n_i, 0))
        out_shape = jax.ShapeDtypeStruct((3, batch, dilation, seq // dilation, DIL_GROUP_WIDTH), BF16)
        if dilation > 1:
            scratch.append(pltpu.VMEM((len(kinds), chunk, HEAD_DIM), F32))
        if dilation > MAX_SINGLE_OP_STRIDE:
            scratch.append(pltpu.VMEM((len(kinds), chunk, HEAD_DIM), F32))
    tok = lambda cols: pl.BlockSpec((tm, cols), lambda i: (i, 0))
    return pl.pallas_call(
        functools.partial(_proj_kernel, kinds=kinds, chunk=chunk, dilation=dilation),
        grid=(T // tm,),
        in_specs=[tok(D_MODEL),
                  pl.BlockSpec((D_MODEL, width), lambda i: (0, 0), pipeline_mode=pl.Buffered(1)),
                  pl.BlockSpec((2, HEAD_DIM), lambda i: (0, 0)),
                  tok(HEAD_DIM), tok(HEAD_DIM)],
        out_specs=out_spec,
        out_shape=out_shape,
        scratch_shapes=scratch,
        compiler_params=_params("parallel"),
        name=name,
    )(h, w, gains, cos, sin)


def _dil_attn_kernel(*refs):
    ins, o_ref, scr = refs[:15], refs[15], refs[16:]
    i = pl.program_id(2)
    steps = DIL_BLOCK // BAND

    row = lax.broadcasted_iota(jnp.int32, (BAND, 2 * BAND), 0)
    col = lax.broadcasted_iota(jnp.int32, (BAND, 2 * BAND), 1)
    band_bias = jnp.where((col >= row) & (col <= row + BAND), 0.0, NEG_BIG)
    start_bias = jnp.where(i == 0, jnp.where(col >= BAND, band_bias, NEG_BIG), band_bias)

    groups = []
    for g, r in enumerate(DILATIONS):
        q_ref, kc_ref, kp_ref, vc_ref, vp_ref = ins[5 * g:5 * g + 5]
        kx_ref, vx_ref, on_ref, ln_ref = scr[4 * g:4 * g + 4]
        kx_ref[:, :BAND, :] = kp_ref[...]
        kx_ref[:, BAND:, :] = kc_ref[...]
        vx_ref[:, :BAND, :HEAD_DIM] = vp_ref[...]
        vx_ref[:, BAND:, :HEAD_DIM] = vc_ref[...]
        vx_ref[:, :, HEAD_DIM:] = jnp.ones((r, vx_ref.shape[1], HEAD_DIM), BF16)
        groups.append((r, q_ref, kx_ref, vx_ref, on_ref, ln_ref))

    def body(t, carry):
        for r, q_ref, kx_ref, vx_ref, on_ref, ln_ref in groups:
            tiles = DIL_BLOCK // r // BAND
            if tiles == steps:
                c, mb = 0, t
            elif tiles == 1:
                c, mb = t, 0
            else:
                c, mb = t % r, t // r
            m0 = mb * BAND if isinstance(mb, int) else pl.multiple_of(mb * BAND, BAND)
            q = q_ref[c, pl.ds(m0, BAND), :]
            k = kx_ref[c, pl.ds(m0, 2 * BAND), :]
            v = vx_ref[c, pl.ds(m0, 2 * BAND), :]
            if isinstance(mb, int):
                bias = start_bias if mb == 0 else band_bias
            else:
                bias = jnp.where(mb == 0, start_bias, band_bias)
            s = _dot_nt(q, k) + bias
            m = jnp.max(s, axis=-1, keepdims=True)
            p = jnp.exp(s - m)
            od = _dot(p.astype(BF16), v)
            den = od[:, HEAD_DIM:]
            dst = pl.ds(mb * (BAND * r) + c, BAND, stride=r) if r > 1 else pl.ds(m0, BAND)
            on_ref[dst, :] = od[:, :HEAD_DIM] / den
            ln_ref[dst, :] = m + jnp.log(den)
        return carry

    lax.fori_loop(0, steps, body, 0, unroll=DIL_ATTN_UNROLL)

    lses = [grp[5][...] for grp in groups]
    top = jnp.maximum(jnp.maximum(lses[0], lses[1]), lses[2])
    ws = [jnp.exp(l - top) for l in lses]
    num = ws[0] * groups[0][4][...] + ws[1] * groups[1][4][...] + ws[2] * groups[2][4][...]
    o_ref[...] = (num / (ws[0] + ws[1] + ws[2])).astype(BF16)


def _dil_attn(qkvs, *, batch, seq):
    n_i = seq // DIL_BLOCK
    in_specs, scratch = [], []
    args = []
    for g, r in enumerate(DILATIONS):
        nq = DIL_BLOCK // r

        def cur(s, nq=nq, r=r):
            return pl.BlockSpec((None, None, r, nq, HEAD_DIM), lambda b, h, i: (s, b, 0, i, h))

        def prev(s, nq=nq, r=r):
            per = nq // BAND
            return pl.BlockSpec((None, None, r, BAND, HEAD_DIM),
                                lambda b, h, i: (s, b, 0, jnp.maximum(i * per - 1, 0), h))

        in_specs += [cur(0), cur(1), prev(1), cur(2), prev(2)]
        args += [qkvs[g]] * 5
        scratch += [pltpu.VMEM((r, BAND + nq, HEAD_DIM), BF16), pltpu.VMEM((r, BAND + nq, 2 * HEAD_DIM), BF16),
                    pltpu.VMEM((DIL_BLOCK, HEAD_DIM), F32), pltpu.VMEM((DIL_BLOCK, HEAD_DIM), F32)]
    return pl.pallas_call(
        _dil_attn_kernel,
        grid=(batch, DIL_HEADS, n_i),
        in_specs=in_specs,
        out_specs=pl.BlockSpec((None, DIL_BLOCK, HEAD_DIM), lambda b, h, i: (b, i, h)),
        out_shape=jax.ShapeDtypeStruct((batch, seq, DIL_GROUP_WIDTH), BF16),
        scratch_shapes=scratch,
        compiler_params=_params("parallel", "parallel", "arbitrary"),
        name="dil_attn",
    )(*args)


def _diff_attn_kernel(lq1_ref, lk1_ref, lq2_ref, lk2_ref, q_ref, k_ref, v_ref, sub_ref, o_ref,
                      vt_ref, acc_ref, s0_ref, s1_ref, *, lam_init):
    qi = pl.program_id(2)
    tq = q_ref.shape[0]
    tk = vt_ref.shape[2]
    per_q = tq // tk

    @pl.when(qi == 0)
    def _():
        for ch in range(vt_ref.shape[0]):
            vt_ref[ch] = v_ref[ch * tk:(ch + 1) * tk, :].astype(F32).T.astype(BF16)

    q = q_ref[...]
    acc_ref[...] = jnp.zeros(acc_ref.shape, F32)

    def scores(kb, s_ref):
        k = k_ref[pl.ds(pl.multiple_of(kb * tk, tk), tk), :]
        for c in range(2):
            lanes = slice(c * HEAD_DIM, (c + 1) * HEAD_DIM)
            s_ref[c] = _dot_nt(k[:, lanes], q[:, lanes])

    def update(kb, s_ref, stats, first_query=None):
        vt = vt_ref[kb]
        out = []
        for c in range(2):
            s = s_ref[c]
            if first_query is not None:
                key = lax.broadcasted_iota(jnp.int32, (tk, tq), 0)
                qry = lax.broadcasted_iota(jnp.int32, (tk, tq), 1)
                s = jnp.where(key + first_query <= qry, s, NEG_BIG)
            m_prev, l_prev = stats[2 * c], stats[2 * c + 1]
            m_new = jnp.maximum(m_prev, jnp.max(s, axis=0, keepdims=True))
            alpha = jnp.exp(m_prev - m_new)
            p = jnp.exp(s - m_new)
            out += [m_new, alpha * l_prev + jnp.sum(p, axis=0, keepdims=True)]
            acc_ref[c] = alpha * acc_ref[c] + _dot(vt, p.astype(BF16))
        return tuple(out)

    def pair(j, stats):
        kb = per_q * j
        scores(kb + 1, s1_ref)
        stats = update(kb, s0_ref, stats)
        scores(kb + 2, s0_ref)
        return update(kb + 1, s1_ref, stats)

    row0 = jnp.full((1, tq), NEG_BIG, F32)
    zero = jnp.zeros((1, tq), F32)
    scores(0, s0_ref)
    stats = lax.fori_loop(0, qi, pair, (row0, zero, row0, zero))
    kd = per_q * qi
    scores(kd + 1, s1_ref)
    stats = update(kd, s0_ref, stats, first_query=0)
    _, l1, _, l2 = update(kd + 1, s1_ref, stats, first_query=tk)

    lam = (jnp.exp(jnp.sum(lq1_ref[...] * lk1_ref[...], axis=-1, keepdims=True))
           - jnp.exp(jnp.sum(lq2_ref[...] * lk2_ref[...], axis=-1, keepdims=True)) + lam_init)
    o = (acc_ref[0] / l1 - lam * (acc_ref[1] / l2)).T
    o_ref[...] = (o * _rms_scale(o) * sub_ref[...] * (1.0 - lam_init)).astype(BF16)


def _diff_attn(qk, vg, lq1, lk1, lq2, lk2, subln, *, batch, seq, lam_init):
    tq, tk = DIFF_TQ, DIFF_TK
    assert tq == 2 * tk
    q_blk, k_blk, v_blk = 0, DIFF_HEADS, 0
    vec = pl.BlockSpec((1, HEAD_DIM), lambda b, h, i: (0, 0))
    return pl.pallas_call(
        functools.partial(_diff_attn_kernel, lam_init=lam_init),
        grid=(batch, DIFF_HEADS, seq // tq),
        in_specs=[
            vec, vec, vec, vec,
            pl.BlockSpec((None, tq, DIFF_HEAD_WIDTH), lambda b, h, i: (b, i, q_blk + h)),
            pl.BlockSpec((None, seq, DIFF_HEAD_WIDTH), lambda b, h, i: (b, 0, k_blk + h)),
            pl.BlockSpec((None, seq, DIFF_HEAD_WIDTH), lambda b, h, i: (b, 0, v_blk + h)),
            pl.BlockSpec((1, DIFF_HEAD_WIDTH), lambda b, h, i: (0, 0)),
        ],
        out_specs=pl.BlockSpec((None, tq, DIFF_HEAD_WIDTH), lambda b, h, i: (b, i, h)),
        out_shape=jax.ShapeDtypeStruct((batch, seq, DIFF_WIDTH), BF16),
        scratch_shapes=[pltpu.VMEM((seq // tk, DIFF_HEAD_WIDTH, tk), BF16),
                        pltpu.VMEM((2, DIFF_HEAD_WIDTH, tq), F32),
                        pltpu.VMEM((2, tk, tq), F32),
                        pltpu.VMEM((2, tk, tq), F32)],
        compiler_params=_params("parallel", "parallel", "arbitrary"),
        name="diff_attn",
    )(lq1, lk1, lq2, lk2, qk, qk, vg, subln)


def _merge_kernel(x_ref, od_ref, of_ref, gd0_ref, gd1_ref, gf0_ref, gf1_ref, wa_ref, wb_ref, wo_ref, o_ref):
    gd = jnp.concatenate([gd0_ref[...], gd1_ref[...]], axis=1).astype(F32)
    gf = jnp.concatenate([gf0_ref[...], gf1_ref[...]], axis=1).astype(F32)
    y = gd * _dot(od_ref[...], wa_ref[...]) + gf * _dot(of_ref[...], wb_ref[...])
    o_ref[...] = x_ref[...] + _dot(y.astype(BF16), wo_ref[...])


def _merge(x, o_dil, o_diff, vg, wa, wb, wo):
    T = x.shape[0]
    tm = MERGE_TM
    half = D_MODEL // 2

    def tok(width, blk=0):
        return pl.BlockSpec((tm, width), lambda i: (i, blk))

    def resident(shape):
        return pl.BlockSpec(shape, lambda i: (0, 0), pipeline_mode=pl.Buffered(1))

    gate0 = DIFF_WIDTH // half
    return pl.pallas_call(
        _merge_kernel,
        grid=(T // tm,),
        in_specs=[tok(D_MODEL), tok(DIL_GROUP_WIDTH), tok(DIFF_WIDTH),
                  tok(half, gate0), tok(half, gate0 + 1), tok(half, gate0 + 2), tok(half, gate0 + 3),
                  resident(wa.shape), resident(wb.shape), resident(wo.shape)],
        out_specs=tok(D_MODEL),
        out_shape=jax.ShapeDtypeStruct((T, D_MODEL), F32),
        compiler_params=_params("parallel"),
        name="merge",
    )(x, o_dil, o_diff, vg, vg, vg, vg, wa, wb, wo)


def _rope_tables(positions):
    lane = jnp.arange(HEAD_DIM)
    freq = ROPE_THETA ** (-(2 * (lane % ROPE_HALF)).astype(F32) / ROPE_DIM)
    inv = jnp.where(lane < ROPE_DIM, freq, 0.0)
    sign = jnp.where(lane < ROPE_HALF, -1.0, jnp.where(lane < ROPE_DIM, 1.0, 0.0)).astype(F32)
    ang = positions.astype(F32).reshape(-1, 1) * inv
    return jnp.cos(ang), jnp.sin(ang) * sign


def _layer(x, tables, layer, ffn1_norm, ffn1_w_gate, ffn1_w_up, ffn1_w_down, mix_norm, w_in,
           dil_q_norm, dil_k_norm, diff_q_norm, diff_k_norm, diff_lq1, diff_lk1, diff_lq2, diff_lk2,
           diff_subln, w_dil_branch, w_diff_branch, w_out, ffn2_norm, ffn2_w_gate, ffn2_w_up, ffn2_w_down,
           *, batch, seq):
    cos, sin = tables
    row = lambda v: v.reshape(1, -1).astype(F32)
    bf = lambda w: w.astype(BF16)
    qk_scale = HEAD_DIM ** -0.5
    lam_init = 0.8 - 0.6 * math.exp(-0.3 * layer)

    x1, h = _ffn(x, row(ffn1_norm), bf(ffn1_w_gate), bf(ffn1_w_up), bf(ffn1_w_down), row(mix_norm))

    diff0 = 3 * DIL_WIDTH
    fv0 = diff0 + 2 * DIFF_WIDTH
    gains = lambda gq, gk: jnp.stack([gq * qk_scale, gk]).astype(F32)

    dil_kinds = (QUERY,) * DIL_HEADS + (KEY,) * DIL_HEADS + (PLAIN,) * DIL_HEADS
    dil_gains = gains(dil_q_norm, dil_k_norm)
    qkvs = []
    for g, r in enumerate(DILATIONS):
        sel = lambda sec: w_in[:, sec * DIL_WIDTH + g * DIL_GROUP_WIDTH:sec * DIL_WIDTH + (g + 1) * DIL_GROUP_WIDTH]
        w_g = bf(jnp.concatenate([sel(0), sel(1), sel(2)], axis=1))
        qkvs.append(_proj(h, w_g, dil_gains, cos, sin, kinds=dil_kinds, tm=PROJ_TM, name=f"dil_proj{g}",
                          dil=(r, batch, seq)))

    n_diff = DIFF_WIDTH // HEAD_DIM
    qk = _proj(h, bf(w_in[:, diff0:fv0]), gains(diff_q_norm, diff_k_norm), cos, sin,
               kinds=(QUERY,) * n_diff + (KEY,) * n_diff, tm=PROJ_TM, name="diff_qk_proj")
    n_gate = 2 * D_MODEL // HEAD_DIM
    vg = _proj(h, bf(w_in[:, fv0:]), dil_gains, cos, sin,
               kinds=(PLAIN,) * n_diff + (GATE,) * n_gate, tm=PROJ_GATE_TM, name="v_gate_proj")

    o_dil = _dil_attn(qkvs, batch=batch, seq=seq)
    o_diff = _diff_attn(qk.reshape(batch, seq, -1), vg.reshape(batch, seq, -1), row(diff_lq1), row(diff_lk1),
                        row(diff_lq2), row(diff_lk2), row(diff_subln), batch=batch, seq=seq, lam_init=lam_init)

    x2 = _merge(x1, o_dil.reshape(batch * seq, -1), o_diff.reshape(batch * seq, -1), vg,
                bf(w_dil_branch), bf(w_diff_branch), bf(w_out))
    (out,) = _ffn(x2, row(ffn2_norm), bf(ffn2_w_gate), bf(ffn2_w_up), bf(ffn2_w_down), None)
    return out


def kernel(x, positions, ffn1_norm, ffn1_w_gate, ffn1_w_up, ffn1_w_down, mix_norm, w_in, dil_q_norm, dil_k_norm, diff_q_norm, diff_k_norm, diff_lq1, diff_lk1, diff_lq2, diff_lk2, diff_subln, w_dil_branch, w_diff_branch, w_out, ffn2_norm, ffn2_w_gate, ffn2_w_up, ffn2_w_down):
    batch, seq, d_model = x.shape
    assert d_model == D_MODEL and seq % DIL_BLOCK == 0
    weights = (ffn1_norm, ffn1_w_gate, ffn1_w_up, ffn1_w_down, mix_norm, w_in, dil_q_norm, dil_k_norm,
               diff_q_norm, diff_k_norm, diff_lq1, diff_lk1, diff_lq2, diff_lk2, diff_subln, w_dil_branch,
               w_diff_branch, w_out, ffn2_norm, ffn2_w_gate, ffn2_w_up, ffn2_w_down)
    tables = _rope_tables(positions)
    y = x.reshape(batch * seq, d_model)
    for layer in range(ffn1_norm.shape[0]):
        y = _layer(y, tables, layer, *(w[layer] for w in weights), batch=batch, seq=seq)
    return y.reshape(batch, seq, d_model)
```

```python
import functools
import math

import jax
import jax.numpy as jnp
from jax import lax
from jax.experimental import pallas as pl
from jax.experimental.pallas import tpu as pltpu

F32 = jnp.float32
BF16 = jnp.bfloat16

D_MODEL = 2048
D_FF = 5632
HEAD_DIM = 128
ROPE_DIM = HEAD_DIM // 4
ROPE_HALF = ROPE_DIM // 2
ROPE_THETA = 500000.0
RMS_EPS = 1e-6
BAND = 128
DILATIONS = (1, 4, 16)
N_DIL_GROUPS = len(DILATIONS)
DIL_HEADS = 4
DIL_GROUP_WIDTH = DIL_HEADS * HEAD_DIM
DIL_WIDTH = N_DIL_GROUPS * DIL_GROUP_WIDTH
DIFF_HEADS = 4
DIFF_HEAD_WIDTH = 2 * HEAD_DIM
DIFF_WIDTH = DIFF_HEADS * DIFF_HEAD_WIDTH
DIL_BLOCK = BAND * DILATIONS[-1]
LOG2_E = math.log2(math.e)
NEG_BIG = -1e30

VMEM_LIMIT_BYTES = 62 * 1024 * 1024

FFN_TM = 1024
FFN_TF = 512
PROJ_TM = 1024
PROJ_GATE_TM = 512
PROJ_CHUNK = 256
PROJ_PART_HEADS = 4
PROJ_PART_WIDTH = PROJ_PART_HEADS * HEAD_DIM
MAX_SINGLE_OP_STRIDE = 4
MERGE_TM = 512
DIFF_TQ = 1024
DIFF_TK = 512
DIFF_ONES_ROWS = 16
DIL_ATTN_UNROLL = 8


def _params(*sem):
    return pltpu.CompilerParams(dimension_semantics=sem, vmem_limit_bytes=VMEM_LIMIT_BYTES)


def _rms_scale(x):
    return lax.rsqrt(jnp.mean(x * x, axis=-1, keepdims=True) + RMS_EPS)


def _dot(a, b):
    return jnp.dot(a, b, preferred_element_type=F32)


def _dot_nt(a, b):
    return lax.dot_general(a, b, (((1,), (1,)), ((), ())), preferred_element_type=F32)


def _ffn_kernel(*refs, emit_norm):
    if emit_norm:
        x_ref, gin_ref, wg_ref, wu_ref, wd_ref, gout_ref, o_ref, hn_ref, h_ref = refs
    else:
        x_ref, gin_ref, wg_ref, wu_ref, wd_ref, o_ref, h_ref = refs
    f = pl.program_id(1)
    last = pl.num_programs(1) - 1

    def step(h):
        g = _dot(h, wg_ref[...])
        u = _dot(h, wu_ref[...])
        a = (g * jax.nn.sigmoid(g)) * u * 0.5
        o_ref[...] += _dot(a.astype(BF16), wd_ref[...])

    def finish():
        if emit_norm:
            y = o_ref[...]
            hn_ref[...] = (y * _rms_scale(y) * gout_ref[...]).astype(BF16)

    @pl.when(f == 0)
    def _():
        x = x_ref[...]
        h = (x * _rms_scale(x) * gin_ref[...]).astype(BF16)
        h_ref[...] = h
        o_ref[...] = x
        step(h)

    @pl.when((f > 0) & (f < last))
    def _():
        step(h_ref[...])

    @pl.when(f == last)
    def _():
        step(h_ref[...])
        finish()


def _ffn(x, g_in, wg, wu, wd, g_out):
    T = x.shape[0]
    tm, tf = FFN_TM, FFN_TF
    emit_norm = g_out is not None
    tok = pl.BlockSpec((tm, D_MODEL), lambda i, f: (i, 0))
    vec = pl.BlockSpec((1, D_MODEL), lambda i, f: (0, 0))
    in_specs = [tok, vec,
                pl.BlockSpec((D_MODEL, tf), lambda i, f: (0, f)),
                pl.BlockSpec((D_MODEL, tf), lambda i, f: (0, f)),
                pl.BlockSpec((tf, D_MODEL), lambda i, f: (f, 0))]
    args = [x, g_in, wg, wu, wd]
    out_specs = [tok]
    out_shape = [jax.ShapeDtypeStruct((T, D_MODEL), F32)]
    if emit_norm:
        in_specs.append(vec)
        args.append(g_out)
        out_specs.append(tok)
        out_shape.append(jax.ShapeDtypeStruct((T, D_MODEL), BF16))
    return pl.pallas_call(
        functools.partial(_ffn_kernel, emit_norm=emit_norm),
        grid=(T // tm, D_FF // tf),
        in_specs=in_specs,
        out_specs=out_specs,
        out_shape=out_shape,
        scratch_shapes=[pltpu.VMEM((tm, D_MODEL), BF16)],
        compiler_params=_params("parallel", "arbitrary"),
        name="ffn",
    )(*args)


QUERY, KEY, PLAIN, GATE = "query", "key", "plain", "gate"


def _sigmoid(y):
    return 0.5 * jnp.tanh(0.5 * y) + 0.5


def _norm_rope(y, gain, cos, sin):
    y = y * _rms_scale(y) * gain
    lane = lax.broadcasted_iota(jnp.int32, y.shape, 1)
    partner = jnp.where(lane < ROPE_HALF, pltpu.roll(y, HEAD_DIM - ROPE_HALF, 1), pltpu.roll(y, ROPE_HALF, 1))
    return y * cos + partner * sin


def _proj_kernel(h_ref, *refs, kinds, chunk, dilation):
    n_parts = len(kinds) // PROJ_PART_HEADS
    w_refs = refs[:n_parts]
    gain_ref, cos_ref, sin_ref, o_ref, y_ref = refs[n_parts:n_parts + 5]
    slab = refs[n_parts + 5:]
    tm = h_ref.shape[0]
    n_chunks = tm // chunk

    def compute(m, slot):
        h = h_ref[pl.ds(m * chunk, chunk), :]
        for p, w_ref in enumerate(w_refs):
            y = _dot(h, w_ref[...])
            for j in range(PROJ_PART_HEADS):
                y_ref[slot, p * PROJ_PART_HEADS + j] = y[:, j * HEAD_DIM:(j + 1) * HEAD_DIM]

    def finish(m, slot):
        rows = pl.ds(m * chunk, chunk)
        for hh, kind in enumerate(kinds):
            y = y_ref[slot, hh]
            if kind in (QUERY, KEY):
                gain = gain_ref[0:1, :] if kind == QUERY else gain_ref[1:2, :]
                y = _norm_rope(y, gain, cos_ref[rows, :], sin_ref[rows, :])
            elif kind == GATE:
                y = _sigmoid(y)
            if dilation is None:
                o_ref[rows, hh * HEAD_DIM:(hh + 1) * HEAD_DIM] = y.astype(BF16)
            else:
                sec, lanes = divmod(hh * HEAD_DIM, DIL_GROUP_WIDTH)
                lanes = slice(lanes, lanes + HEAD_DIM)
                n = chunk // dilation
                dst = pl.ds(m * n, n)
                if dilation == 1:
                    o_ref[sec, 0, dst, lanes] = y.astype(BF16)
                elif dilation <= MAX_SINGLE_OP_STRIDE:
                    slab_ref = slab[0]
                    slab_ref[hh] = y
                    for c in range(dilation):
                        o_ref[sec, c, dst, lanes] = slab_ref[hh, pl.ds(c, n, stride=dilation), :].astype(BF16)
                else:
                    slab_ref, slab2_ref = slab
                    step = MAX_SINGLE_OP_STRIDE
                    assert dilation == step * step
                    n1 = chunk // step
                    slab_ref[hh] = y
                    for c_lo in range(step):
                        slab2_ref[hh, c_lo * n1:(c_lo + 1) * n1, :] = slab_ref[hh, pl.ds(c_lo, n1, stride=step), :]
                    for c_lo in range(step):
                        for c_hi in range(step):
                            o_ref[sec, c_lo + step * c_hi, dst, lanes] = slab2_ref[
                                hh, pl.ds(c_lo * n1 + c_hi, n, stride=step), :].astype(BF16)

    compute(0, 0)
    for m in range(n_chunks - 1):
        finish(m, m % 2)
        compute(m + 1, (m + 1) % 2)
    finish(n_chunks - 1, (n_chunks - 1) % 2)


def _proj(h, w, gains, cos, sin, *, parts, kinds, tm, name, dil=None):
    T = h.shape[0]
    part_width = PROJ_PART_WIDTH
    width = len(kinds) * HEAD_DIM
    assert width == len(parts) * part_width and w.shape[1] % part_width == 0
    chunk = PROJ_CHUNK
    scratch = [pltpu.VMEM((2, len(kinds), chunk, HEAD_DIM), F32)]
    if dil is None:
        dilation = None
        out_spec = pl.BlockSpec((tm, width), lambda i: (i, 0))
        out_shape = jax.ShapeDtypeStruct((T, width), BF16)
    else:
        dilation, batch, seq = dil
        n_i = seq // tm
        out_spec = pl.BlockSpec((3, None, dilation, tm // dilation, DIL_GROUP_WIDTH),
                                lambda i: (0, i // n_i, 0, i % n_i, 0))
        out_shape = jax.ShapeDtypeStruct((3, batch, dilation, seq // dilation, DIL_GROUP_WIDTH), BF16)
        if dilation > 1:
            scratch.append(pltpu.VMEM((len(kinds), chunk, HEAD_DIM), F32))
        if dilation > MAX_SINGLE_OP_STRIDE:
            scratch.append(pltpu.VMEM((len(kinds), chunk, HEAD_DIM), F32))
    tok = lambda cols: pl.BlockSpec((tm, cols), lambda i: (i, 0))
    w_specs = [pl.BlockSpec((D_MODEL, part_width), functools.partial(lambda blk, i: (0, blk), blk),
                            pipeline_mode=pl.Buffered(1)) for blk in parts]
    return pl.pallas_call(
        functools.partial(_proj_kernel, kinds=kinds, chunk=chunk, dilation=dilation),
        grid=(T // tm,),
        in_specs=[tok(D_MODEL), *w_specs, pl.BlockSpec((2, HEAD_DIM), lambda i: (0, 0)),
                  tok(HEAD_DIM), tok(HEAD_DIM)],
        out_specs=out_spec,
        out_shape=out_shape,
        scratch_shapes=scratch,
        compiler_params=_params("parallel"),
        name=name,
    )(h, *([w] * len(parts)), gains, cos, sin)


def _dil_attn_kernel(*refs):
    ins, o_ref, scr = refs[:15], refs[15], refs[16:]
    i = pl.program_id(2)
    steps = DIL_BLOCK // BAND

    row = lax.broadcasted_iota(jnp.int32, (BAND, 2 * BAND), 0)
    col = lax.broadcasted_iota(jnp.int32, (BAND, 2 * BAND), 1)
    band_bias = jnp.where((col >= row) & (col <= row + BAND), 0.0, NEG_BIG)
    start_bias = jnp.where(i == 0, jnp.where(col >= BAND, band_bias, NEG_BIG), band_bias)

    groups = []
    for g, r in enumerate(DILATIONS):
        q_ref, kc_ref, kp_ref, vc_ref, vp_ref = ins[5 * g:5 * g + 5]
        kx_ref, vx_ref, on_ref, ln_ref = scr[4 * g:4 * g + 4]
        kx_ref[:, :BAND, :] = kp_ref[...]
        kx_ref[:, BAND:, :] = kc_ref[...]
        vx_ref[:, :BAND, :HEAD_DIM] = vp_ref[...]
        vx_ref[:, BAND:, :HEAD_DIM] = vc_ref[...]
        vx_ref[:, :, HEAD_DIM:] = jnp.ones((r, vx_ref.shape[1], HEAD_DIM), BF16)
        groups.append((r, q_ref, kx_ref, vx_ref, on_ref, ln_ref))

    def body(t, carry):
        for r, q_ref, kx_ref, vx_ref, on_ref, ln_ref in groups:
            tiles = DIL_BLOCK // r // BAND
            if tiles == steps:
                c, mb = 0, t
            elif tiles == 1:
                c, mb = t, 0
            else:
                c, mb = t % r, t // r
            m0 = mb * BAND if isinstance(mb, int) else pl.multiple_of(mb * BAND, BAND)
            q = q_ref[c, pl.ds(m0, BAND), :]
            k = kx_ref[c, pl.ds(m0, 2 * BAND), :]
            v = vx_ref[c, pl.ds(m0, 2 * BAND), :]
            if isinstance(mb, int):
                bias = start_bias if mb == 0 else band_bias
            else:
                bias = jnp.where(mb == 0, start_bias, band_bias)
            s = _dot_nt(q, k) + bias
            m = jnp.max(s, axis=-1, keepdims=True)
            p = jnp.exp(s - m)
            od = _dot(p.astype(BF16), v)
            den = od[:, HEAD_DIM:]
            dst = pl.ds(mb * (BAND * r) + c, BAND, stride=r) if r > 1 else pl.ds(m0, BAND)
            on_ref[dst, :] = od[:, :HEAD_DIM] / den
            ln_ref[dst, :] = m + jnp.log(den)
        return carry

    lax.fori_loop(0, steps, body, 0, unroll=DIL_ATTN_UNROLL)

    lses = [grp[5][...] for grp in groups]
    top = jnp.maximum(jnp.maximum(lses[0], lses[1]), lses[2])
    ws = [jnp.exp(l - top) for l in lses]
    num = ws[0] * groups[0][4][...] + ws[1] * groups[1][4][...] + ws[2] * groups[2][4][...]
    o_ref[...] = (num / (ws[0] + ws[1] + ws[2])).astype(BF16)


def _dil_attn(qkvs, *, batch, seq):
    n_i = seq // DIL_BLOCK
    in_specs, scratch = [], []
    args = []
    for g, r in enumerate(DILATIONS):
        nq = DIL_BLOCK // r

        def cur(s, nq=nq, r=r):
            return pl.BlockSpec((None, None, r, nq, HEAD_DIM), lambda b, h, i: (s, b, 0, i, h))

        def prev(s, nq=nq, r=r):
            per = nq // BAND
            return pl.BlockSpec((None, None, r, BAND, HEAD_DIM),
                                lambda b, h, i: (s, b, 0, jnp.maximum(i * per - 1, 0), h))

        in_specs += [cur(0), cur(1), prev(1), cur(2), prev(2)]
        args += [qkvs[g]] * 5
        scratch += [pltpu.VMEM((r, BAND + nq, HEAD_DIM), BF16), pltpu.VMEM((r, BAND + nq, 2 * HEAD_DIM), BF16),
                    pltpu.VMEM((DIL_BLOCK, HEAD_DIM), F32), pltpu.VMEM((DIL_BLOCK, HEAD_DIM), F32)]
    return pl.pallas_call(
        _dil_attn_kernel,
        grid=(batch, DIL_HEADS, n_i),
        in_specs=in_specs,
        out_specs=pl.BlockSpec((None, DIL_BLOCK, HEAD_DIM), lambda b, h, i: (b, i, h)),
        out_shape=jax.ShapeDtypeStruct((batch, seq, DIL_GROUP_WIDTH), BF16),
        scratch_shapes=scratch,
        compiler_params=_params("parallel", "parallel", "arbitrary"),
        name="dil_attn",
    )(*args)


def _diff_attn_kernel(lq1_ref, lk1_ref, lq2_ref, lk2_ref, q_ref, k_ref, v_ref, sub_ref, o_ref,
                      vt_ref, acc_ref, s0_ref, s1_ref, *, lam_init):
    qi = pl.program_id(2)
    tq = q_ref.shape[0]
    tk = vt_ref.shape[2]
    dv = DIFF_HEAD_WIDTH
    per_q = tq // tk

    @pl.when(qi == 0)
    def _():
        for ch in range(vt_ref.shape[0]):
            vt_ref[ch, :dv, :] = v_ref[ch * tk:(ch + 1) * tk, :].astype(F32).T.astype(BF16)
            vt_ref[ch, dv:, :] = jnp.ones((vt_ref.shape[1] - dv, tk), BF16)

    q = q_ref[...]
    acc_ref[...] = jnp.zeros(acc_ref.shape, F32)

    def scores(kb, s_ref):
        k = k_ref[pl.ds(pl.multiple_of(kb * tk, tk), tk), :]
        for c in range(2):
            lanes = slice(c * HEAD_DIM, (c + 1) * HEAD_DIM)
            s_ref[c] = _dot_nt(k[:, lanes], q[:, lanes])

    def update(kb, s_ref, maxes, qcols=slice(None), mask=None):
        vt = vt_ref[kb]
        out = []
        for c in range(2):
            s = s_ref[c, :, qcols]
            if mask is not None:
                s = jnp.where(mask, s, NEG_BIG)
            m_new = jnp.maximum(maxes[c], jnp.max(s, axis=0, keepdims=True))
            alpha = jnp.exp2(maxes[c] - m_new)
            p = jnp.exp2(s - m_new)
            out.append(m_new)
            acc_ref[c, :, qcols] = alpha * acc_ref[c, :, qcols] + _dot(vt, p.astype(BF16))
        return tuple(out)

    def pair(j, maxes):
        kb = per_q * j
        scores(kb + 1, s1_ref)
        maxes = update(kb, s0_ref, maxes)
        scores(kb + 2, s0_ref)
        return update(kb + 1, s1_ref, maxes)

    row0 = jnp.full((1, tq), NEG_BIG, F32)
    scores(0, s0_ref)
    maxes = lax.fori_loop(0, qi, pair, (row0, row0))

    kd = per_q * qi
    lo, hi = slice(0, tk), slice(tk, tq)
    key = lax.broadcasted_iota(jnp.int32, (tk, tk), 0)
    qry = lax.broadcasted_iota(jnp.int32, (tk, tk), 1)
    tri = key <= qry
    k_last = k_ref[pl.ds(pl.multiple_of((kd + 1) * tk, tk), tk), :]
    for c in range(2):
        lanes = slice(c * HEAD_DIM, (c + 1) * HEAD_DIM)
        s1_ref[c, :, hi] = _dot_nt(k_last[:, lanes], q[tk:, lanes])
    update(kd, s0_ref, tuple(m[:, lo] for m in maxes), lo, tri)
    m_hi = update(kd, s0_ref, tuple(m[:, hi] for m in maxes), hi)
    update(kd + 1, s1_ref, m_hi, hi, tri)

    lam = (jnp.exp(jnp.sum(lq1_ref[...] * lk1_ref[...], axis=-1, keepdims=True))
           - jnp.exp(jnp.sum(lq2_ref[...] * lk2_ref[...], axis=-1, keepdims=True)) + lam_init)
    o = (acc_ref[0, :dv] / acc_ref[0, dv:dv + 1] - lam * (acc_ref[1, :dv] / acc_ref[1, dv:dv + 1])).T
    o_ref[...] = (o * _rms_scale(o) * sub_ref[...] * (1.0 - lam_init)).astype(BF16)


def _diff_attn(qk, vg, lq1, lk1, lq2, lk2, subln, *, batch, seq, lam_init):
    tq, tk = DIFF_TQ, DIFF_TK
    assert tq == 2 * tk
    q_blk, k_blk, v_blk = 0, DIFF_HEADS, 0
    vec = pl.BlockSpec((1, HEAD_DIM), lambda b, h, i: (0, 0))
    return pl.pallas_call(
        functools.partial(_diff_attn_kernel, lam_init=lam_init),
        grid=(batch, DIFF_HEADS, seq // tq),
        in_specs=[
            vec, vec, vec, vec,
            pl.BlockSpec((None, tq, DIFF_HEAD_WIDTH), lambda b, h, i: (b, i, q_blk + h)),
            pl.BlockSpec((None, seq, DIFF_HEAD_WIDTH), lambda b, h, i: (b, 0, k_blk + h)),
            pl.BlockSpec((None, seq, DIFF_HEAD_WIDTH), lambda b, h, i: (b, 0, v_blk + h)),
            pl.BlockSpec((1, DIFF_HEAD_WIDTH), lambda b, h, i: (0, 0)),
        ],
        out_specs=pl.BlockSpec((None, tq, DIFF_HEAD_WIDTH), lambda b, h, i: (b, i, h)),
        out_shape=jax.ShapeDtypeStruct((batch, seq, DIFF_WIDTH), BF16),
        scratch_shapes=[pltpu.VMEM((seq // tk, DIFF_HEAD_WIDTH + DIFF_ONES_ROWS, tk), BF16),
                        pltpu.VMEM((2, DIFF_HEAD_WIDTH + DIFF_ONES_ROWS, tq), F32),
                        pltpu.VMEM((2, tk, tq), F32),
                        pltpu.VMEM((2, tk, tq), F32)],
        compiler_params=_params("parallel", "parallel", "arbitrary"),
        name="diff_attn",
    )(lq1, lk1, lq2, lk2, qk, qk, vg, subln)


def _merge_kernel(x_ref, od_ref, of_ref, gd0_ref, gd1_ref, gf0_ref, gf1_ref, wa_ref, wb_ref, wo_ref, o_ref):
    gd = jnp.concatenate([gd0_ref[...], gd1_ref[...]], axis=1).astype(F32)
    gf = jnp.concatenate([gf0_ref[...], gf1_ref[...]], axis=1).astype(F32)
    y = gd * _dot(od_ref[...], wa_ref[...]) + gf * _dot(of_ref[...], wb_ref[...])
    o_ref[...] = x_ref[...] + _dot(y.astype(BF16), wo_ref[...])


def _merge(x, o_dil, o_diff, vg, wa, wb, wo):
    T = x.shape[0]
    tm = MERGE_TM
    half = D_MODEL // 2

    def tok(width, blk=0):
        return pl.BlockSpec((tm, width), lambda i: (i, blk))

    def resident(shape):
        return pl.BlockSpec(shape, lambda i: (0, 0), pipeline_mode=pl.Buffered(1))

    gate0 = DIFF_WIDTH // half
    return pl.pallas_call(
        _merge_kernel,
        grid=(T // tm,),
        in_specs=[tok(D_MODEL), tok(DIL_GROUP_WIDTH), tok(DIFF_WIDTH),
                  tok(half, gate0), tok(half, gate0 + 1), tok(half, gate0 + 2), tok(half, gate0 + 3),
                  resident(wa.shape), resident(wb.shape), resident(wo.shape)],
        out_specs=tok(D_MODEL),
        out_shape=jax.ShapeDtypeStruct((T, D_MODEL), F32),
        compiler_params=_params("parallel"),
        name="merge",
    )(x, o_dil, o_diff, vg, vg, vg, vg, wa, wb, wo)


def _rope_tables(positions):
    lane = jnp.arange(HEAD_DIM)
    freq = ROPE_THETA ** (-(2 * (lane % ROPE_HALF)).astype(F32) / ROPE_DIM)
    inv = jnp.where(lane < ROPE_DIM, freq, 0.0)
    sign = jnp.where(lane < ROPE_HALF, -1.0, jnp.where(lane < ROPE_DIM, 1.0, 0.0)).astype(F32)
    ang = positions.astype(F32).reshape(-1, 1) * inv
    return jnp.cos(ang), jnp.sin(ang) * sign


def _layer(x, tables, layer, ffn1_norm, ffn1_w_gate, ffn1_w_up, ffn1_w_down, mix_norm, w_in,
           dil_q_norm, dil_k_norm, diff_q_norm, diff_k_norm, diff_lq1, diff_lk1, diff_lq2, diff_lk2,
           diff_subln, w_dil_branch, w_diff_branch, w_out, ffn2_norm, ffn2_w_gate, ffn2_w_up, ffn2_w_down,
           *, batch, seq):
    cos, sin = tables
    row = lambda v: v.reshape(1, -1).astype(F32)
    bf = lambda w: w.astype(BF16)
    qk_scale = HEAD_DIM ** -0.5
    lam_init = 0.8 - 0.6 * math.exp(-0.3 * layer)

    x1, h = _ffn(x, row(ffn1_norm), bf(ffn1_w_gate), bf(ffn1_w_up), bf(ffn1_w_down), row(mix_norm))

    w_in_bf = bf(w_in)
    blocks = lambda start, width: tuple(range(start // PROJ_PART_WIDTH, (start + width) // PROJ_PART_WIDTH))
    diff0 = 3 * DIL_WIDTH
    fv0 = diff0 + 2 * DIFF_WIDTH
    gains = lambda gq, gk: jnp.stack([gq * qk_scale, gk]).astype(F32)

    dil_kinds = (QUERY,) * DIL_HEADS + (KEY,) * DIL_HEADS + (PLAIN,) * DIL_HEADS
    dil_gains = gains(dil_q_norm, dil_k_norm)
    qkvs = [_proj(h, w_in_bf, dil_gains, cos, sin, parts=tuple(sec * N_DIL_GROUPS + g for sec in range(3)),
                  kinds=dil_kinds, tm=PROJ_TM, name=f"dil_proj{g}", dil=(r, batch, seq))
            for g, r in enumerate(DILATIONS)]

    n_diff = DIFF_WIDTH // HEAD_DIM
    qk = _proj(h, w_in_bf, gains(diff_q_norm * LOG2_E, diff_k_norm), cos, sin, parts=blocks(diff0, 2 * DIFF_WIDTH),
               kinds=(QUERY,) * n_diff + (KEY,) * n_diff, tm=PROJ_TM, name="diff_qk_proj")
    n_gate = 2 * D_MODEL // HEAD_DIM
    vg = _proj(h, w_in_bf, dil_gains, cos, sin, parts=blocks(fv0, DIFF_WIDTH + 2 * D_MODEL),
               kinds=(PLAIN,) * n_diff + (GATE,) * n_gate, tm=PROJ_GATE_TM, name="v_gate_proj")

    o_dil = _dil_attn(qkvs, batch=batch, seq=seq)
    o_diff = _diff_attn(qk.reshape(batch, seq, -1), vg.reshape(batch, seq, -1), row(diff_lq1), row(diff_lk1),
                        row(diff_lq2), row(diff_lk2), row(diff_subln), batch=batch, seq=seq, lam_init=lam_init)

    x2 = _merge(x1, o_dil.reshape(batch * seq, -1), o_diff.reshape(batch * seq, -1), vg,
                bf(w_dil_branch), bf(w_diff_branch), bf(w_out))
    (out,) = _ffn(x2, row(ffn2_norm), bf(ffn2_w_gate), bf(ffn2_w_up), bf(ffn2_w_down), None)
    return out


def kernel(x, positions, ffn1_norm, ffn1_w_gate, ffn1_w_up, ffn1_w_down, mix_norm, w_in, dil_q_norm, dil_k_norm, diff_q_norm, diff_k_norm, diff_lq1, diff_lk1, diff_lq2, diff_lk2, diff_subln, w_dil_branch, w_diff_branch, w_out, ffn2_norm, ffn2_w_gate, ffn2_w_up, ffn2_w_down):
    batch, seq, d_model = x.shape
    assert d_model == D_MODEL and seq % DIL_BLOCK == 0
    weights = (ffn1_norm, ffn1_w_gate, ffn1_w_up, ffn1_w_down, mix_norm, w_in, dil_q_norm, dil_k_norm,
               diff_q_norm, diff_k_norm, diff_lq1, diff_lk1, diff_lq2, diff_lk2, diff_subln, w_dil_branch,
               w_diff_branch, w_out, ffn2_norm, ffn2_w_gate, ffn2_w_up, ffn2_w_down)
    tables = _rope_tables(positions)
    y = x.reshape(batch * seq, d_model)
    for layer in range(ffn1_norm.shape[0]):
        y = _layer(y, tables, layer, *(w[layer] for w in weights), batch=batch, seq=seq)
    return y.reshape(batch, seq, d_model)
```

```python
import functools
import math

import jax
import jax.numpy as jnp
from jax import lax
from jax.experimental import pallas as pl
from jax.experimental.pallas import tpu as pltpu

F32 = jnp.float32
BF16 = jnp.bfloat16

D_MODEL = 2048
D_FF = 5632
HEAD_DIM = 128
ROPE_DIM = HEAD_DIM // 4
ROPE_HALF = ROPE_DIM // 2
ROPE_THETA = 500000.0
RMS_EPS = 1e-6
BAND = 128
DILATIONS = (1, 4, 16)
N_DIL_GROUPS = len(DILATIONS)
DIL_HEADS = 4
DIL_GROUP_WIDTH = DIL_HEADS * HEAD_DIM
DIL_WIDTH = N_DIL_GROUPS * DIL_GROUP_WIDTH
DIFF_HEADS = 4
DIFF_HEAD_WIDTH = 2 * HEAD_DIM
DIFF_WIDTH = DIFF_HEADS * DIFF_HEAD_WIDTH
DIL_BLOCK = BAND * DILATIONS[-1]
LOG2_E = math.log2(math.e)
NEG_BIG = -1e30

VMEM_LIMIT_BYTES = 62 * 1024 * 1024

FFN_TM = 1024
FFN_TF = 512
PROJ_TM = 1024
PROJ_GATE_TM = 512
PROJ_CHUNK = 256
PROJ_PART_HEADS = 4
PROJ_PART_WIDTH = PROJ_PART_HEADS * HEAD_DIM
MAX_SINGLE_OP_STRIDE = 4
MERGE_TM = 512
DIFF_TQ = 1024
DIFF_TK = 512
BF16_SUBLANES = 16
DIFF_ONES_ROWS = BF16_SUBLANES
DIL_ATTN_UNROLL = 8


def _params(*sem):
    return pltpu.CompilerParams(dimension_semantics=sem, vmem_limit_bytes=VMEM_LIMIT_BYTES)


def _rms_scale(x):
    return lax.rsqrt(jnp.mean(x * x, axis=-1, keepdims=True) + RMS_EPS)


def _dot(a, b):
    return jnp.dot(a, b, preferred_element_type=F32)


def _dot_nt(a, b):
    return lax.dot_general(a, b, (((1,), (1,)), ((), ())), preferred_element_type=F32)


def _ffn_kernel(*refs, emit_norm):
    if emit_norm:
        x_ref, gin_ref, wg_ref, wu_ref, wd_ref, gout_ref, o_ref, hn_ref, h_ref = refs
    else:
        x_ref, gin_ref, wg_ref, wu_ref, wd_ref, o_ref, h_ref = refs
    f = pl.program_id(1)
    last = pl.num_programs(1) - 1

    def step(h):
        g = _dot(h, wg_ref[...])
        u = _dot(h, wu_ref[...])
        a = (g * jax.nn.sigmoid(g)) * u * 0.5
        o_ref[...] += _dot(a.astype(BF16), wd_ref[...])

    def finish():
        if emit_norm:
            y = o_ref[...]
            hn_ref[...] = (y * _rms_scale(y) * gout_ref[...]).astype(BF16)

    @pl.when(f == 0)
    def _():
        x = x_ref[...]
        h = (x * _rms_scale(x) * gin_ref[...]).astype(BF16)
        h_ref[...] = h
        o_ref[...] = x
        step(h)

    @pl.when((f > 0) & (f < last))
    def _():
        step(h_ref[...])

    @pl.when(f == last)
    def _():
        step(h_ref[...])
        finish()


def _ffn(x, g_in, wg, wu, wd, g_out):
    T = x.shape[0]
    tm, tf = FFN_TM, FFN_TF
    emit_norm = g_out is not None
    tok = pl.BlockSpec((tm, D_MODEL), lambda i, f: (i, 0))
    vec = pl.BlockSpec((1, D_MODEL), lambda i, f: (0, 0))
    in_specs = [tok, vec,
                pl.BlockSpec((D_MODEL, tf), lambda i, f: (0, f)),
                pl.BlockSpec((D_MODEL, tf), lambda i, f: (0, f)),
                pl.BlockSpec((tf, D_MODEL), lambda i, f: (f, 0))]
    args = [x, g_in, wg, wu, wd]
    out_specs = [tok]
    out_shape = [jax.ShapeDtypeStruct((T, D_MODEL), F32)]
    if emit_norm:
        in_specs.append(vec)
        args.append(g_out)
        out_specs.append(tok)
        out_shape.append(jax.ShapeDtypeStruct((T, D_MODEL), BF16))
    return pl.pallas_call(
        functools.partial(_ffn_kernel, emit_norm=emit_norm),
        grid=(T // tm, D_FF // tf),
        in_specs=in_specs,
        out_specs=out_specs,
        out_shape=out_shape,
        scratch_shapes=[pltpu.VMEM((tm, D_MODEL), BF16)],
        compiler_params=_params("parallel", "arbitrary"),
        name="ffn",
    )(*args)


QUERY, KEY, PLAIN, GATE = "query", "key", "plain", "gate"


def _sigmoid(y):
    return 0.5 * jnp.tanh(0.5 * y) + 0.5


def _norm_rope(y, gain, cos, sin):
    y = y * _rms_scale(y) * gain
    lane = lax.broadcasted_iota(jnp.int32, y.shape, 1)
    partner = jnp.where(lane < ROPE_HALF, pltpu.roll(y, HEAD_DIM - ROPE_HALF, 1), pltpu.roll(y, ROPE_HALF, 1))
    return y * cos + partner * sin


def _proj_kernel(h_ref, *refs, kinds, chunk, dilation):
    n_parts = len(kinds) // PROJ_PART_HEADS
    w_refs = refs[:n_parts]
    gain_ref, cos_ref, sin_ref, o_ref, y_ref = refs[n_parts:n_parts + 5]
    slab = refs[n_parts + 5:]
    tm = h_ref.shape[0]
    n_chunks = tm // chunk

    def compute(m, slot):
        h = h_ref[pl.ds(m * chunk, chunk), :]
        for p, w_ref in enumerate(w_refs):
            y = _dot(h, w_ref[...])
            for j in range(PROJ_PART_HEADS):
                y_ref[slot, p * PROJ_PART_HEADS + j] = y[:, j * HEAD_DIM:(j + 1) * HEAD_DIM]

    def finish(m, slot):
        rows = pl.ds(m * chunk, chunk)
        for hh, kind in enumerate(kinds):
            y = y_ref[slot, hh]
            if kind in (QUERY, KEY):
                gain = gain_ref[0:1, :] if kind == QUERY else gain_ref[1:2, :]
                y = _norm_rope(y, gain, cos_ref[rows, :], sin_ref[rows, :])
            elif kind == GATE:
                y = _sigmoid(y)
            if dilation is None:
                o_ref[rows, hh * HEAD_DIM:(hh + 1) * HEAD_DIM] = y.astype(BF16)
            else:
                sec, lanes = divmod(hh * HEAD_DIM, DIL_GROUP_WIDTH)
                lanes = slice(lanes, lanes + HEAD_DIM)
                n = chunk // dilation
                dst = pl.ds(m * n, n)
                if dilation == 1:
                    o_ref[sec, 0, dst, lanes] = y.astype(BF16)
                elif dilation <= MAX_SINGLE_OP_STRIDE:
                    slab_ref = slab[0]
                    slab_ref[hh] = y
                    for c in range(dilation):
                        o_ref[sec, c, dst, lanes] = slab_ref[hh, pl.ds(c, n, stride=dilation), :].astype(BF16)
                else:
                    slab_ref, slab2_ref = slab
                    step = MAX_SINGLE_OP_STRIDE
                    assert dilation == step * step
                    n1 = chunk // step
                    slab_ref[hh] = y
                    for c_lo in range(step):
                        slab2_ref[hh, c_lo * n1:(c_lo + 1) * n1, :] = slab_ref[hh, pl.ds(c_lo, n1, stride=step), :]
                    for c_lo in range(step):
                        for c_hi in range(step):
                            o_ref[sec, c_lo + step * c_hi, dst, lanes] = slab2_ref[
                                hh, pl.ds(c_lo * n1 + c_hi, n, stride=step), :].astype(BF16)

    compute(0, 0)
    for m in range(n_chunks - 1):
        finish(m, m % 2)
        compute(m + 1, (m + 1) % 2)
    finish(n_chunks - 1, (n_chunks - 1) % 2)


def _proj(h, w, gains, cos, sin, *, parts, kinds, tm, name, dil=None):
    T = h.shape[0]
    part_width = PROJ_PART_WIDTH
    width = len(kinds) * HEAD_DIM
    assert width == len(parts) * part_width and w.shape[1] % part_width == 0
    chunk = PROJ_CHUNK
    scratch = [pltpu.VMEM((2, len(kinds), chunk, HEAD_DIM), F32)]
    if dil is None:
        dilation = None
        out_spec = pl.BlockSpec((tm, width), lambda i: (i, 0))
        out_shape = jax.ShapeDtypeStruct((T, width), BF16)
    else:
        dilation, batch, seq = dil
        n_i = seq // tm
        out_spec = pl.BlockSpec((3, None, dilation, tm // dilation, DIL_GROUP_WIDTH),
                                lambda i: (0, i // n_i, 0, i % n_i, 0))
        out_shape = jax.ShapeDtypeStruct((3, batch, dilation, seq // dilation, DIL_GROUP_WIDTH), BF16)
        if dilation > 1:
            scratch.append(pltpu.VMEM((len(kinds), chunk, HEAD_DIM), F32))
        if dilation > MAX_SINGLE_OP_STRIDE:
            scratch.append(pltpu.VMEM((len(kinds), chunk, HEAD_DIM), F32))
    tok = lambda cols: pl.BlockSpec((tm, cols), lambda i: (i, 0))
    w_specs = [pl.BlockSpec((D_MODEL, part_width), functools.partial(lambda blk, i: (0, blk), blk),
                            pipeline_mode=pl.Buffered(1)) for blk in parts]
    return pl.pallas_call(
        functools.partial(_proj_kernel, kinds=kinds, chunk=chunk, dilation=dilation),
        grid=(T // tm,),
        in_specs=[tok(D_MODEL), *w_specs, pl.BlockSpec((2, HEAD_DIM), lambda i: (0, 0)),
                  tok(HEAD_DIM), tok(HEAD_DIM)],
        out_specs=out_spec,
        out_shape=out_shape,
        scratch_shapes=scratch,
        compiler_params=_params("parallel"),
        name=name,
    )(h, *([w] * len(parts)), gains, cos, sin)


def _dil_attn_kernel(*refs):
    ins, o_ref, scr = refs[:15], refs[15], refs[16:]
    i = pl.program_id(2)
    steps = DIL_BLOCK // BAND

    row = lax.broadcasted_iota(jnp.int32, (BAND, 2 * BAND), 0)
    col = lax.broadcasted_iota(jnp.int32, (BAND, 2 * BAND), 1)
    band_bias = jnp.where((col >= row) & (col <= row + BAND), 0.0, NEG_BIG)
    start_bias = jnp.where(i == 0, jnp.where(col >= BAND, band_bias, NEG_BIG), band_bias)

    groups = []
    for g, r in enumerate(DILATIONS):
        q_ref, kc_ref, kp_ref, vc_ref, vp_ref = ins[5 * g:5 * g + 5]
        kx_ref, vx_ref, on_ref, ln_ref = scr[4 * g:4 * g + 4]
        kx_ref[:, :BAND, :] = kp_ref[...]
        kx_ref[:, BAND:, :] = kc_ref[...]
        vx_ref[:, :BAND, :HEAD_DIM] = vp_ref[...]
        vx_ref[:, BAND:, :HEAD_DIM] = vc_ref[...]
        vx_ref[:, :, HEAD_DIM:] = jnp.ones((r, vx_ref.shape[1], HEAD_DIM), BF16)
        groups.append((r, q_ref, kx_ref, vx_ref, on_ref, ln_ref))

    def body(t, carry):
        for r, q_ref, kx_ref, vx_ref, on_ref, ln_ref in groups:
            tiles = DIL_BLOCK // r // BAND
            if tiles == steps:
                c, mb = 0, t
            elif tiles == 1:
                c, mb = t, 0
            else:
                c, mb = t % r, t // r
            m0 = mb * BAND if isinstance(mb, int) else pl.multiple_of(mb * BAND, BAND)
            q = q_ref[c, pl.ds(m0, BAND), :]
            k = kx_ref[c, pl.ds(m0, 2 * BAND), :]
            v = vx_ref[c, pl.ds(m0, 2 * BAND), :]
            if isinstance(mb, int):
                bias = start_bias if mb == 0 else band_bias
            else:
                bias = jnp.where(mb == 0, start_bias, band_bias)
            s = _dot_nt(q, k) + bias
            m = jnp.max(s, axis=-1, keepdims=True)
            p = jnp.exp(s - m)
            od = _dot(p.astype(BF16), v)
            den = od[:, HEAD_DIM:]
            dst = pl.ds(mb * (BAND * r) + c, BAND, stride=r) if r > 1 else pl.ds(m0, BAND)
            on_ref[dst, :] = od[:, :HEAD_DIM] / den
            ln_ref[dst, :] = m + jnp.log(den)
        return carry

    lax.fori_loop(0, steps, body, 0, unroll=DIL_ATTN_UNROLL)

    lses = [grp[5][...] for grp in groups]
    top = jnp.maximum(jnp.maximum(lses[0], lses[1]), lses[2])
    ws = [jnp.exp(l - top) for l in lses]
    num = ws[0] * groups[0][4][...] + ws[1] * groups[1][4][...] + ws[2] * groups[2][4][...]
    o_ref[...] = (num / (ws[0] + ws[1] + ws[2])).astype(BF16)


def _dil_attn(qkvs, *, batch, seq):
    n_i = seq // DIL_BLOCK
    in_specs, scratch = [], []
    args = []
    for g, r in enumerate(DILATIONS):
        nq = DIL_BLOCK // r

        def cur(s, nq=nq, r=r):
            return pl.BlockSpec((None, None, r, nq, HEAD_DIM), lambda b, h, i: (s, b, 0, i, h))

        def prev(s, nq=nq, r=r):
            per = nq // BAND
            return pl.BlockSpec((None, None, r, BAND, HEAD_DIM),
                                lambda b, h, i: (s, b, 0, jnp.maximum(i * per - 1, 0), h))

        in_specs += [cur(0), cur(1), prev(1), cur(2), prev(2)]
        args += [qkvs[g]] * 5
        scratch += [pltpu.VMEM((r, BAND + nq, HEAD_DIM), BF16), pltpu.VMEM((r, BAND + nq, 2 * HEAD_DIM), BF16),
                    pltpu.VMEM((DIL_BLOCK, HEAD_DIM), F32), pltpu.VMEM((DIL_BLOCK, HEAD_DIM), F32)]
    return pl.pallas_call(
        _dil_attn_kernel,
        grid=(batch, DIL_HEADS, n_i),
        in_specs=in_specs,
        out_specs=pl.BlockSpec((None, DIL_BLOCK, HEAD_DIM), lambda b, h, i: (b, i, h)),
        out_shape=jax.ShapeDtypeStruct((batch, seq, DIL_GROUP_WIDTH), BF16),
        scratch_shapes=scratch,
        compiler_params=_params("parallel", "parallel", "arbitrary"),
        name="dil_attn",
    )(*args)


def _diff_attn_kernel(lq1_ref, lk1_ref, lq2_ref, lk2_ref, q_ref, k_ref, v_ref, sub_ref, *refs, lam_init, n_casts):
    cast_in, (o_ref, *cast_out) = refs[:n_casts], refs[n_casts:2 * n_casts + 1]
    vt_ref, acc_ref, s0_ref, s1_ref = refs[2 * n_casts + 1:]
    for src_ref, dst_ref in zip(cast_in, cast_out):
        dst_ref[...] = src_ref[...].astype(BF16)

    qi = pl.program_id(2)
    tq = q_ref.shape[0]
    tk = vt_ref.shape[2]
    dv = DIFF_HEAD_WIDTH
    per_q = tq // tk

    @pl.when(qi == 0)
    def _():
        for ch in range(vt_ref.shape[0]):
            vt_ref[ch, :dv, :] = v_ref[ch * tk:(ch + 1) * tk, :].astype(F32).T.astype(BF16)
            vt_ref[ch, dv:, :] = jnp.ones((vt_ref.shape[1] - dv, tk), BF16)

    q = q_ref[...]
    acc_ref[...] = jnp.zeros(acc_ref.shape, F32)

    def scores(kb, s_ref):
        k = k_ref[pl.ds(pl.multiple_of(kb * tk, tk), tk), :]
        for c in range(2):
            lanes = slice(c * HEAD_DIM, (c + 1) * HEAD_DIM)
            s_ref[c] = _dot_nt(k[:, lanes], q[:, lanes])

    def update(kb, s_ref, maxes, qcols=slice(None), mask=None):
        vt = vt_ref[kb]
        out = []
        for c in range(2):
            s = s_ref[c, :, qcols]
            if mask is not None:
                s = jnp.where(mask, s, NEG_BIG)
            m_new = jnp.maximum(maxes[c], jnp.max(s, axis=0, keepdims=True))
            alpha = jnp.exp2(maxes[c] - m_new)
            p = jnp.exp2(s - m_new)
            out.append(m_new)
            acc_ref[c, :, qcols] = alpha * acc_ref[c, :, qcols] + _dot(vt, p.astype(BF16))
        return tuple(out)

    def pair(j, maxes):
        kb = per_q * j
        scores(kb + 1, s1_ref)
        maxes = update(kb, s0_ref, maxes)
        scores(kb + 2, s0_ref)
        return update(kb + 1, s1_ref, maxes)

    row0 = jnp.full((1, tq), NEG_BIG, F32)
    scores(0, s0_ref)
    maxes = lax.fori_loop(0, qi, pair, (row0, row0))

    kd = per_q * qi
    lo, hi = slice(0, tk), slice(tk, tq)
    key = lax.broadcasted_iota(jnp.int32, (tk, tk), 0)
    qry = lax.broadcasted_iota(jnp.int32, (tk, tk), 1)
    tri = key <= qry
    k_last = k_ref[pl.ds(pl.multiple_of((kd + 1) * tk, tk), tk), :]
    for c in range(2):
        lanes = slice(c * HEAD_DIM, (c + 1) * HEAD_DIM)
        s1_ref[c, :, hi] = _dot_nt(k_last[:, lanes], q[tk:, lanes])
    update(kd, s0_ref, tuple(m[:, lo] for m in maxes), lo, tri)
    m_hi = update(kd, s0_ref, tuple(m[:, hi] for m in maxes), hi)
    update(kd + 1, s1_ref, m_hi, hi, tri)

    lam = (jnp.exp(jnp.sum(lq1_ref[...] * lk1_ref[...], axis=-1, keepdims=True))
           - jnp.exp(jnp.sum(lq2_ref[...] * lk2_ref[...], axis=-1, keepdims=True)) + lam_init)
    o = (acc_ref[0, :dv] / acc_ref[0, dv:dv + 1] - lam * (acc_ref[1, :dv] / acc_ref[1, dv:dv + 1])).T
    o_ref[...] = (o * _rms_scale(o) * sub_ref[...] * (1.0 - lam_init)).astype(BF16)


def _diff_attn(qk, vg, lq1, lk1, lq2, lk2, subln, casts, *, batch, seq, lam_init):
    tq, tk = DIFF_TQ, DIFF_TK
    assert tq == 2 * tk
    q_blk, k_blk, v_blk = 0, DIFF_HEADS, 0
    n_q = seq // tq
    steps = batch * DIFF_HEADS * n_q
    vec = pl.BlockSpec((1, HEAD_DIM), lambda b, h, i: (0, 0))

    flat = []
    for w in casts:
        cols = w.shape[-1]
        rows = w.size // cols
        while rows % (steps * BF16_SUBLANES):
            rows, cols = rows * 2, cols // 2
        flat.append(w.reshape(rows, cols))
    cast_specs = [pl.BlockSpec((w.shape[0] // steps, w.shape[1]),
                               lambda b, h, i: ((b * DIFF_HEADS + h) * n_q + i, 0)) for w in flat]
    o, *cast = pl.pallas_call(
        functools.partial(_diff_attn_kernel, lam_init=lam_init, n_casts=len(flat)),
        grid=(batch, DIFF_HEADS, n_q),
        in_specs=[
            vec, vec, vec, vec,
            pl.BlockSpec((None, tq, DIFF_HEAD_WIDTH), lambda b, h, i: (b, i, q_blk + h)),
            pl.BlockSpec((None, seq, DIFF_HEAD_WIDTH), lambda b, h, i: (b, 0, k_blk + h)),
            pl.BlockSpec((None, seq, DIFF_HEAD_WIDTH), lambda b, h, i: (b, 0, v_blk + h)),
            pl.BlockSpec((1, DIFF_HEAD_WIDTH), lambda b, h, i: (0, 0)),
            *cast_specs,
        ],
        out_specs=[pl.BlockSpec((None, tq, DIFF_HEAD_WIDTH), lambda b, h, i: (b, i, h)), *cast_specs],
        out_shape=[jax.ShapeDtypeStruct((batch, seq, DIFF_WIDTH), BF16),
                   *(jax.ShapeDtypeStruct(w.shape, BF16) for w in flat)],
        scratch_shapes=[pltpu.VMEM((seq // tk, DIFF_HEAD_WIDTH + DIFF_ONES_ROWS, tk), BF16),
                        pltpu.VMEM((2, DIFF_HEAD_WIDTH + DIFF_ONES_ROWS, tq), F32),
                        pltpu.VMEM((2, tk, tq), F32),
                        pltpu.VMEM((2, tk, tq), F32)],
        compiler_params=_params("parallel", "parallel", "arbitrary"),
        name="diff_attn",
    )(lq1, lk1, lq2, lk2, qk, qk, vg, subln, *flat)
    return o, [c.reshape(w.shape) for c, w in zip(cast, casts)]


def _merge_kernel(x_ref, od_ref, of_ref, gd0_ref, gd1_ref, gf0_ref, gf1_ref, wa_ref, wb_ref, wo_ref, o_ref):
    gd = jnp.concatenate([gd0_ref[...], gd1_ref[...]], axis=1).astype(F32)
    gf = jnp.concatenate([gf0_ref[...], gf1_ref[...]], axis=1).astype(F32)
    y = gd * _dot(od_ref[...], wa_ref[...]) + gf * _dot(of_ref[...], wb_ref[...])
    o_ref[...] = x_ref[...] + _dot(y.astype(BF16), wo_ref[...])


def _merge(x, o_dil, o_diff, vg, wa, wb, wo):
    T = x.shape[0]
    tm = MERGE_TM
    half = D_MODEL // 2

    def tok(width, blk=0):
        return pl.BlockSpec((tm, width), lambda i: (i, blk))

    def resident(shape):
        return pl.BlockSpec(shape, lambda i: (0, 0), pipeline_mode=pl.Buffered(1))

    gate0 = DIFF_WIDTH // half
    return pl.pallas_call(
        _merge_kernel,
        grid=(T // tm,),
        in_specs=[tok(D_MODEL), tok(DIL_GROUP_WIDTH), tok(DIFF_WIDTH),
                  tok(half, gate0), tok(half, gate0 + 1), tok(half, gate0 + 2), tok(half, gate0 + 3),
                  resident(wa.shape), resident(wb.shape), resident(wo.shape)],
        out_specs=tok(D_MODEL),
        out_shape=jax.ShapeDtypeStruct((T, D_MODEL), F32),
        compiler_params=_params("parallel"),
        name="merge",
    )(x, o_dil, o_diff, vg, vg, vg, vg, wa, wb, wo)


def _rope_tables(positions):
    lane = jnp.arange(HEAD_DIM)
    freq = ROPE_THETA ** (-(2 * (lane % ROPE_HALF)).astype(F32) / ROPE_DIM)
    inv = jnp.where(lane < ROPE_DIM, freq, 0.0)
    sign = jnp.where(lane < ROPE_HALF, -1.0, jnp.where(lane < ROPE_DIM, 1.0, 0.0)).astype(F32)
    ang = positions.astype(F32).reshape(-1, 1) * inv
    return jnp.cos(ang), jnp.sin(ang) * sign


def _layer(x, tables, layer, ffn1_norm, ffn1_w_gate, ffn1_w_up, ffn1_w_down, mix_norm, w_in,
           dil_q_norm, dil_k_norm, diff_q_norm, diff_k_norm, diff_lq1, diff_lk1, diff_lq2, diff_lk2,
           diff_subln, w_dil_branch, w_diff_branch, w_out, ffn2_norm, ffn2_w_gate, ffn2_w_up, ffn2_w_down,
           *, batch, seq):
    cos, sin = tables
    row = lambda v: v.reshape(1, -1).astype(F32)
    bf = lambda w: w.astype(BF16)
    qk_scale = HEAD_DIM ** -0.5
    lam_init = 0.8 - 0.6 * math.exp(-0.3 * layer)

    x1, h = _ffn(x, row(ffn1_norm), bf(ffn1_w_gate), bf(ffn1_w_up), bf(ffn1_w_down), row(mix_norm))

    w_in_bf = bf(w_in)
    blocks = lambda start, width: tuple(range(start // PROJ_PART_WIDTH, (start + width) // PROJ_PART_WIDTH))
    diff0 = 3 * DIL_WIDTH
    fv0 = diff0 + 2 * DIFF_WIDTH
    gains = lambda gq, gk: jnp.stack([gq * qk_scale, gk]).astype(F32)

    dil_kinds = (QUERY,) * DIL_HEADS + (KEY,) * DIL_HEADS + (PLAIN,) * DIL_HEADS
    dil_gains = gains(dil_q_norm, dil_k_norm)
    qkvs = [_proj(h, w_in_bf, dil_gains, cos, sin, parts=tuple(sec * N_DIL_GROUPS + g for sec in range(3)),
                  kinds=dil_kinds, tm=PROJ_TM, name=f"dil_proj{g}", dil=(r, batch, seq))
            for g, r in enumerate(DILATIONS)]

    n_diff = DIFF_WIDTH // HEAD_DIM
    qk = _proj(h, w_in_bf, gains(diff_q_norm * LOG2_E, diff_k_norm), cos, sin, parts=blocks(diff0, 2 * DIFF_WIDTH),
               kinds=(QUERY,) * n_diff + (KEY,) * n_diff, tm=PROJ_TM, name="diff_qk_proj")
    n_gate = 2 * D_MODEL // HEAD_DIM
    vg = _proj(h, w_in_bf, dil_gains, cos, sin, parts=blocks(fv0, DIFF_WIDTH + 2 * D_MODEL),
               kinds=(PLAIN,) * n_diff + (GATE,) * n_gate, tm=PROJ_GATE_TM, name="v_gate_proj")

    o_dil = _dil_attn(qkvs, batch=batch, seq=seq)
    later = [w_dil_branch, w_diff_branch, w_out, ffn2_w_gate, ffn2_w_up, ffn2_w_down]
    o_diff, (wa, wb, wo, wg2, wu2, wd2) = _diff_attn(
        qk.reshape(batch, seq, -1), vg.reshape(batch, seq, -1), row(diff_lq1), row(diff_lk1), row(diff_lq2),
        row(diff_lk2), row(diff_subln), later, batch=batch, seq=seq, lam_init=lam_init)

    x2 = _merge(x1, o_dil.reshape(batch * seq, -1), o_diff.reshape(batch * seq, -1), vg, wa, wb, wo)
    (out,) = _ffn(x2, row(ffn2_norm), wg2, wu2, wd2, None)
    return out


def kernel(x, positions, ffn1_norm, ffn1_w_gate, ffn1_w_up, ffn1_w_down, mix_norm, w_in, dil_q_norm, dil_k_norm, diff_q_norm, diff_k_norm, diff_lq1, diff_lk1, diff_lq2, diff_lk2, diff_subln, w_dil_branch, w_diff_branch, w_out, ffn2_norm, ffn2_w_gate, ffn2_w_up, ffn2_w_down):
    batch, seq, d_model = x.shape
    assert d_model == D_MODEL and seq % DIL_BLOCK == 0
    weights = (ffn1_norm, ffn1_w_gate, ffn1_w_up, ffn1_w_down, mix_norm, w_in, dil_q_norm, dil_k_norm,
               diff_q_norm, diff_k_norm, diff_lq1, diff_lk1, diff_lq2, diff_lk2, diff_subln, w_dil_branch,
               w_diff_branch, w_out, ffn2_norm, ffn2_w_gate, ffn2_w_up, ffn2_w_down)
    tables = _rope_tables(positions)
    y = x.reshape(batch * seq, d_model)
    for layer in range(ffn1_norm.shape[0]):
        y = _layer(y, tables, layer, *(w[layer] for w in weights), batch=batch, seq=seq)
    return y.reshape(batch, seq, d_model)
```

```python
import functools
import math

import jax
import jax.numpy as jnp
from jax import lax
from jax.experimental import pallas as pl
from jax.experimental.pallas import tpu as pltpu

F32 = jnp.float32
BF16 = jnp.bfloat16

D_MODEL = 2048
D_FF = 5632
HEAD_DIM = 128
ROPE_DIM = HEAD_DIM // 4
ROPE_HALF = ROPE_DIM // 2
ROPE_THETA = 500000.0
RMS_EPS = 1e-6
BAND = 128
DILATIONS = (1, 4, 16)
N_DIL_GROUPS = len(DILATIONS)
DIL_HEADS = 4
DIL_GROUP_WIDTH = DIL_HEADS * HEAD_DIM
DIL_WIDTH = N_DIL_GROUPS * DIL_GROUP_WIDTH
DIFF_HEADS = 4
DIFF_HEAD_WIDTH = 2 * HEAD_DIM
DIFF_WIDTH = DIFF_HEADS * DIFF_HEAD_WIDTH
DIL_BLOCK = BAND * DILATIONS[-1]
LOG2_E = math.log2(math.e)
NEG_BIG = -1e30

VMEM_LIMIT_BYTES = 62 * 1024 * 1024

FFN_TM = 1024
FFN_TF = 512
PROJ_TM = 1024
PROJ_GATE_TM = 512
PROJ_CHUNK = 256
PROJ_PART_HEADS = 4
PROJ_PART_WIDTH = PROJ_PART_HEADS * HEAD_DIM
MAX_SINGLE_OP_STRIDE = 4
MERGE_TM = 512
DIFF_TQ = 1024
DIFF_TK = 512
BF16_SUBLANES = 16
DIFF_ONES_ROWS = BF16_SUBLANES


def _params(*sem):
    return pltpu.CompilerParams(dimension_semantics=sem, vmem_limit_bytes=VMEM_LIMIT_BYTES)


def _rms_scale(x):
    return lax.rsqrt(jnp.mean(x * x, axis=-1, keepdims=True) + RMS_EPS)


def _dot(a, b):
    return jnp.dot(a, b, preferred_element_type=F32)


def _dot_nt(a, b):
    return lax.dot_general(a, b, (((1,), (1,)), ((), ())), preferred_element_type=F32)


def _ffn_kernel(*refs, emit_norm):
    if emit_norm:
        x_ref, gin_ref, wg_ref, wu_ref, wd_ref, gout_ref, o_ref, hn_ref, h_ref = refs
    else:
        x_ref, gin_ref, wg_ref, wu_ref, wd_ref, o_ref, h_ref = refs
    f = pl.program_id(1)
    last = pl.num_programs(1) - 1

    def step(h):
        g = _dot(h, wg_ref[...])
        u = _dot(h, wu_ref[...])
        a = (g * jax.nn.sigmoid(g)) * u * 0.5
        o_ref[...] += _dot(a.astype(BF16), wd_ref[...])

    def finish():
        if emit_norm:
            y = o_ref[...]
            hn_ref[...] = (y * _rms_scale(y) * gout_ref[...]).astype(BF16)

    @pl.when(f == 0)
    def _():
        x = x_ref[...]
        h = (x * _rms_scale(x) * gin_ref[...]).astype(BF16)
        h_ref[...] = h
        o_ref[...] = x
        step(h)

    @pl.when((f > 0) & (f < last))
    def _():
        step(h_ref[...])

    @pl.when(f == last)
    def _():
        step(h_ref[...])
        finish()


def _ffn(x, g_in, wg, wu, wd, g_out):
    T = x.shape[0]
    tm, tf = FFN_TM, FFN_TF
    emit_norm = g_out is not None
    tok = pl.BlockSpec((tm, D_MODEL), lambda i, f: (i, 0))
    vec = pl.BlockSpec((1, D_MODEL), lambda i, f: (0, 0))
    in_specs = [tok, vec,
                pl.BlockSpec((D_MODEL, tf), lambda i, f: (0, f)),
                pl.BlockSpec((D_MODEL, tf), lambda i, f: (0, f)),
                pl.BlockSpec((tf, D_MODEL), lambda i, f: (f, 0))]
    args = [x, g_in, wg, wu, wd]
    out_specs = [tok]
    out_shape = [jax.ShapeDtypeStruct((T, D_MODEL), F32)]
    if emit_norm:
        in_specs.append(vec)
        args.append(g_out)
        out_specs.append(tok)
        out_shape.append(jax.ShapeDtypeStruct((T, D_MODEL), BF16))
    return pl.pallas_call(
        functools.partial(_ffn_kernel, emit_norm=emit_norm),
        grid=(T // tm, D_FF // tf),
        in_specs=in_specs,
        out_specs=out_specs,
        out_shape=out_shape,
        scratch_shapes=[pltpu.VMEM((tm, D_MODEL), BF16)],
        compiler_params=_params("parallel", "arbitrary"),
        name="ffn",
    )(*args)


QUERY, KEY, PLAIN, GATE = "query", "key", "plain", "gate"


def _sigmoid(y):
    return 0.5 * jnp.tanh(0.5 * y) + 0.5


def _norm_rope(y, gain, cos, sin):
    y = y * _rms_scale(y) * gain
    lane = lax.broadcasted_iota(jnp.int32, y.shape, 1)
    partner = jnp.where(lane < ROPE_HALF, pltpu.roll(y, HEAD_DIM - ROPE_HALF, 1), pltpu.roll(y, ROPE_HALF, 1))
    return y * cos + partner * sin


def _proj_stages(h_ref, w_refs, gain_ref, cos_ref, sin_ref, o_ref, y_ref, slab, *, kinds, chunk, dilation):
    def compute(m, slot):
        h = h_ref[pl.ds(m * chunk, chunk), :]
        for p, w_ref in enumerate(w_refs):
            y = _dot(h, w_ref[...])
            for j in range(PROJ_PART_HEADS):
                y_ref[slot, p * PROJ_PART_HEADS + j] = y[:, j * HEAD_DIM:(j + 1) * HEAD_DIM]

    def finish(m, slot):
        rows = pl.ds(m * chunk, chunk)
        for hh, kind in enumerate(kinds):
            y = y_ref[slot, hh]
            if kind in (QUERY, KEY):
                gain = gain_ref[0:1, :] if kind == QUERY else gain_ref[1:2, :]
                y = _norm_rope(y, gain, cos_ref[rows, :], sin_ref[rows, :])
            elif kind == GATE:
                y = _sigmoid(y)
            if dilation is None:
                o_ref[rows, hh * HEAD_DIM:(hh + 1) * HEAD_DIM] = y.astype(BF16)
            else:
                sec, lanes = divmod(hh * HEAD_DIM, DIL_GROUP_WIDTH)
                lanes = slice(lanes, lanes + HEAD_DIM)
                n = chunk // dilation
                dst = pl.ds(m * n, n)
                if dilation == 1:
                    o_ref[sec, 0, dst, lanes] = y.astype(BF16)
                elif dilation <= MAX_SINGLE_OP_STRIDE:
                    slab_ref = slab[0]
                    slab_ref[hh] = y
                    for c in range(dilation):
                        o_ref[sec, c, dst, lanes] = slab_ref[hh, pl.ds(c, n, stride=dilation), :].astype(BF16)
                else:
                    slab_ref, slab2_ref = slab
                    step = MAX_SINGLE_OP_STRIDE
                    assert dilation == step * step
                    n1 = chunk // step
                    slab_ref[hh] = y
                    for c_lo in range(step):
                        slab2_ref[hh, c_lo * n1:(c_lo + 1) * n1, :] = slab_ref[hh, pl.ds(c_lo, n1, stride=step), :]
                    for c_lo in range(step):
                        for c_hi in range(step):
                            o_ref[sec, c_lo + step * c_hi, dst, lanes] = slab2_ref[
                                hh, pl.ds(c_lo * n1 + c_hi, n, stride=step), :].astype(BF16)

    return compute, finish


def _proj_kernel(h_ref, *refs, kinds, chunk, dilation):
    n_parts = len(kinds) // PROJ_PART_HEADS
    gain_ref, cos_ref, sin_ref, o_ref, y_ref = refs[n_parts:n_parts + 5]
    compute, finish = _proj_stages(h_ref, refs[:n_parts], gain_ref, cos_ref, sin_ref, o_ref, y_ref,
                                   refs[n_parts + 5:], kinds=kinds, chunk=chunk, dilation=dilation)
    n_chunks = h_ref.shape[0] // chunk
    compute(0, 0)
    for m in range(n_chunks - 1):
        finish(m, m % 2)
        compute(m + 1, (m + 1) % 2)
    finish(n_chunks - 1, (n_chunks - 1) % 2)


def _weight_specs(parts):
    return [pl.BlockSpec((D_MODEL, PROJ_PART_WIDTH), functools.partial(lambda blk, i: (0, blk), blk),
                         pipeline_mode=pl.Buffered(1)) for blk in parts]


def _proj(h, w, gains, cos, sin, *, parts, kinds, tm, name, dil=None):
    T = h.shape[0]
    width = len(kinds) * HEAD_DIM
    assert width == len(parts) * PROJ_PART_WIDTH and w.shape[1] % PROJ_PART_WIDTH == 0
    chunk = PROJ_CHUNK
    scratch = [pltpu.VMEM((2, len(kinds), chunk, HEAD_DIM), F32)]
    if dil is None:
        dilation = None
        out_spec = pl.BlockSpec((tm, width), lambda i: (i, 0))
        out_shape = jax.ShapeDtypeStruct((T, width), BF16)
    else:
        dilation, batch, seq = dil
        n_i = seq // tm
        out_spec = pl.BlockSpec((3, None, dilation, tm // dilation, DIL_GROUP_WIDTH),
                                lambda i: (0, i // n_i, 0, i % n_i, 0))
        out_shape = jax.ShapeDtypeStruct((3, batch, dilation, seq // dilation, DIL_GROUP_WIDTH), BF16)
        if dilation > 1:
            scratch.append(pltpu.VMEM((len(kinds), chunk, HEAD_DIM), F32))
        if dilation > MAX_SINGLE_OP_STRIDE:
            scratch.append(pltpu.VMEM((len(kinds), chunk, HEAD_DIM), F32))
    tok = lambda cols: pl.BlockSpec((tm, cols), lambda i: (i, 0))
    return pl.pallas_call(
        functools.partial(_proj_kernel, kinds=kinds, chunk=chunk, dilation=dilation),
        grid=(T // tm,),
        in_specs=[tok(D_MODEL), *_weight_specs(parts), pl.BlockSpec((2, HEAD_DIM), lambda i: (0, 0)),
                  tok(HEAD_DIM), tok(HEAD_DIM)],
        out_specs=out_spec,
        out_shape=out_shape,
        scratch_shapes=scratch,
        compiler_params=_params("parallel"),
        name=name,
    )(h, *([w] * len(parts)), gains, cos, sin)


def _dil_attn_stages(ins, o_ref, scr, i):
    steps = DIL_BLOCK // BAND

    row = lax.broadcasted_iota(jnp.int32, (BAND, 2 * BAND), 0)
    col = lax.broadcasted_iota(jnp.int32, (BAND, 2 * BAND), 1)
    band_bias = jnp.where((col >= row) & (col <= row + BAND), 0.0, NEG_BIG)
    start_bias = jnp.where(i == 0, jnp.where(col >= BAND, band_bias, NEG_BIG), band_bias)

    groups = []
    for g, r in enumerate(DILATIONS):
        q_ref, kc_ref, kp_ref, vc_ref, vp_ref = ins[5 * g:5 * g + 5]
        kx_ref, vx_ref, on_ref, ln_ref = scr[4 * g:4 * g + 4]
        kx_ref[:, :BAND, :] = kp_ref[...]
        kx_ref[:, BAND:, :] = kc_ref[...]
        vx_ref[:, :BAND, :HEAD_DIM] = vp_ref[...]
        vx_ref[:, BAND:, :HEAD_DIM] = vc_ref[...]
        vx_ref[:, :, HEAD_DIM:] = jnp.ones((r, vx_ref.shape[1], HEAD_DIM), BF16)
        groups.append((r, q_ref, kx_ref, vx_ref, on_ref, ln_ref))

    def tile(t):
        for r, q_ref, kx_ref, vx_ref, on_ref, ln_ref in groups:
            tiles = DIL_BLOCK // r // BAND
            c, mb = (0, t) if tiles == steps else (t, 0) if tiles == 1 else (t % r, t // r)
            m0 = mb * BAND
            q = q_ref[c, pl.ds(m0, BAND), :]
            k = kx_ref[c, pl.ds(m0, 2 * BAND), :]
            v = vx_ref[c, pl.ds(m0, 2 * BAND), :]
            s = _dot_nt(q, k) + (start_bias if mb == 0 else band_bias)
            m = jnp.max(s, axis=-1, keepdims=True)
            p = jnp.exp(s - m)
            od = _dot(p.astype(BF16), v)
            den = od[:, HEAD_DIM:]
            dst = pl.ds(mb * (BAND * r) + c, BAND, stride=r) if r > 1 else pl.ds(m0, BAND)
            on_ref[dst, :] = od[:, :HEAD_DIM] / den
            ln_ref[dst, :] = m + jnp.log(den)

    def mix():
        lses = [grp[5][...] for grp in groups]
        top = jnp.maximum(jnp.maximum(lses[0], lses[1]), lses[2])
        ws = [jnp.exp(l - top) for l in lses]
        num = ws[0] * groups[0][4][...] + ws[1] * groups[1][4][...] + ws[2] * groups[2][4][...]
        o_ref[...] = (num / (ws[0] + ws[1] + ws[2])).astype(BF16)

    return tile, mix


def _dil_attn_specs(unit):
    in_specs, scratch = [], []
    for r in DILATIONS:
        nq = DIL_BLOCK // r
        per = nq // BAND

        def cur(s, r=r, nq=nq):
            def index(*ids):
                b, h, i = unit(*ids)
                return (s, b, 0, i, h)
            return pl.BlockSpec((None, None, r, nq, HEAD_DIM), index)

        def prev(s, r=r, per=per):
            def index(*ids):
                b, h, i = unit(*ids)
                return (s, b, 0, jnp.maximum(i * per - 1, 0), h)
            return pl.BlockSpec((None, None, r, BAND, HEAD_DIM), index)

        in_specs += [cur(0), cur(1), prev(1), cur(2), prev(2)]
        scratch += [pltpu.VMEM((r, BAND + nq, HEAD_DIM), BF16), pltpu.VMEM((r, BAND + nq, 2 * HEAD_DIM), BF16),
                    pltpu.VMEM((DIL_BLOCK, HEAD_DIM), F32), pltpu.VMEM((DIL_BLOCK, HEAD_DIM), F32)]

    def out_index(*ids):
        b, h, i = unit(*ids)
        return (b, i, h)

    return in_specs, pl.BlockSpec((None, DIL_BLOCK, HEAD_DIM), out_index), scratch


def _gate_dil_kernel(h_ref, *refs, kinds, chunk, n_i):
    n_parts = len(kinds) // PROJ_PART_HEADS
    n_dil = 5 * N_DIL_GROUPS
    w_refs, dil_ins = refs[:n_parts], refs[n_parts:n_parts + n_dil]
    vg_ref, od_ref, y_ref = refs[n_parts + n_dil:n_parts + n_dil + 3]
    dil_scr = refs[n_parts + n_dil + 3:]
    compute, finish = _proj_stages(h_ref, w_refs, None, None, None, vg_ref, y_ref, (), kinds=kinds, chunk=chunk,
                                   dilation=None)
    tile, mix = _dil_attn_stages(dil_ins, od_ref, dil_scr, pl.program_id(0) % n_i)
    n_chunks = h_ref.shape[0] // chunk
    per_chunk = (DIL_BLOCK // BAND) // n_chunks
    compute(0, 0)
    for m in range(n_chunks):
        for t in range(m * per_chunk, (m + 1) * per_chunk):
            tile(t)
        finish(m, m % 2)
        if m + 1 < n_chunks:
            compute(m + 1, (m + 1) % 2)
    mix()


def _gate_dil(h, w, qkvs, *, parts, kinds, batch, seq):
    T = h.shape[0]
    tm, chunk = PROJ_GATE_TM, PROJ_CHUNK
    n_i = seq // DIL_BLOCK
    steps = T // tm
    assert steps == batch * DIL_HEADS * n_i
    width = len(kinds) * HEAD_DIM
    unit = lambda s: (s // (DIL_HEADS * n_i), (s // n_i) % DIL_HEADS, s % n_i)
    dil_specs, od_spec, dil_scratch = _dil_attn_specs(unit)
    dil_args = [qkvs[g] for g in range(N_DIL_GROUPS) for _ in range(5)]
    return pl.pallas_call(
        functools.partial(_gate_dil_kernel, kinds=kinds, chunk=chunk, n_i=n_i),
        grid=(steps,),
        in_specs=[pl.BlockSpec((tm, D_MODEL), lambda s: (s, 0)), *_weight_specs(parts), *dil_specs],
        out_specs=[pl.BlockSpec((tm, width), lambda s: (s, 0)), od_spec],
        out_shape=[jax.ShapeDtypeStruct((T, width), BF16),
                   jax.ShapeDtypeStruct((batch, seq, DIL_GROUP_WIDTH), BF16)],
        scratch_shapes=[pltpu.VMEM((2, len(kinds), chunk, HEAD_DIM), F32), *dil_scratch],
        compiler_params=_params("arbitrary"),
        name="gate_dil",
    )(h, *([w] * len(parts)), *dil_args)


def _diff_attn_kernel(lq1_ref, lk1_ref, lq2_ref, lk2_ref, q_ref, k_ref, v_ref, sub_ref, *refs, lam_init, n_casts):
    cast_in, (o_ref, *cast_out) = refs[:n_casts], refs[n_casts:2 * n_casts + 1]
    vt_ref, acc_ref, s0_ref, s1_ref = refs[2 * n_casts + 1:]
    for src_ref, dst_ref in zip(cast_in, cast_out):
        dst_ref[...] = src_ref[...].astype(BF16)

    qi = pl.program_id(2)
    tq = q_ref.shape[0]
    tk = vt_ref.shape[2]
    dv = DIFF_HEAD_WIDTH
    per_q = tq // tk

    @pl.when(qi == 0)
    def _():
        for ch in range(vt_ref.shape[0]):
            vt_ref[ch, :dv, :] = v_ref[ch * tk:(ch + 1) * tk, :].astype(F32).T.astype(BF16)
            vt_ref[ch, dv:, :] = jnp.ones((vt_ref.shape[1] - dv, tk), BF16)

    q = q_ref[...]
    acc_ref[...] = jnp.zeros(acc_ref.shape, F32)

    def scores(kb, s_ref):
        k = k_ref[pl.ds(pl.multiple_of(kb * tk, tk), tk), :]
        for c in range(2):
            lanes = slice(c * HEAD_DIM, (c + 1) * HEAD_DIM)
            s_ref[c] = _dot_nt(k[:, lanes], q[:, lanes])

    def update(kb, s_ref, maxes, qcols=slice(None), mask=None):
        vt = vt_ref[kb]
        out = []
        for c in range(2):
            s = s_ref[c, :, qcols]
            if mask is not None:
                s = jnp.where(mask, s, NEG_BIG)
            m_new = jnp.maximum(maxes[c], jnp.max(s, axis=0, keepdims=True))
            alpha = jnp.exp2(maxes[c] - m_new)
            p = jnp.exp2(s - m_new)
            out.append(m_new)
            acc_ref[c, :, qcols] = alpha * acc_ref[c, :, qcols] + _dot(vt, p.astype(BF16))
        return tuple(out)

    def pair(j, maxes):
        kb = per_q * j
        scores(kb + 1, s1_ref)
        maxes = update(kb, s0_ref, maxes)
        scores(kb + 2, s0_ref)
        return update(kb + 1, s1_ref, maxes)

    row0 = jnp.full((1, tq), NEG_BIG, F32)
    scores(0, s0_ref)
    maxes = lax.fori_loop(0, qi, pair, (row0, row0))

    kd = per_q * qi
    lo, hi = slice(0, tk), slice(tk, tq)
    key = lax.broadcasted_iota(jnp.int32, (tk, tk), 0)
    qry = lax.broadcasted_iota(jnp.int32, (tk, tk), 1)
    tri = key <= qry
    k_last = k_ref[pl.ds(pl.multiple_of((kd + 1) * tk, tk), tk), :]
    for c in range(2):
        lanes = slice(c * HEAD_DIM, (c + 1) * HEAD_DIM)
        s1_ref[c, :, hi] = _dot_nt(k_last[:, lanes], q[tk:, lanes])
    update(kd, s0_ref, tuple(m[:, lo] for m in maxes), lo, tri)
    m_hi = update(kd, s0_ref, tuple(m[:, hi] for m in maxes), hi)
    update(kd + 1, s1_ref, m_hi, hi, tri)

    lam = (jnp.exp(jnp.sum(lq1_ref[...] * lk1_ref[...], axis=-1, keepdims=True))
           - jnp.exp(jnp.sum(lq2_ref[...] * lk2_ref[...], axis=-1, keepdims=True)) + lam_init)
    o = (acc_ref[0, :dv] / acc_ref[0, dv:dv + 1] - lam * (acc_ref[1, :dv] / acc_ref[1, dv:dv + 1])).T
    o_ref[...] = (o * _rms_scale(o) * sub_ref[...] * (1.0 - lam_init)).astype(BF16)


def _diff_attn(qk, vg, lq1, lk1, lq2, lk2, subln, casts, *, batch, seq, lam_init):
    tq, tk = DIFF_TQ, DIFF_TK
    assert tq == 2 * tk
    q_blk, k_blk, v_blk = 0, DIFF_HEADS, 0
    n_q = seq // tq
    steps = batch * DIFF_HEADS * n_q
    vec = pl.BlockSpec((1, HEAD_DIM), lambda b, h, i: (0, 0))

    cast_specs = []
    for w in casts:
        share = steps
        while w.shape[0] % (share * BF16_SUBLANES):
            share //= 2
        cast_specs.append(pl.BlockSpec(
            (w.shape[0] // share, w.shape[1]),
            functools.partial(lambda rep, b, h, i: (((b * DIFF_HEADS + h) * n_q + i) // rep, 0), steps // share)))
    o, *cast = pl.pallas_call(
        functools.partial(_diff_attn_kernel, lam_init=lam_init, n_casts=len(casts)),
        grid=(batch, DIFF_HEADS, n_q),
        in_specs=[
            vec, vec, vec, vec,
            pl.BlockSpec((None, tq, DIFF_HEAD_WIDTH), lambda b, h, i: (b, i, q_blk + h)),
            pl.BlockSpec((None, seq, DIFF_HEAD_WIDTH), lambda b, h, i: (b, 0, k_blk + h)),
            pl.BlockSpec((None, seq, DIFF_HEAD_WIDTH), lambda b, h, i: (b, 0, v_blk + h)),
            pl.BlockSpec((1, DIFF_HEAD_WIDTH), lambda b, h, i: (0, 0)),
            *cast_specs,
        ],
        out_specs=[pl.BlockSpec((None, tq, DIFF_HEAD_WIDTH), lambda b, h, i: (b, i, h)), *cast_specs],
        out_shape=[jax.ShapeDtypeStruct((batch, seq, DIFF_WIDTH), BF16),
                   *(jax.ShapeDtypeStruct(w.shape, BF16) for w in casts)],
        scratch_shapes=[pltpu.VMEM((seq // tk, DIFF_HEAD_WIDTH + DIFF_ONES_ROWS, tk), BF16),
                        pltpu.VMEM((2, DIFF_HEAD_WIDTH + DIFF_ONES_ROWS, tq), F32),
                        pltpu.VMEM((2, tk, tq), F32),
                        pltpu.VMEM((2, tk, tq), F32)],
        compiler_params=_params("arbitrary", "arbitrary", "arbitrary"),
        name="diff_attn",
    )(lq1, lk1, lq2, lk2, qk, qk, vg, subln, *casts)
    return o, cast


def _merge_kernel(x_ref, od_ref, of_ref, gd0_ref, gd1_ref, gf0_ref, gf1_ref, wa_ref, wb_ref, wo_ref, o_ref):
    gd = jnp.concatenate([gd0_ref[...], gd1_ref[...]], axis=1).astype(F32)
    gf = jnp.concatenate([gf0_ref[...], gf1_ref[...]], axis=1).astype(F32)
    y = gd * _dot(od_ref[...], wa_ref[...]) + gf * _dot(of_ref[...], wb_ref[...])
    o_ref[...] = x_ref[...] + _dot(y.astype(BF16), wo_ref[...])


def _merge(x, o_dil, o_diff, vg, wa, wb, wo):
    T = x.shape[0]
    tm = MERGE_TM
    half = D_MODEL // 2

    def tok(width, blk=0):
        return pl.BlockSpec((tm, width), lambda i: (i, blk))

    def resident(shape):
        return pl.BlockSpec(shape, lambda i: (0, 0), pipeline_mode=pl.Buffered(1))

    gate0 = DIFF_WIDTH // half
    return pl.pallas_call(
        _merge_kernel,
        grid=(T // tm,),
        in_specs=[tok(D_MODEL), tok(DIL_GROUP_WIDTH), tok(DIFF_WIDTH),
                  tok(half, gate0), tok(half, gate0 + 1), tok(half, gate0 + 2), tok(half, gate0 + 3),
                  resident(wa.shape), resident(wb.shape), resident(wo.shape)],
        out_specs=tok(D_MODEL),
        out_shape=jax.ShapeDtypeStruct((T, D_MODEL), F32),
        compiler_params=_params("parallel"),
        name="merge",
    )(x, o_dil, o_diff, vg, vg, vg, vg, wa, wb, wo)


def _rope_tables(positions):
    lane = jnp.arange(HEAD_DIM)
    freq = ROPE_THETA ** (-(2 * (lane % ROPE_HALF)).astype(F32) / ROPE_DIM)
    inv = jnp.where(lane < ROPE_DIM, freq, 0.0)
    sign = jnp.where(lane < ROPE_HALF, -1.0, jnp.where(lane < ROPE_DIM, 1.0, 0.0)).astype(F32)
    ang = positions.astype(F32).reshape(-1, 1) * inv
    return jnp.cos(ang), jnp.sin(ang) * sign


def _layer(x, tables, layer, ffn1_norm, ffn1_w_gate, ffn1_w_up, ffn1_w_down, mix_norm, w_in,
           dil_q_norm, dil_k_norm, diff_q_norm, diff_k_norm, diff_lq1, diff_lk1, diff_lq2, diff_lk2,
           diff_subln, w_dil_branch, w_diff_branch, w_out, ffn2_norm, ffn2_w_gate, ffn2_w_up, ffn2_w_down,
           *, batch, seq):
    cos, sin = tables
    row = lambda v: v.reshape(1, -1).astype(F32)
    bf = lambda w: w.astype(BF16)
    qk_scale = HEAD_DIM ** -0.5
    lam_init = 0.8 - 0.6 * math.exp(-0.3 * layer)

    x1, h = _ffn(x, row(ffn1_norm), bf(ffn1_w_gate), bf(ffn1_w_up), bf(ffn1_w_down), row(mix_norm))

    w_in_bf = bf(w_in)
    blocks = lambda start, width: tuple(range(start // PROJ_PART_WIDTH, (start + width) // PROJ_PART_WIDTH))
    diff0 = 3 * DIL_WIDTH
    fv0 = diff0 + 2 * DIFF_WIDTH
    gains = lambda gq, gk: jnp.stack([gq * qk_scale, gk]).astype(F32)

    dil_kinds = (QUERY,) * DIL_HEADS + (KEY,) * DIL_HEADS + (PLAIN,) * DIL_HEADS
    qkvs = [_proj(h, w_in_bf, gains(dil_q_norm, dil_k_norm), cos, sin,
                  parts=tuple(sec * N_DIL_GROUPS + g for sec in range(3)),
                  kinds=dil_kinds, tm=PROJ_TM, name=f"dil_proj{g}", dil=(r, batch, seq))
            for g, r in enumerate(DILATIONS)]

    n_diff = DIFF_WIDTH // HEAD_DIM
    qk = _proj(h, w_in_bf, gains(diff_q_norm * LOG2_E, diff_k_norm), cos, sin, parts=blocks(diff0, 2 * DIFF_WIDTH),
               kinds=(QUERY,) * n_diff + (KEY,) * n_diff, tm=PROJ_TM, name="diff_qk_proj")
    n_gate = 2 * D_MODEL // HEAD_DIM
    vg, o_dil = _gate_dil(h, w_in_bf, qkvs, parts=blocks(fv0, DIFF_WIDTH + 2 * D_MODEL),
                          kinds=(PLAIN,) * n_diff + (GATE,) * n_gate, batch=batch, seq=seq)

    later = [w_dil_branch, w_diff_branch, w_out, ffn2_w_gate, ffn2_w_up, ffn2_w_down]
    o_diff, (wa, wb, wo, wg2, wu2, wd2) = _diff_attn(
        qk.reshape(batch, seq, -1), vg.reshape(batch, seq, -1), row(diff_lq1), row(diff_lk1), row(diff_lq2),
        row(diff_lk2), row(diff_subln), later, batch=batch, seq=seq, lam_init=lam_init)

    x2 = _merge(x1, o_dil.reshape(batch * seq, -1), o_diff.reshape(batch * seq, -1), vg, wa, wb, wo)
    (out,) = _ffn(x2, row(ffn2_norm), wg2, wu2, wd2, None)
    return out


def kernel(x, positions, ffn1_norm, ffn1_w_gate, ffn1_w_up, ffn1_w_down, mix_norm, w_in, dil_q_norm, dil_k_norm, diff_q_norm, diff_k_norm, diff_lq1, diff_lk1, diff_lq2, diff_lk2, diff_subln, w_dil_branch, w_diff_branch, w_out, ffn2_norm, ffn2_w_gate, ffn2_w_up, ffn2_w_down):
    batch, seq, d_model = x.shape
    assert d_model == D_MODEL and seq % DIL_BLOCK == 0
    weights = (ffn1_norm, ffn1_w_gate, ffn1_w_up, ffn1_w_down, mix_norm, w_in, dil_q_norm, dil_k_norm,
               diff_q_norm, diff_k_norm, diff_lq1, diff_lk1, diff_lq2, diff_lk2, diff_subln, w_dil_branch,
               w_diff_branch, w_out, ffn2_norm, ffn2_w_gate, ffn2_w_up, ffn2_w_down)
    tables = _rope_tables(positions)
    y = x.reshape(batch * seq, d_model)
    for layer in range(ffn1_norm.shape[0]):
        y = _layer(y, tables, layer, *(w[layer] for w in weights), batch=batch, seq=seq)
    return y.reshape(batch, seq, d_model)
```

```python
import functools
import math

import jax
import jax.numpy as jnp
from jax import lax
from jax.experimental import pallas as pl
from jax.experimental.pallas import tpu as pltpu

F32 = jnp.float32
BF16 = jnp.bfloat16

D_MODEL = 2048
D_FF = 5632
HEAD_DIM = 128
ROPE_DIM = HEAD_DIM // 4
ROPE_HALF = ROPE_DIM // 2
ROPE_THETA = 500000.0
RMS_EPS = 1e-6
BAND = 128
DILATIONS = (1, 4, 16)
N_DIL_GROUPS = len(DILATIONS)
DIL_HEADS = 4
DIL_GROUP_WIDTH = DIL_HEADS * HEAD_DIM
DIL_WIDTH = N_DIL_GROUPS * DIL_GROUP_WIDTH
DIFF_HEADS = 4
DIFF_HEAD_WIDTH = 2 * HEAD_DIM
DIFF_WIDTH = DIFF_HEADS * DIFF_HEAD_WIDTH
DIL_BLOCK = BAND * DILATIONS[-1]
LOG2_E = math.log2(math.e)
NEG_BIG = -1e30

VMEM_LIMIT_BYTES = 62 * 1024 * 1024

FFN_TM = 1024
FFN_TF = 512
PROJ_TM = 1024
PROJ_GATE_TM = 512
PROJ_CHUNK = 256
PROJ_PART_HEADS = 4
PROJ_PART_WIDTH = PROJ_PART_HEADS * HEAD_DIM
MAX_SINGLE_OP_STRIDE = 4
MERGE_TM = 512
DIFF_TQ = 1024
DIFF_TK = 512
BF16_SUBLANES = 16
DIFF_ONES_ROWS = BF16_SUBLANES


def _params(*sem):
    return pltpu.CompilerParams(dimension_semantics=sem, vmem_limit_bytes=VMEM_LIMIT_BYTES)


def _rms_scale(x):
    return lax.rsqrt(jnp.mean(x * x, axis=-1, keepdims=True) + RMS_EPS)


def _dot(a, b):
    return jnp.dot(a, b, preferred_element_type=F32)


def _dot_nt(a, b):
    return lax.dot_general(a, b, (((1,), (1,)), ((), ())), preferred_element_type=F32)


def _ffn_kernel(*refs, emit_norm):
    if emit_norm:
        x_ref, gin_ref, wg_ref, wu_ref, wd_ref, gout_ref, o_ref, hn_ref, h_ref = refs
    else:
        x_ref, gin_ref, wg_ref, wu_ref, wd_ref, o_ref, h_ref = refs
    f = pl.program_id(1)
    last = pl.num_programs(1) - 1

    def step(h):
        g = _dot(h, wg_ref[...])
        u = _dot(h, wu_ref[...])
        a = (g * jax.nn.sigmoid(g)) * u * 0.5
        o_ref[...] += _dot(a.astype(BF16), wd_ref[...])

    def finish():
        if emit_norm:
            y = o_ref[...]
            hn_ref[...] = (y * _rms_scale(y) * gout_ref[...]).astype(BF16)

    @pl.when(f == 0)
    def _():
        x = x_ref[...]
        h = (x * _rms_scale(x) * gin_ref[...]).astype(BF16)
        h_ref[...] = h
        o_ref[...] = x
        step(h)

    @pl.when((f > 0) & (f < last))
    def _():
        step(h_ref[...])

    @pl.when(f == last)
    def _():
        step(h_ref[...])
        finish()


def _ffn(x, g_in, wg, wu, wd, g_out):
    T = x.shape[0]
    tm, tf = FFN_TM, FFN_TF
    emit_norm = g_out is not None
    tok = pl.BlockSpec((tm, D_MODEL), lambda i, f: (i, 0))
    vec = pl.BlockSpec((1, D_MODEL), lambda i, f: (0, 0))
    in_specs = [tok, vec,
                pl.BlockSpec((D_MODEL, tf), lambda i, f: (0, f)),
                pl.BlockSpec((D_MODEL, tf), lambda i, f: (0, f)),
                pl.BlockSpec((tf, D_MODEL), lambda i, f: (f, 0))]
    args = [x, g_in, wg, wu, wd]
    out_specs = [tok]
    out_shape = [jax.ShapeDtypeStruct((T, D_MODEL), F32)]
    if emit_norm:
        in_specs.append(vec)
        args.append(g_out)
        out_specs.append(tok)
        out_shape.append(jax.ShapeDtypeStruct((T, D_MODEL), BF16))
    return pl.pallas_call(
        functools.partial(_ffn_kernel, emit_norm=emit_norm),
        grid=(T // tm, D_FF // tf),
        in_specs=in_specs,
        out_specs=out_specs,
        out_shape=out_shape,
        scratch_shapes=[pltpu.VMEM((tm, D_MODEL), BF16)],
        compiler_params=_params("parallel", "arbitrary"),
        name="ffn",
    )(*args)


QUERY, KEY, PLAIN, GATE = "query", "key", "plain", "gate"


def _sigmoid(y):
    return 0.5 * jnp.tanh(0.5 * y) + 0.5


def _norm_rope(y, gain, cos, sin):
    y = y * _rms_scale(y) * gain
    lane = lax.broadcasted_iota(jnp.int32, y.shape, 1)
    partner = jnp.where(lane < ROPE_HALF, pltpu.roll(y, HEAD_DIM - ROPE_HALF, 1), pltpu.roll(y, ROPE_HALF, 1))
    return y * cos + partner * sin


def _proj_stages(h_ref, w_refs, gain_ref, cos_ref, sin_ref, o_ref, y_ref, slab, *, kinds, chunk, dilation):
    def compute(m, slot):
        h = h_ref[pl.ds(m * chunk, chunk), :]
        for p, w_ref in enumerate(w_refs):
            y = _dot(h, w_ref[...])
            for j in range(PROJ_PART_HEADS):
                y_ref[slot, p * PROJ_PART_HEADS + j] = y[:, j * HEAD_DIM:(j + 1) * HEAD_DIM]

    def finish(m, slot):
        rows = pl.ds(m * chunk, chunk)
        for hh, kind in enumerate(kinds):
            y = y_ref[slot, hh]
            if kind in (QUERY, KEY):
                gain = gain_ref[0:1, :] if kind == QUERY else gain_ref[1:2, :]
                y = _norm_rope(y, gain, cos_ref[rows, :], sin_ref[rows, :])
            elif kind == GATE:
                y = _sigmoid(y)
            if dilation is None:
                o_ref[rows, hh * HEAD_DIM:(hh + 1) * HEAD_DIM] = y.astype(BF16)
            else:
                sec, head = divmod(hh, DIL_HEADS)
                n = chunk // dilation
                dst = pl.ds(m * n, n)
                if dilation == 1:
                    o_ref[sec, head, 0, dst, :] = y.astype(BF16)
                elif dilation <= MAX_SINGLE_OP_STRIDE:
                    slab_ref = slab[0]
                    slab_ref[hh] = y
                    for c in range(dilation):
                        o_ref[sec, head, c, dst, :] = slab_ref[hh, pl.ds(c, n, stride=dilation), :].astype(BF16)
                else:
                    slab_ref, slab2_ref = slab
                    step = MAX_SINGLE_OP_STRIDE
                    assert dilation == step * step
                    n1 = chunk // step
                    slab_ref[hh] = y
                    for c_lo in range(step):
                        slab2_ref[hh, c_lo * n1:(c_lo + 1) * n1, :] = slab_ref[hh, pl.ds(c_lo, n1, stride=step), :]
                    for c_lo in range(step):
                        for c_hi in range(step):
                            o_ref[sec, head, c_lo + step * c_hi, dst, :] = slab2_ref[
                                hh, pl.ds(c_lo * n1 + c_hi, n, stride=step), :].astype(BF16)

    return compute, finish


def _proj_kernel(h_ref, *refs, kinds, chunk, dilation):
    n_parts = len(kinds) // PROJ_PART_HEADS
    gain_ref, cos_ref, sin_ref, o_ref, y_ref = refs[n_parts:n_parts + 5]
    compute, finish = _proj_stages(h_ref, refs[:n_parts], gain_ref, cos_ref, sin_ref, o_ref, y_ref,
                                   refs[n_parts + 5:], kinds=kinds, chunk=chunk, dilation=dilation)
    n_chunks = h_ref.shape[0] // chunk
    compute(0, 0)
    for m in range(n_chunks - 1):
        finish(m, m % 2)
        compute(m + 1, (m + 1) % 2)
    finish(n_chunks - 1, (n_chunks - 1) % 2)


def _weight_specs(parts):
    return [pl.BlockSpec((D_MODEL, PROJ_PART_WIDTH), functools.partial(lambda blk, i: (0, blk), blk),
                         pipeline_mode=pl.Buffered(1)) for blk in parts]


def _proj(h, w, gains, cos, sin, *, parts, kinds, tm, name, dil=None):
    T = h.shape[0]
    width = len(kinds) * HEAD_DIM
    assert width == len(parts) * PROJ_PART_WIDTH and w.shape[1] % PROJ_PART_WIDTH == 0
    chunk = PROJ_CHUNK
    scratch = [pltpu.VMEM((2, len(kinds), chunk, HEAD_DIM), F32)]
    if dil is None:
        dilation = None
        out_spec = pl.BlockSpec((tm, width), lambda i: (i, 0))
        out_shape = jax.ShapeDtypeStruct((T, width), BF16)
    else:
        dilation, batch, seq = dil
        n_i = seq // tm
        out_spec = pl.BlockSpec((3, None, DIL_HEADS, dilation, tm // dilation, HEAD_DIM),
                                lambda i: (0, i // n_i, 0, 0, i % n_i, 0))
        out_shape = jax.ShapeDtypeStruct((3, batch, DIL_HEADS, dilation, seq // dilation, HEAD_DIM), BF16)
        if dilation > 1:
            scratch.append(pltpu.VMEM((len(kinds), chunk, HEAD_DIM), F32))
        if dilation > MAX_SINGLE_OP_STRIDE:
            scratch.append(pltpu.VMEM((len(kinds), chunk, HEAD_DIM), F32))
    tok = lambda cols: pl.BlockSpec((tm, cols), lambda i: (i, 0))
    return pl.pallas_call(
        functools.partial(_proj_kernel, kinds=kinds, chunk=chunk, dilation=dilation),
        grid=(T // tm,),
        in_specs=[tok(D_MODEL), *_weight_specs(parts), pl.BlockSpec((2, HEAD_DIM), lambda i: (0, 0)),
                  tok(HEAD_DIM), tok(HEAD_DIM)],
        out_specs=out_spec,
        out_shape=out_shape,
        scratch_shapes=scratch,
        compiler_params=_params("parallel"),
        name=name,
    )(h, *([w] * len(parts)), gains, cos, sin)


def _dil_attn_stages(ins, o_ref, scr, i):
    steps = DIL_BLOCK // BAND

    row = lax.broadcasted_iota(jnp.int32, (BAND, 2 * BAND), 0)
    col = lax.broadcasted_iota(jnp.int32, (BAND, 2 * BAND), 1)
    band_bias = jnp.where((col >= row) & (col <= row + BAND), 0.0, NEG_BIG)
    start_bias = jnp.where(i == 0, jnp.where(col >= BAND, band_bias, NEG_BIG), band_bias)

    groups = []
    for g, r in enumerate(DILATIONS):
        q_ref, kc_ref, kp_ref, vc_ref, vp_ref = ins[5 * g:5 * g + 5]
        kx_ref, vx_ref, on_ref, ln_ref = scr[4 * g:4 * g + 4]
        kx_ref[:, :BAND, :] = kp_ref[...]
        kx_ref[:, BAND:, :] = kc_ref[...]
        vx_ref[:, :BAND, :HEAD_DIM] = vp_ref[...]
        vx_ref[:, BAND:, :HEAD_DIM] = vc_ref[...]
        vx_ref[:, :, HEAD_DIM:] = jnp.ones((r, vx_ref.shape[1], HEAD_DIM), BF16)
        groups.append((r, q_ref, kx_ref, vx_ref, on_ref, ln_ref))

    def tile(t):
        for r, q_ref, kx_ref, vx_ref, on_ref, ln_ref in groups:
            tiles = DIL_BLOCK // r // BAND
            c, mb = (0, t) if tiles == steps else (t, 0) if tiles == 1 else (t % r, t // r)
            m0 = mb * BAND
            q = q_ref[c, pl.ds(m0, BAND), :]
            k = kx_ref[c, pl.ds(m0, 2 * BAND), :]
            v = vx_ref[c, pl.ds(m0, 2 * BAND), :]
            s = _dot_nt(q, k) + (start_bias if mb == 0 else band_bias)
            m = jnp.max(s, axis=-1, keepdims=True)
            p = jnp.exp(s - m)
            od = _dot(p.astype(BF16), v)
            den = od[:, HEAD_DIM:]
            dst = pl.ds(mb * (BAND * r) + c, BAND, stride=r) if r > 1 else pl.ds(m0, BAND)
            on_ref[dst, :] = od[:, :HEAD_DIM] / den
            ln_ref[dst, :] = m + jnp.log(den)

    def mix():
        lses = [grp[5][...] for grp in groups]
        top = jnp.maximum(jnp.maximum(lses[0], lses[1]), lses[2])
        ws = [jnp.exp(l - top) for l in lses]
        num = ws[0] * groups[0][4][...] + ws[1] * groups[1][4][...] + ws[2] * groups[2][4][...]
        o_ref[...] = (num / (ws[0] + ws[1] + ws[2])).astype(BF16)

    return tile, mix


def _dil_attn_specs(unit):
    in_specs, scratch = [], []
    for r in DILATIONS:
        nq = DIL_BLOCK // r
        per = nq // BAND

        def cur(s, r=r, nq=nq):
            def index(*ids):
                b, h, i = unit(*ids)
                return (s, b, h, 0, i, 0)
            return pl.BlockSpec((None, None, None, r, nq, HEAD_DIM), index)

        def prev(s, r=r, per=per):
            def index(*ids):
                b, h, i = unit(*ids)
                return (s, b, h, 0, jnp.maximum(i * per - 1, 0), 0)
            return pl.BlockSpec((None, None, None, r, BAND, HEAD_DIM), index)

        in_specs += [cur(0), cur(1), prev(1), cur(2), prev(2)]
        scratch += [pltpu.VMEM((r, BAND + nq, HEAD_DIM), BF16), pltpu.VMEM((r, BAND + nq, 2 * HEAD_DIM), BF16),
                    pltpu.VMEM((DIL_BLOCK, HEAD_DIM), F32), pltpu.VMEM((DIL_BLOCK, HEAD_DIM), F32)]

    def out_index(*ids):
        b, h, i = unit(*ids)
        return (b, i, h)

    return in_specs, pl.BlockSpec((None, DIL_BLOCK, HEAD_DIM), out_index), scratch


def _gate_dil_kernel(h_ref, *refs, kinds, chunk, n_i):
    n_parts = len(kinds) // PROJ_PART_HEADS
    n_dil = 5 * N_DIL_GROUPS
    w_refs, dil_ins = refs[:n_parts], refs[n_parts:n_parts + n_dil]
    vg_ref, od_ref, y_ref = refs[n_parts + n_dil:n_parts + n_dil + 3]
    dil_scr = refs[n_parts + n_dil + 3:]
    compute, finish = _proj_stages(h_ref, w_refs, None, None, None, vg_ref, y_ref, (), kinds=kinds, chunk=chunk,
                                   dilation=None)
    tile, mix = _dil_attn_stages(dil_ins, od_ref, dil_scr, pl.program_id(0) % n_i)
    n_chunks = h_ref.shape[0] // chunk
    per_chunk = (DIL_BLOCK // BAND) // n_chunks
    compute(0, 0)
    for m in range(n_chunks):
        for t in range(m * per_chunk, (m + 1) * per_chunk):
            tile(t)
        finish(m, m % 2)
        if m + 1 < n_chunks:
            compute(m + 1, (m + 1) % 2)
    mix()


def _gate_dil(h, w, qkvs, *, parts, kinds, batch, seq):
    T = h.shape[0]
    tm, chunk = PROJ_GATE_TM, PROJ_CHUNK
    n_i = seq // DIL_BLOCK
    steps = T // tm
    assert steps == batch * DIL_HEADS * n_i
    width = len(kinds) * HEAD_DIM
    unit = lambda s: (s // (DIL_HEADS * n_i), (s // n_i) % DIL_HEADS, s % n_i)
    dil_specs, od_spec, dil_scratch = _dil_attn_specs(unit)
    dil_args = [qkvs[g] for g in range(N_DIL_GROUPS) for _ in range(5)]
    return pl.pallas_call(
        functools.partial(_gate_dil_kernel, kinds=kinds, chunk=chunk, n_i=n_i),
        grid=(steps,),
        in_specs=[pl.BlockSpec((tm, D_MODEL), lambda s: (s, 0)), *_weight_specs(parts), *dil_specs],
        out_specs=[pl.BlockSpec((tm, width), lambda s: (s, 0)), od_spec],
        out_shape=[jax.ShapeDtypeStruct((T, width), BF16),
                   jax.ShapeDtypeStruct((batch, seq, DIL_GROUP_WIDTH), BF16)],
        scratch_shapes=[pltpu.VMEM((2, len(kinds), chunk, HEAD_DIM), F32), *dil_scratch],
        compiler_params=_params("arbitrary"),
        name="gate_dil",
    )(h, *([w] * len(parts)), *dil_args)


def _diff_attn_kernel(lq1_ref, lk1_ref, lq2_ref, lk2_ref, q_ref, k_ref, v_ref, sub_ref, *refs, lam_init, n_casts):
    cast_in, (o_ref, *cast_out) = refs[:n_casts], refs[n_casts:2 * n_casts + 1]
    vt_ref, acc_ref, s0_ref, s1_ref = refs[2 * n_casts + 1:]
    for src_ref, dst_ref in zip(cast_in, cast_out):
        dst_ref[...] = src_ref[...].astype(BF16)

    qi = pl.program_id(2)
    tq = q_ref.shape[0]
    tk = vt_ref.shape[2]
    dv = DIFF_HEAD_WIDTH
    per_q = tq // tk

    @pl.when(qi == 0)
    def _():
        for ch in range(vt_ref.shape[0]):
            vt_ref[ch, :dv, :] = v_ref[ch * tk:(ch + 1) * tk, :].astype(F32).T.astype(BF16)
            vt_ref[ch, dv:, :] = jnp.ones((vt_ref.shape[1] - dv, tk), BF16)

    q = q_ref[...]
    acc_ref[...] = jnp.zeros(acc_ref.shape, F32)

    def scores(kb, s_ref):
        k = k_ref[pl.ds(pl.multiple_of(kb * tk, tk), tk), :]
        for c in range(2):
            lanes = slice(c * HEAD_DIM, (c + 1) * HEAD_DIM)
            s_ref[c] = _dot_nt(k[:, lanes], q[:, lanes])

    def update(kb, s_ref, maxes, qcols=slice(None), mask=None):
        vt = vt_ref[kb]
        out = []
        for c in range(2):
            s = s_ref[c, :, qcols]
            if mask is not None:
                s = jnp.where(mask, s, NEG_BIG)
            m_new = jnp.maximum(maxes[c], jnp.max(s, axis=0, keepdims=True))
            alpha = jnp.exp2(maxes[c] - m_new)
            p = jnp.exp2(s - m_new)
            out.append(m_new)
            acc_ref[c, :, qcols] = alpha * acc_ref[c, :, qcols] + _dot(vt, p.astype(BF16))
        return tuple(out)

    def pair(j, maxes):
        kb = per_q * j
        scores(kb + 1, s1_ref)
        maxes = update(kb, s0_ref, maxes)
        scores(kb + 2, s0_ref)
        return update(kb + 1, s1_ref, maxes)

    row0 = jnp.full((1, tq), NEG_BIG, F32)
    scores(0, s0_ref)
    maxes = lax.fori_loop(0, qi, pair, (row0, row0))

    kd = per_q * qi
    lo, hi = slice(0, tk), slice(tk, tq)
    key = lax.broadcasted_iota(jnp.int32, (tk, tk), 0)
    qry = lax.broadcasted_iota(jnp.int32, (tk, tk), 1)
    tri = key <= qry
    k_last = k_ref[pl.ds(pl.multiple_of((kd + 1) * tk, tk), tk), :]
    for c in range(2):
        lanes = slice(c * HEAD_DIM, (c + 1) * HEAD_DIM)
        s1_ref[c, :, hi] = _dot_nt(k_last[:, lanes], q[tk:, lanes])
    update(kd, s0_ref, tuple(m[:, lo] for m in maxes), lo, tri)
    m_hi = update(kd, s0_ref, tuple(m[:, hi] for m in maxes), hi)
    update(kd + 1, s1_ref, m_hi, hi, tri)

    lam = (jnp.exp(jnp.sum(lq1_ref[...] * lk1_ref[...], axis=-1, keepdims=True))
           - jnp.exp(jnp.sum(lq2_ref[...] * lk2_ref[...], axis=-1, keepdims=True)) + lam_init)
    o = (acc_ref[0, :dv] / acc_ref[0, dv:dv + 1] - lam * (acc_ref[1, :dv] / acc_ref[1, dv:dv + 1])).T
    o_ref[...] = (o * _rms_scale(o) * sub_ref[...] * (1.0 - lam_init)).astype(BF16)


def _diff_attn(qk, vg, lq1, lk1, lq2, lk2, subln, casts, *, batch, seq, lam_init):
    tq, tk = DIFF_TQ, DIFF_TK
    assert tq == 2 * tk
    q_blk, k_blk, v_blk = 0, DIFF_HEADS, 0
    n_q = seq // tq
    steps = batch * DIFF_HEADS * n_q
    vec = pl.BlockSpec((1, HEAD_DIM), lambda b, h, i: (0, 0))

    cast_specs = []
    for w in casts:
        share = steps
        while w.shape[0] % (share * BF16_SUBLANES):
            share //= 2
        cast_specs.append(pl.BlockSpec(
            (w.shape[0] // share, w.shape[1]),
            functools.partial(lambda rep, b, h, i: (((b * DIFF_HEADS + h) * n_q + i) // rep, 0), steps // share)))
    o, *cast = pl.pallas_call(
        functools.partial(_diff_attn_kernel, lam_init=lam_init, n_casts=len(casts)),
        grid=(batch, DIFF_HEADS, n_q),
        in_specs=[
            vec, vec, vec, vec,
            pl.BlockSpec((None, tq, DIFF_HEAD_WIDTH), lambda b, h, i: (b, i, q_blk + h)),
            pl.BlockSpec((None, seq, DIFF_HEAD_WIDTH), lambda b, h, i: (b, 0, k_blk + h)),
            pl.BlockSpec((None, seq, DIFF_HEAD_WIDTH), lambda b, h, i: (b, 0, v_blk + h)),
            pl.BlockSpec((1, DIFF_HEAD_WIDTH), lambda b, h, i: (0, 0)),
            *cast_specs,
        ],
        out_specs=[pl.BlockSpec((None, tq, DIFF_HEAD_WIDTH), lambda b, h, i: (b, i, h)), *cast_specs],
        out_shape=[jax.ShapeDtypeStruct((batch, seq, DIFF_WIDTH), BF16),
                   *(jax.ShapeDtypeStruct(w.shape, BF16) for w in casts)],
        scratch_shapes=[pltpu.VMEM((seq // tk, DIFF_HEAD_WIDTH + DIFF_ONES_ROWS, tk), BF16),
                        pltpu.VMEM((2, DIFF_HEAD_WIDTH + DIFF_ONES_ROWS, tq), F32),
                        pltpu.VMEM((2, tk, tq), F32),
                        pltpu.VMEM((2, tk, tq), F32)],
        compiler_params=_params("arbitrary", "arbitrary", "arbitrary"),
        name="diff_attn",
    )(lq1, lk1, lq2, lk2, qk, qk, vg, subln, *casts)
    return o, cast


def _merge_kernel(x_ref, od_ref, of_ref, gd0_ref, gd1_ref, gf0_ref, gf1_ref, wa_ref, wb_ref, wo_ref, o_ref):
    gd = jnp.concatenate([gd0_ref[...], gd1_ref[...]], axis=1).astype(F32)
    gf = jnp.concatenate([gf0_ref[...], gf1_ref[...]], axis=1).astype(F32)
    y = gd * _dot(od_ref[...], wa_ref[...]) + gf * _dot(of_ref[...], wb_ref[...])
    o_ref[...] = x_ref[...] + _dot(y.astype(BF16), wo_ref[...])


def _merge(x, o_dil, o_diff, vg, wa, wb, wo):
    T = x.shape[0]
    tm = MERGE_TM
    half = D_MODEL // 2

    def tok(width, blk=0):
        return pl.BlockSpec((tm, width), lambda i: (i, blk))

    def resident(shape):
        return pl.BlockSpec(shape, lambda i: (0, 0), pipeline_mode=pl.Buffered(1))

    gate0 = DIFF_WIDTH // half
    return pl.pallas_call(
        _merge_kernel,
        grid=(T // tm,),
        in_specs=[tok(D_MODEL), tok(DIL_GROUP_WIDTH), tok(DIFF_WIDTH),
                  tok(half, gate0), tok(half, gate0 + 1), tok(half, gate0 + 2), tok(half, gate0 + 3),
                  resident(wa.shape), resident(wb.shape), resident(wo.shape)],
        out_specs=tok(D_MODEL),
        out_shape=jax.ShapeDtypeStruct((T, D_MODEL), F32),
        compiler_params=_params("parallel"),
        name="merge",
    )(x, o_dil, o_diff, vg, vg, vg, vg, wa, wb, wo)


def _rope_tables(positions):
    lane = jnp.arange(HEAD_DIM)
    freq = ROPE_THETA ** (-(2 * (lane % ROPE_HALF)).astype(F32) / ROPE_DIM)
    inv = jnp.where(lane < ROPE_DIM, freq, 0.0)
    sign = jnp.where(lane < ROPE_HALF, -1.0, jnp.where(lane < ROPE_DIM, 1.0, 0.0)).astype(F32)
    ang = positions.astype(F32).reshape(-1, 1) * inv
    return jnp.cos(ang), jnp.sin(ang) * sign


def _layer(x, tables, layer, ffn1_norm, ffn1_w_gate, ffn1_w_up, ffn1_w_down, mix_norm, w_in,
           dil_q_norm, dil_k_norm, diff_q_norm, diff_k_norm, diff_lq1, diff_lk1, diff_lq2, diff_lk2,
           diff_subln, w_dil_branch, w_diff_branch, w_out, ffn2_norm, ffn2_w_gate, ffn2_w_up, ffn2_w_down,
           *, batch, seq):
    cos, sin = tables
    row = lambda v: v.reshape(1, -1).astype(F32)
    bf = lambda w: w.astype(BF16)
    qk_scale = HEAD_DIM ** -0.5
    lam_init = 0.8 - 0.6 * math.exp(-0.3 * layer)

    x1, h = _ffn(x, row(ffn1_norm), bf(ffn1_w_gate), bf(ffn1_w_up), bf(ffn1_w_down), row(mix_norm))

    w_in_bf = bf(w_in)
    blocks = lambda start, width: tuple(range(start // PROJ_PART_WIDTH, (start + width) // PROJ_PART_WIDTH))
    diff0 = 3 * DIL_WIDTH
    fv0 = diff0 + 2 * DIFF_WIDTH
    gains = lambda gq, gk: jnp.stack([gq * qk_scale, gk]).astype(F32)

    dil_kinds = (QUERY,) * DIL_HEADS + (KEY,) * DIL_HEADS + (PLAIN,) * DIL_HEADS
    qkvs = [_proj(h, w_in_bf, gains(dil_q_norm, dil_k_norm), cos, sin,
                  parts=tuple(sec * N_DIL_GROUPS + g for sec in range(3)),
                  kinds=dil_kinds, tm=PROJ_TM, name=f"dil_proj{g}", dil=(r, batch, seq))
            for g, r in enumerate(DILATIONS)]

    n_diff = DIFF_WIDTH // HEAD_DIM
    qk = _proj(h, w_in_bf, gains(diff_q_norm * LOG2_E, diff_k_norm), cos, sin, parts=blocks(diff0, 2 * DIFF_WIDTH),
               kinds=(QUERY,) * n_diff + (KEY,) * n_diff, tm=PROJ_TM, name="diff_qk_proj")
    n_gate = 2 * D_MODEL // HEAD_DIM
    vg, o_dil = _gate_dil(h, w_in_bf, qkvs, parts=blocks(fv0, DIFF_WIDTH + 2 * D_MODEL),
                          kinds=(PLAIN,) * n_diff + (GATE,) * n_gate, batch=batch, seq=seq)

    later = [w_dil_branch, w_diff_branch, w_out, ffn2_w_gate, ffn2_w_up, ffn2_w_down]
    o_diff, (wa, wb, wo, wg2, wu2, wd2) = _diff_attn(
        qk.reshape(batch, seq, -1), vg.reshape(batch, seq, -1), row(diff_lq1), row(diff_lk1), row(diff_lq2),
        row(diff_lk2), row(diff_subln), later, batch=batch, seq=seq, lam_init=lam_init)

    x2 = _merge(x1, o_dil.reshape(batch * seq, -1), o_diff.reshape(batch * seq, -1), vg, wa, wb, wo)
    (out,) = _ffn(x2, row(ffn2_norm), wg2, wu2, wd2, None)
    return out


def kernel(x, positions, ffn1_norm, ffn1_w_gate, ffn1_w_up, ffn1_w_down, mix_norm, w_in, dil_q_norm, dil_k_norm, diff_q_norm, diff_k_norm, diff_lq1, diff_lk1, diff_lq2, diff_lk2, diff_subln, w_dil_branch, w_diff_branch, w_out, ffn2_norm, ffn2_w_gate, ffn2_w_up, ffn2_w_down):
    batch, seq, d_model = x.shape
    assert d_model == D_MODEL and seq % DIL_BLOCK == 0
    weights = (ffn1_norm, ffn1_w_gate, ffn1_w_up, ffn1_w_down, mix_norm, w_in, dil_q_norm, dil_k_norm,
               diff_q_norm, diff_k_norm, diff_lq1, diff_lk1, diff_lq2, diff_lk2, diff_subln, w_dil_branch,
               w_diff_branch, w_out, ffn2_norm, ffn2_w_gate, ffn2_w_up, ffn2_w_down)
    tables = _rope_tables(positions)
    y = x.reshape(batch * seq, d_model)
    for layer in range(ffn1_norm.shape[0]):
        y = _layer(y, tables, layer, *(w[layer] for w in weights), batch=batch, seq=seq)
    return y.reshape(batch, seq, d_model)
```

```python
import functools
import math

import jax
import jax.numpy as jnp
from jax import lax
from jax.experimental import pallas as pl
from jax.experimental.pallas import tpu as pltpu

F32 = jnp.float32
BF16 = jnp.bfloat16

D_MODEL = 2048
D_FF = 5632
HEAD_DIM = 128
ROPE_DIM = HEAD_DIM // 4
ROPE_HALF = ROPE_DIM // 2
ROPE_THETA = 500000.0
RMS_EPS = 1e-6
BAND = 128
DILATIONS = (1, 4, 16)
N_DIL_GROUPS = len(DILATIONS)
DIL_HEADS = 4
DIL_GROUP_WIDTH = DIL_HEADS * HEAD_DIM
DIL_WIDTH = N_DIL_GROUPS * DIL_GROUP_WIDTH
DIFF_HEADS = 4
DIFF_HEAD_WIDTH = 2 * HEAD_DIM
DIFF_WIDTH = DIFF_HEADS * DIFF_HEAD_WIDTH
DIL_BLOCK = BAND * DILATIONS[-1]
LOG2_E = math.log2(math.e)
NEG_BIG = -1e30

VMEM_LIMIT_BYTES = 62 * 1024 * 1024

FFN_TM = 1024
FFN_TF = 512
FFN_HEAD_TF = 256
PROJ_TM = 1024
PROJ_GATE_TM = 512
PROJ_CHUNK = 256
PROJ_PART_HEADS = 4
PROJ_PART_WIDTH = PROJ_PART_HEADS * HEAD_DIM
MAX_SINGLE_OP_STRIDE = 4
MERGE_TM = 512
DIFF_TQ = 1024
DIFF_TK = 512
BF16_SUBLANES = 16
DIFF_ONES_ROWS = BF16_SUBLANES


def _params(*sem):
    return pltpu.CompilerParams(dimension_semantics=sem, vmem_limit_bytes=VMEM_LIMIT_BYTES)


def _rms_scale(x):
    return lax.rsqrt(jnp.mean(x * x, axis=-1, keepdims=True) + RMS_EPS)


def _dot(a, b):
    return jnp.dot(a, b, preferred_element_type=F32)


def _dot_nt(a, b):
    return lax.dot_general(a, b, (((1,), (1,)), ((), ())), preferred_element_type=F32)


def _ffn_kernel(*refs, emit_norm, emit_weights, n_aliased):
    x_ref, gin_ref, wg_ref, wu_ref, wd_ref = refs[:5]
    gout_ref = refs[5] if emit_norm else None
    outs = list(refs[5 + emit_norm + n_aliased:])
    o_ref = outs.pop(0)
    hn_ref = outs.pop(0) if emit_norm else None
    w_outs = [outs.pop(0) for _ in range(3)] if emit_weights else None
    (h_ref,) = outs
    f = pl.program_id(1)
    last = pl.num_programs(1) - 1

    def step(h):
        wg, wu, wd = wg_ref[...], wu_ref[...], wd_ref[...]
        if emit_weights:
            wg, wu, wd = wg.astype(BF16), wu.astype(BF16), wd.astype(BF16)
            for w_out, w in zip(w_outs, (wg, wu, wd)):
                w_out[...] = w
        g = _dot(h, wg)
        u = _dot(h, wu)
        a = (g * jax.nn.sigmoid(g)) * u * 0.5
        o_ref[...] += _dot(a.astype(BF16), wd)

    def finish():
        if emit_norm:
            y = o_ref[...]
            hn_ref[...] = (y * _rms_scale(y) * gout_ref[...]).astype(BF16)

    @pl.when(f == 0)
    def _():
        x = x_ref[...]
        h = (x * _rms_scale(x) * gin_ref[...]).astype(BF16)
        h_ref[...] = h
        o_ref[...] = x
        step(h)

    @pl.when((f > 0) & (f < last))
    def _():
        step(h_ref[...])

    @pl.when(f == last)
    def _():
        step(h_ref[...])
        finish()


def _ffn(x, g_in, wg, wu, wd, g_out, *, first_tile=0, n_tiles=None, partial=()):
    T = x.shape[0]
    tm = FFN_TM
    emit_norm = g_out is not None
    emit_weights = wg.dtype == F32
    tf = FFN_HEAD_TF if emit_weights else FFN_TF
    n_tiles = T // tm - first_tile if n_tiles is None else n_tiles
    mode = dict(pipeline_mode=pl.Buffered(1)) if n_tiles == 1 else {}
    tok = pl.BlockSpec((tm, D_MODEL), lambda i, f: (i + first_tile, 0), **mode)
    vec = pl.BlockSpec((1, D_MODEL), lambda i, f: (0, 0))
    w_up_spec = pl.BlockSpec((D_MODEL, tf), lambda i, f: (0, f))
    w_down_spec = pl.BlockSpec((tf, D_MODEL), lambda i, f: (f, 0))
    in_specs = [tok, vec, w_up_spec, w_up_spec, w_down_spec]
    args = [x, g_in, wg, wu, wd]
    out_specs = [tok]
    out_shape = [jax.ShapeDtypeStruct((T, D_MODEL), F32)]
    if emit_norm:
        in_specs.append(vec)
        args.append(g_out)
        out_specs.append(tok)
        out_shape.append(jax.ShapeDtypeStruct((T, D_MODEL), BF16))
    if emit_weights:
        out_specs += [w_up_spec, w_up_spec, w_down_spec]
        out_shape += [jax.ShapeDtypeStruct(w.shape, BF16) for w in (wg, wu, wd)]
    aliases = {len(args) + k: k for k in range(len(partial))}
    in_specs += [pl.BlockSpec(memory_space=pl.ANY)] * len(partial)
    args += list(partial)
    return pl.pallas_call(
        functools.partial(_ffn_kernel, emit_norm=emit_norm, emit_weights=emit_weights, n_aliased=len(partial)),
        grid=(n_tiles, D_FF // tf),
        in_specs=in_specs,
        out_specs=out_specs,
        out_shape=out_shape,
        scratch_shapes=[pltpu.VMEM((tm, D_MODEL), BF16)],
        input_output_aliases=aliases,
        compiler_params=_params("parallel", "arbitrary"),
        name="ffn_head" if emit_weights else "ffn",
    )(*args)


QUERY, KEY, PLAIN, GATE = "query", "key", "plain", "gate"


def _sigmoid(y):
    return 0.5 * jnp.tanh(0.5 * y) + 0.5


def _norm_rope(y, gain, cos, sin):
    y = y * _rms_scale(y) * gain
    lane = lax.broadcasted_iota(jnp.int32, y.shape, 1)
    partner = jnp.where(lane < ROPE_HALF, pltpu.roll(y, HEAD_DIM - ROPE_HALF, 1), pltpu.roll(y, ROPE_HALF, 1))
    return y * cos + partner * sin


def _proj_stages(h_ref, w_refs, gain_ref, cos_ref, sin_ref, o_ref, y_ref, slab, *, kinds, chunk, dilation):
    def compute(m, slot):
        h = h_ref[pl.ds(m * chunk, chunk), :]
        for p, w_ref in enumerate(w_refs):
            y = _dot(h, w_ref[...])
            for j in range(PROJ_PART_HEADS):
                y_ref[slot, p * PROJ_PART_HEADS + j] = y[:, j * HEAD_DIM:(j + 1) * HEAD_DIM]

    def finish(m, slot):
        rows = pl.ds(m * chunk, chunk)
        for hh, kind in enumerate(kinds):
            y = y_ref[slot, hh]
            if kind in (QUERY, KEY):
                gain = gain_ref[0:1, :] if kind == QUERY else gain_ref[1:2, :]
                y = _norm_rope(y, gain, cos_ref[rows, :], sin_ref[rows, :])
            elif kind == GATE:
                y = _sigmoid(y)
            if dilation is None:
                o_ref[rows, hh * HEAD_DIM:(hh + 1) * HEAD_DIM] = y.astype(BF16)
            else:
                sec, head = divmod(hh, DIL_HEADS)
                n = chunk // dilation
                dst = pl.ds(m * n, n)
                if dilation == 1:
                    o_ref[sec, head, 0, dst, :] = y.astype(BF16)
                elif dilation <= MAX_SINGLE_OP_STRIDE:
                    slab_ref = slab[0]
                    slab_ref[hh] = y
                    for c in range(dilation):
                        o_ref[sec, head, c, dst, :] = slab_ref[hh, pl.ds(c, n, stride=dilation), :].astype(BF16)
                else:
                    slab_ref, slab2_ref = slab
                    step = MAX_SINGLE_OP_STRIDE
                    assert dilation == step * step
                    n1 = chunk // step
                    slab_ref[hh] = y
                    for c_lo in range(step):
                        slab2_ref[hh, c_lo * n1:(c_lo + 1) * n1, :] = slab_ref[hh, pl.ds(c_lo, n1, stride=step), :]
                    for c_lo in range(step):
                        for c_hi in range(step):
                            o_ref[sec, head, c_lo + step * c_hi, dst, :] = slab2_ref[
                                hh, pl.ds(c_lo * n1 + c_hi, n, stride=step), :].astype(BF16)

    return compute, finish


def _proj_kernel(h_ref, *refs, kinds, chunk, dilation):
    n_parts = len(kinds) // PROJ_PART_HEADS
    gain_ref, cos_ref, sin_ref, o_ref, y_ref = refs[n_parts:n_parts + 5]
    compute, finish = _proj_stages(h_ref, refs[:n_parts], gain_ref, cos_ref, sin_ref, o_ref, y_ref,
                                   refs[n_parts + 5:], kinds=kinds, chunk=chunk, dilation=dilation)
    n_chunks = h_ref.shape[0] // chunk
    compute(0, 0)
    for m in range(n_chunks - 1):
        finish(m, m % 2)
        compute(m + 1, (m + 1) % 2)
    finish(n_chunks - 1, (n_chunks - 1) % 2)


def _weight_specs(parts):
    return [pl.BlockSpec((D_MODEL, PROJ_PART_WIDTH), functools.partial(lambda blk, i: (0, blk), blk),
                         pipeline_mode=pl.Buffered(1)) for blk in parts]


def _proj(h, w, gains, cos, sin, *, parts, kinds, tm, name, dil=None):
    T = h.shape[0]
    width = len(kinds) * HEAD_DIM
    assert width == len(parts) * PROJ_PART_WIDTH and w.shape[1] % PROJ_PART_WIDTH == 0
    chunk = PROJ_CHUNK
    scratch = [pltpu.VMEM((2, len(kinds), chunk, HEAD_DIM), F32)]
    if dil is None:
        dilation = None
        out_spec = pl.BlockSpec((tm, width), lambda i: (i, 0))
        out_shape = jax.ShapeDtypeStruct((T, width), BF16)
    else:
        dilation, batch, seq = dil
        n_i = seq // tm
        out_spec = pl.BlockSpec((3, None, DIL_HEADS, dilation, tm // dilation, HEAD_DIM),
                                lambda i: (0, i // n_i, 0, 0, i % n_i, 0))
        out_shape = jax.ShapeDtypeStruct((3, batch, DIL_HEADS, dilation, seq // dilation, HEAD_DIM), BF16)
        if dilation > 1:
            scratch.append(pltpu.VMEM((len(kinds), chunk, HEAD_DIM), F32))
        if dilation > MAX_SINGLE_OP_STRIDE:
            scratch.append(pltpu.VMEM((len(kinds), chunk, HEAD_DIM), F32))
    tok = lambda cols: pl.BlockSpec((tm, cols), lambda i: (i, 0))
    return pl.pallas_call(
        functools.partial(_proj_kernel, kinds=kinds, chunk=chunk, dilation=dilation),
        grid=(T // tm,),
        in_specs=[tok(D_MODEL), *_weight_specs(parts), pl.BlockSpec((2, HEAD_DIM), lambda i: (0, 0)),
                  tok(HEAD_DIM), tok(HEAD_DIM)],
        out_specs=out_spec,
        out_shape=out_shape,
        scratch_shapes=scratch,
        compiler_params=_params("parallel"),
        name=name,
    )(h, *([w] * len(parts)), gains, cos, sin)


def _dil_attn_stages(ins, o_ref, scr, i):
    steps = DIL_BLOCK // BAND

    row = lax.broadcasted_iota(jnp.int32, (BAND, 2 * BAND), 0)
    col = lax.broadcasted_iota(jnp.int32, (BAND, 2 * BAND), 1)
    band_bias = jnp.where((col >= row) & (col <= row + BAND), 0.0, NEG_BIG)
    start_bias = jnp.where(i == 0, jnp.where(col >= BAND, band_bias, NEG_BIG), band_bias)

    groups = []
    for g, r in enumerate(DILATIONS):
        q_ref, kc_ref, kp_ref, vc_ref, vp_ref = ins[5 * g:5 * g + 5]
        kx_ref, vx_ref, on_ref, ln_ref = scr[4 * g:4 * g + 4]
        kx_ref[:, :BAND, :] = kp_ref[...]
        kx_ref[:, BAND:, :] = kc_ref[...]
        vx_ref[:, :BAND, :HEAD_DIM] = vp_ref[...]
        vx_ref[:, BAND:, :HEAD_DIM] = vc_ref[...]
        vx_ref[:, :, HEAD_DIM:] = jnp.ones((r, vx_ref.shape[1], HEAD_DIM), BF16)
        groups.append((r, q_ref, kx_ref, vx_ref, on_ref, ln_ref))

    def tile(t):
        for r, q_ref, kx_ref, vx_ref, on_ref, ln_ref in groups:
            tiles = DIL_BLOCK // r // BAND
            c, mb = (0, t) if tiles == steps else (t, 0) if tiles == 1 else (t % r, t // r)
            m0 = mb * BAND
            q = q_ref[c, pl.ds(m0, BAND), :]
            k = kx_ref[c, pl.ds(m0, 2 * BAND), :]
            v = vx_ref[c, pl.ds(m0, 2 * BAND), :]
            s = _dot_nt(q, k) + (start_bias if mb == 0 else band_bias)
            m = jnp.max(s, axis=-1, keepdims=True)
            p = jnp.exp(s - m)
            od = _dot(p.astype(BF16), v)
            den = od[:, HEAD_DIM:]
            dst = pl.ds(mb * (BAND * r) + c, BAND, stride=r) if r > 1 else pl.ds(m0, BAND)
            on_ref[dst, :] = od[:, :HEAD_DIM] / den
            ln_ref[dst, :] = m + jnp.log(den)

    def mix():
        lses = [grp[5][...] for grp in groups]
        top = jnp.maximum(jnp.maximum(lses[0], lses[1]), lses[2])
        ws = [jnp.exp(l - top) for l in lses]
        num = ws[0] * groups[0][4][...] + ws[1] * groups[1][4][...] + ws[2] * groups[2][4][...]
        o_ref[...] = (num / (ws[0] + ws[1] + ws[2])).astype(BF16)

    return tile, mix


def _dil_attn_specs(unit):
    in_specs, scratch = [], []
    for r in DILATIONS:
        nq = DIL_BLOCK // r
        per = nq // BAND

        def cur(s, r=r, nq=nq):
            def index(*ids):
                b, h, i = unit(*ids)
                return (s, b, h, 0, i, 0)
            return pl.BlockSpec((None, None, None, r, nq, HEAD_DIM), index)

        def prev(s, r=r, per=per):
            def index(*ids):
                b, h, i = unit(*ids)
                return (s, b, h, 0, jnp.maximum(i * per - 1, 0), 0)
            return pl.BlockSpec((None, None, None, r, BAND, HEAD_DIM), index)

        in_specs += [cur(0), cur(1), prev(1), cur(2), prev(2)]
        scratch += [pltpu.VMEM((r, BAND + nq, HEAD_DIM), BF16), pltpu.VMEM((r, BAND + nq, 2 * HEAD_DIM), BF16),
                    pltpu.VMEM((DIL_BLOCK, HEAD_DIM), F32), pltpu.VMEM((DIL_BLOCK, HEAD_DIM), F32)]

    def out_index(*ids):
        b, h, i = unit(*ids)
        return (b, i, h)

    return in_specs, pl.BlockSpec((None, DIL_BLOCK, HEAD_DIM), out_index), scratch


def _gate_dil_kernel(h_ref, *refs, kinds, chunk, n_i):
    n_parts = len(kinds) // PROJ_PART_HEADS
    n_dil = 5 * N_DIL_GROUPS
    w_refs, dil_ins = refs[:n_parts], refs[n_parts:n_parts + n_dil]
    vg_ref, od_ref, y_ref = refs[n_parts + n_dil:n_parts + n_dil + 3]
    dil_scr = refs[n_parts + n_dil + 3:]
    compute, finish = _proj_stages(h_ref, w_refs, None, None, None, vg_ref, y_ref, (), kinds=kinds, chunk=chunk,
                                   dilation=None)
    tile, mix = _dil_attn_stages(dil_ins, od_ref, dil_scr, pl.program_id(0) % n_i)
    n_chunks = h_ref.shape[0] // chunk
    per_chunk = (DIL_BLOCK // BAND) // n_chunks
    compute(0, 0)
    for m in range(n_chunks):
        for t in range(m * per_chunk, (m + 1) * per_chunk):
            tile(t)
        finish(m, m % 2)
        if m + 1 < n_chunks:
            compute(m + 1, (m + 1) % 2)
    mix()


def _gate_dil(h, w, qkvs, *, parts, kinds, batch, seq):
    T = h.shape[0]
    tm, chunk = PROJ_GATE_TM, PROJ_CHUNK
    n_i = seq // DIL_BLOCK
    steps = T // tm
    assert steps == batch * DIL_HEADS * n_i
    width = len(kinds) * HEAD_DIM
    unit = lambda s: (s // (DIL_HEADS * n_i), (s // n_i) % DIL_HEADS, s % n_i)
    dil_specs, od_spec, dil_scratch = _dil_attn_specs(unit)
    dil_args = [qkvs[g] for g in range(N_DIL_GROUPS) for _ in range(5)]
    return pl.pallas_call(
        functools.partial(_gate_dil_kernel, kinds=kinds, chunk=chunk, n_i=n_i),
        grid=(steps,),
        in_specs=[pl.BlockSpec((tm, D_MODEL), lambda s: (s, 0)), *_weight_specs(parts), *dil_specs],
        out_specs=[pl.BlockSpec((tm, width), lambda s: (s, 0)), od_spec],
        out_shape=[jax.ShapeDtypeStruct((T, width), BF16),
                   jax.ShapeDtypeStruct((batch, seq, DIL_GROUP_WIDTH), BF16)],
        scratch_shapes=[pltpu.VMEM((2, len(kinds), chunk, HEAD_DIM), F32), *dil_scratch],
        compiler_params=_params("arbitrary"),
        name="gate_dil",
    )(h, *([w] * len(parts)), *dil_args)


def _diff_attn_kernel(lq1_ref, lk1_ref, lq2_ref, lk2_ref, q_ref, k_ref, v_ref, sub_ref, *refs, lam_init, n_casts):
    cast_in, (o_ref, *cast_out) = refs[:n_casts], refs[n_casts:2 * n_casts + 1]
    vt_ref, acc_ref, s0_ref, s1_ref = refs[2 * n_casts + 1:]
    for src_ref, dst_ref in zip(cast_in, cast_out):
        dst_ref[...] = src_ref[...].astype(BF16)

    qi = pl.program_id(2)
    tq = q_ref.shape[0]
    tk = vt_ref.shape[2]
    dv = DIFF_HEAD_WIDTH
    per_q = tq // tk

    @pl.when(qi == 0)
    def _():
        for ch in range(vt_ref.shape[0]):
            vt_ref[ch, :dv, :] = v_ref[ch * tk:(ch + 1) * tk, :].astype(F32).T.astype(BF16)
            vt_ref[ch, dv:, :] = jnp.ones((vt_ref.shape[1] - dv, tk), BF16)

    q = q_ref[...]
    acc_ref[...] = jnp.zeros(acc_ref.shape, F32)

    def scores(kb, s_ref):
        k = k_ref[pl.ds(pl.multiple_of(kb * tk, tk), tk), :]
        for c in range(2):
            lanes = slice(c * HEAD_DIM, (c + 1) * HEAD_DIM)
            s_ref[c] = _dot_nt(k[:, lanes], q[:, lanes])

    def update(kb, s_ref, maxes, qcols=slice(None), mask=None):
        vt = vt_ref[kb]
        out = []
        for c in range(2):
            s = s_ref[c, :, qcols]
            if mask is not None:
                s = jnp.where(mask, s, NEG_BIG)
            m_new = jnp.maximum(maxes[c], jnp.max(s, axis=0, keepdims=True))
            alpha = jnp.exp2(maxes[c] - m_new)
            p = jnp.exp2(s - m_new)
            out.append(m_new)
            acc_ref[c, :, qcols] = alpha * acc_ref[c, :, qcols] + _dot(vt, p.astype(BF16))
        return tuple(out)

    def pair(j, maxes):
        kb = per_q * j
        scores(kb + 1, s1_ref)
        maxes = update(kb, s0_ref, maxes)
        scores(kb + 2, s0_ref)
        return update(kb + 1, s1_ref, maxes)

    row0 = jnp.full((1, tq), NEG_BIG, F32)
    scores(0, s0_ref)
    maxes = lax.fori_loop(0, qi, pair, (row0, row0))

    kd = per_q * qi
    lo, hi = slice(0, tk), slice(tk, tq)
    key = lax.broadcasted_iota(jnp.int32, (tk, tk), 0)
    qry = lax.broadcasted_iota(jnp.int32, (tk, tk), 1)
    tri = key <= qry
    k_last = k_ref[pl.ds(pl.multiple_of((kd + 1) * tk, tk), tk), :]
    for c in range(2):
        lanes = slice(c * HEAD_DIM, (c + 1) * HEAD_DIM)
        s1_ref[c, :, hi] = _dot_nt(k_last[:, lanes], q[tk:, lanes])
    update(kd, s0_ref, tuple(m[:, lo] for m in maxes), lo, tri)
    m_hi = update(kd, s0_ref, tuple(m[:, hi] for m in maxes), hi)
    update(kd + 1, s1_ref, m_hi, hi, tri)

    lam = (jnp.exp(jnp.sum(lq1_ref[...] * lk1_ref[...], axis=-1, keepdims=True))
           - jnp.exp(jnp.sum(lq2_ref[...] * lk2_ref[...], axis=-1, keepdims=True)) + lam_init)
    o = (acc_ref[0, :dv] / acc_ref[0, dv:dv + 1] - lam * (acc_ref[1, :dv] / acc_ref[1, dv:dv + 1])).T
    o_ref[...] = (o * _rms_scale(o) * sub_ref[...] * (1.0 - lam_init)).astype(BF16)


def _diff_attn(qk, vg, lq1, lk1, lq2, lk2, subln, casts, *, batch, seq, lam_init):
    tq, tk = DIFF_TQ, DIFF_TK
    assert tq == 2 * tk
    q_blk, k_blk, v_blk = 0, DIFF_HEADS, 0
    n_q = seq // tq
    steps = batch * DIFF_HEADS * n_q
    vec = pl.BlockSpec((1, HEAD_DIM), lambda b, h, i: (0, 0))

    cast_specs = []
    for w in casts:
        share = steps
        while w.shape[0] % (share * BF16_SUBLANES):
            share //= 2
        cast_specs.append(pl.BlockSpec(
            (w.shape[0] // share, w.shape[1]),
            functools.partial(lambda rep, b, h, i: (((b * DIFF_HEADS + h) * n_q + i) // rep, 0), steps // share)))
    o, *cast = pl.pallas_call(
        functools.partial(_diff_attn_kernel, lam_init=lam_init, n_casts=len(casts)),
        grid=(batch, DIFF_HEADS, n_q),
        in_specs=[
            vec, vec, vec, vec,
            pl.BlockSpec((None, tq, DIFF_HEAD_WIDTH), lambda b, h, i: (b, i, q_blk + h)),
            pl.BlockSpec((None, seq, DIFF_HEAD_WIDTH), lambda b, h, i: (b, 0, k_blk + h)),
            pl.BlockSpec((None, seq, DIFF_HEAD_WIDTH), lambda b, h, i: (b, 0, v_blk + h)),
            pl.BlockSpec((1, DIFF_HEAD_WIDTH), lambda b, h, i: (0, 0)),
            *cast_specs,
        ],
        out_specs=[pl.BlockSpec((None, tq, DIFF_HEAD_WIDTH), lambda b, h, i: (b, i, h)), *cast_specs],
        out_shape=[jax.ShapeDtypeStruct((batch, seq, DIFF_WIDTH), BF16),
                   *(jax.ShapeDtypeStruct(w.shape, BF16) for w in casts)],
        scratch_shapes=[pltpu.VMEM((seq // tk, DIFF_HEAD_WIDTH + DIFF_ONES_ROWS, tk), BF16),
                        pltpu.VMEM((2, DIFF_HEAD_WIDTH + DIFF_ONES_ROWS, tq), F32),
                        pltpu.VMEM((2, tk, tq), F32),
                        pltpu.VMEM((2, tk, tq), F32)],
        compiler_params=_params("arbitrary", "arbitrary", "arbitrary"),
        name="diff_attn",
    )(lq1, lk1, lq2, lk2, qk, qk, vg, subln, *casts)
    return o, cast


def _merge_kernel(x_ref, od_ref, of_ref, gd0_ref, gd1_ref, gf0_ref, gf1_ref, wa_ref, wb_ref, wo_ref, o_ref):
    gd = jnp.concatenate([gd0_ref[...], gd1_ref[...]], axis=1).astype(F32)
    gf = jnp.concatenate([gf0_ref[...], gf1_ref[...]], axis=1).astype(F32)
    y = gd * _dot(od_ref[...], wa_ref[...]) + gf * _dot(of_ref[...], wb_ref[...])
    o_ref[...] = x_ref[...] + _dot(y.astype(BF16), wo_ref[...])


def _merge(x, o_dil, o_diff, vg, wa, wb, wo):
    T = x.shape[0]
    tm = MERGE_TM
    half = D_MODEL // 2

    def tok(width, blk=0):
        return pl.BlockSpec((tm, width), lambda i: (i, blk))

    def resident(shape):
        return pl.BlockSpec(shape, lambda i: (0, 0), pipeline_mode=pl.Buffered(1))

    gate0 = DIFF_WIDTH // half
    return pl.pallas_call(
        _merge_kernel,
        grid=(T // tm,),
        in_specs=[tok(D_MODEL), tok(DIL_GROUP_WIDTH), tok(DIFF_WIDTH),
                  tok(half, gate0), tok(half, gate0 + 1), tok(half, gate0 + 2), tok(half, gate0 + 3),
                  resident(wa.shape), resident(wb.shape), resident(wo.shape)],
        out_specs=tok(D_MODEL),
        out_shape=jax.ShapeDtypeStruct((T, D_MODEL), F32),
        compiler_params=_params("parallel"),
        name="merge",
    )(x, o_dil, o_diff, vg, vg, vg, vg, wa, wb, wo)


def _rope_tables(positions):
    lane = jnp.arange(HEAD_DIM)
    freq = ROPE_THETA ** (-(2 * (lane % ROPE_HALF)).astype(F32) / ROPE_DIM)
    inv = jnp.where(lane < ROPE_DIM, freq, 0.0)
    sign = jnp.where(lane < ROPE_HALF, -1.0, jnp.where(lane < ROPE_DIM, 1.0, 0.0)).astype(F32)
    ang = positions.astype(F32).reshape(-1, 1) * inv
    return jnp.cos(ang), jnp.sin(ang) * sign


def _layer(x, tables, layer, ffn1_norm, ffn1_w_gate, ffn1_w_up, ffn1_w_down, mix_norm, w_in,
           dil_q_norm, dil_k_norm, diff_q_norm, diff_k_norm, diff_lq1, diff_lk1, diff_lq2, diff_lk2,
           diff_subln, w_dil_branch, w_diff_branch, w_out, ffn2_norm, ffn2_w_gate, ffn2_w_up, ffn2_w_down,
           *, batch, seq):
    cos, sin = tables
    row = lambda v: v.reshape(1, -1).astype(F32)
    bf = lambda w: w.astype(BF16)
    qk_scale = HEAD_DIM ** -0.5
    lam_init = 0.8 - 0.6 * math.exp(-0.3 * layer)

    x1, h, wg1, wu1, wd1 = _ffn(x, row(ffn1_norm), ffn1_w_gate, ffn1_w_up, ffn1_w_down, row(mix_norm), n_tiles=1)
    x1, h = _ffn(x, row(ffn1_norm), wg1, wu1, wd1, row(mix_norm), first_tile=1, partial=(x1, h))

    w_in_bf = bf(w_in)
    blocks = lambda start, width: tuple(range(start // PROJ_PART_WIDTH, (start + width) // PROJ_PART_WIDTH))
    diff0 = 3 * DIL_WIDTH
    fv0 = diff0 + 2 * DIFF_WIDTH
    gains = lambda gq, gk: jnp.stack([gq * qk_scale, gk]).astype(F32)

    dil_kinds = (QUERY,) * DIL_HEADS + (KEY,) * DIL_HEADS + (PLAIN,) * DIL_HEADS
    qkvs = [_proj(h, w_in_bf, gains(dil_q_norm, dil_k_norm), cos, sin,
                  parts=tuple(sec * N_DIL_GROUPS + g for sec in range(3)),
                  kinds=dil_kinds, tm=PROJ_TM, name=f"dil_proj{g}", dil=(r, batch, seq))
            for g, r in enumerate(DILATIONS)]

    n_diff = DIFF_WIDTH // HEAD_DIM
    qk = _proj(h, w_in_bf, gains(diff_q_norm * LOG2_E, diff_k_norm), cos, sin, parts=blocks(diff0, 2 * DIFF_WIDTH),
               kinds=(QUERY,) * n_diff + (KEY,) * n_diff, tm=PROJ_TM, name="diff_qk_proj")
    n_gate = 2 * D_MODEL // HEAD_DIM
    vg, o_dil = _gate_dil(h, w_in_bf, qkvs, parts=blocks(fv0, DIFF_WIDTH + 2 * D_MODEL),
                          kinds=(PLAIN,) * n_diff + (GATE,) * n_gate, batch=batch, seq=seq)

    later = [w_dil_branch, w_diff_branch, w_out, ffn2_w_gate, ffn2_w_up, ffn2_w_down]
    o_diff, (wa, wb, wo, wg2, wu2, wd2) = _diff_attn(
        qk.reshape(batch, seq, -1), vg.reshape(batch, seq, -1), row(diff_lq1), row(diff_lk1), row(diff_lq2),
        row(diff_lk2), row(diff_subln), later, batch=batch, seq=seq, lam_init=lam_init)

    x2 = _merge(x1, o_dil.reshape(batch * seq, -1), o_diff.reshape(batch * seq, -1), vg, wa, wb, wo)
    (out,) = _ffn(x2, row(ffn2_norm), wg2, wu2, wd2, None)
    return out


def kernel(x, positions, ffn1_norm, ffn1_w_gate, ffn1_w_up, ffn1_w_down, mix_norm, w_in, dil_q_norm, dil_k_norm, diff_q_norm, diff_k_norm, diff_lq1, diff_lk1, diff_lq2, diff_lk2, diff_subln, w_dil_branch, w_diff_branch, w_out, ffn2_norm, ffn2_w_gate, ffn2_w_up, ffn2_w_down):
    batch, seq, d_model = x.shape
    assert d_model == D_MODEL and seq % DIL_BLOCK == 0
    weights = (ffn1_norm, ffn1_w_gate, ffn1_w_up, ffn1_w_down, mix_norm, w_in, dil_q_norm, dil_k_norm,
               diff_q_norm, diff_k_norm, diff_lq1, diff_lk1, diff_lq2, diff_lk2, diff_subln, w_dil_branch,
               w_diff_branch, w_out, ffn2_norm, ffn2_w_gate, ffn2_w_up, ffn2_w_down)
    tables = _rope_tables(positions)
    y = x.reshape(batch * seq, d_model)
    for layer in range(ffn1_norm.shape[0]):
        y = _layer(y, tables, layer, *(w[layer] for w in weights), batch=batch, seq=seq)
    return y.reshape(batch, seq, d_model)
```

```python
import functools
import math

import jax
import jax.numpy as jnp
from jax import lax
from jax.experimental import pallas as pl
from jax.experimental.pallas import tpu as pltpu

F32 = jnp.float32
BF16 = jnp.bfloat16

D_MODEL = 2048
D_FF = 5632
HEAD_DIM = 128
ROPE_DIM = HEAD_DIM // 4
ROPE_HALF = ROPE_DIM // 2
ROPE_THETA = 500000.0
RMS_EPS = 1e-6
BAND = 128
DILATIONS = (1, 4, 16)
N_DIL_GROUPS = len(DILATIONS)
DIL_HEADS = 4
DIL_GROUP_WIDTH = DIL_HEADS * HEAD_DIM
DIL_WIDTH = N_DIL_GROUPS * DIL_GROUP_WIDTH
DIFF_HEADS = 4
DIFF_HEAD_WIDTH = 2 * HEAD_DIM
DIFF_WIDTH = DIFF_HEADS * DIFF_HEAD_WIDTH
DIL_SECTION_SLOT = (2, 0, 1)
DIL_BLOCK = BAND * DILATIONS[-1]
LOG2_E = math.log2(math.e)
NEG_BIG = -1e30

VMEM_LIMIT_BYTES = 62 * 1024 * 1024

FFN_TM = 1024
FFN_TF = 512
FFN_HEAD_TF = 256
PROJ_TM = 1024
PROJ_GATE_TM = 512
PROJ_CHUNK = 256
PROJ_PART_HEADS = 4
PROJ_PART_WIDTH = PROJ_PART_HEADS * HEAD_DIM
MAX_SINGLE_OP_STRIDE = 4
MERGE_TM = 512
DIFF_TQ = 1024
DIFF_TK = 512
BF16_SUBLANES = 16
DIFF_ONES_ROWS = BF16_SUBLANES


def _params(*sem):
    return pltpu.CompilerParams(dimension_semantics=sem, vmem_limit_bytes=VMEM_LIMIT_BYTES)


def _rms_scale(x):
    return lax.rsqrt(jnp.mean(x * x, axis=-1, keepdims=True) + RMS_EPS)


def _dot(a, b):
    return jnp.dot(a, b, preferred_element_type=F32)


def _dot_nt(a, b):
    return lax.dot_general(a, b, (((1,), (1,)), ((), ())), preferred_element_type=F32)


def _ffn_kernel(*refs, emit_norm, emit_weights, n_aliased):
    x_ref, gin_ref, wg_ref, wu_ref, wd_ref = refs[:5]
    gout_ref = refs[5] if emit_norm else None
    outs = list(refs[5 + emit_norm + n_aliased:])
    o_ref = outs.pop(0)
    hn_ref = outs.pop(0) if emit_norm else None
    w_outs = [outs.pop(0) for _ in range(3)] if emit_weights else None
    (h_ref,) = outs
    f = pl.program_id(1)
    last = pl.num_programs(1) - 1

    def step(h):
        wg, wu, wd = wg_ref[...], wu_ref[...], wd_ref[...]
        if emit_weights:
            wg, wu, wd = wg.astype(BF16), wu.astype(BF16), wd.astype(BF16)
            for w_out, w in zip(w_outs, (wg, wu, wd)):
                w_out[...] = w
        g = _dot(h, wg)
        u = _dot(h, wu)
        a = (g * jax.nn.sigmoid(g)) * u * 0.5
        o_ref[...] += _dot(a.astype(BF16), wd)

    def finish():
        if emit_norm:
            y = o_ref[...]
            hn_ref[...] = (y * _rms_scale(y) * gout_ref[...]).astype(BF16)

    @pl.when(f == 0)
    def _():
        x = x_ref[...]
        h = (x * _rms_scale(x) * gin_ref[...]).astype(BF16)
        h_ref[...] = h
        o_ref[...] = x
        step(h)

    @pl.when((f > 0) & (f < last))
    def _():
        step(h_ref[...])

    @pl.when(f == last)
    def _():
        step(h_ref[...])
        finish()


def _ffn(x, g_in, wg, wu, wd, g_out, *, first_tile=0, n_tiles=None, partial=()):
    T = x.shape[0]
    tm = FFN_TM
    emit_norm = g_out is not None
    emit_weights = wg.dtype == F32
    tf = FFN_HEAD_TF if emit_weights else FFN_TF
    n_tiles = T // tm - first_tile if n_tiles is None else n_tiles
    mode = dict(pipeline_mode=pl.Buffered(1)) if n_tiles == 1 else {}
    tok = pl.BlockSpec((tm, D_MODEL), lambda i, f: (i + first_tile, 0), **mode)
    vec = pl.BlockSpec((1, D_MODEL), lambda i, f: (0, 0))
    w_up_spec = pl.BlockSpec((D_MODEL, tf), lambda i, f: (0, f))
    w_down_spec = pl.BlockSpec((tf, D_MODEL), lambda i, f: (f, 0))
    in_specs = [tok, vec, w_up_spec, w_up_spec, w_down_spec]
    args = [x, g_in, wg, wu, wd]
    out_specs = [tok]
    out_shape = [jax.ShapeDtypeStruct((T, D_MODEL), F32)]
    if emit_norm:
        in_specs.append(vec)
        args.append(g_out)
        out_specs.append(tok)
        out_shape.append(jax.ShapeDtypeStruct((T, D_MODEL), BF16))
    if emit_weights:
        out_specs += [w_up_spec, w_up_spec, w_down_spec]
        out_shape += [jax.ShapeDtypeStruct(w.shape, BF16) for w in (wg, wu, wd)]
    aliases = {len(args) + k: k for k in range(len(partial))}
    in_specs += [pl.BlockSpec(memory_space=pl.ANY)] * len(partial)
    args += list(partial)
    return pl.pallas_call(
        functools.partial(_ffn_kernel, emit_norm=emit_norm, emit_weights=emit_weights, n_aliased=len(partial)),
        grid=(n_tiles, D_FF // tf),
        in_specs=in_specs,
        out_specs=out_specs,
        out_shape=out_shape,
        scratch_shapes=[pltpu.VMEM((tm, D_MODEL), BF16)],
        input_output_aliases=aliases,
        compiler_params=_params("parallel", "arbitrary"),
        name="ffn_head" if emit_weights else "ffn",
    )(*args)


QUERY, KEY, PLAIN, GATE = "query", "key", "plain", "gate"


def _sigmoid(y):
    return 0.5 * jnp.tanh(0.5 * y) + 0.5


def _norm_rope(y, gain, cos, sin):
    y = y * _rms_scale(y) * gain
    lane = lax.broadcasted_iota(jnp.int32, y.shape, 1)
    partner = jnp.where(lane < ROPE_HALF, pltpu.roll(y, HEAD_DIM - ROPE_HALF, 1), pltpu.roll(y, ROPE_HALF, 1))
    return y * cos + partner * sin


def _proj_stages(h_ref, w_refs, gain_ref, cos_ref, sin_ref, o_ref, y_ref, slab, *, kinds, chunk, dilation):
    def compute(m, slot):
        h = h_ref[pl.ds(m * chunk, chunk), :]
        for p, w_ref in enumerate(w_refs):
            y = _dot(h, w_ref[...])
            for j in range(PROJ_PART_HEADS):
                y_ref[slot, p * PROJ_PART_HEADS + j] = y[:, j * HEAD_DIM:(j + 1) * HEAD_DIM]

    def finish(m, slot):
        rows = pl.ds(m * chunk, chunk)
        for hh, kind in enumerate(kinds):
            y = y_ref[slot, hh]
            if kind in (QUERY, KEY):
                gain = gain_ref[0:1, :] if kind == QUERY else gain_ref[1:2, :]
                y = _norm_rope(y, gain, cos_ref[rows, :], sin_ref[rows, :])
            elif kind == GATE:
                y = _sigmoid(y)
            if dilation is None:
                o_ref[rows, hh * HEAD_DIM:(hh + 1) * HEAD_DIM] = y.astype(BF16)
            else:
                sec, head = divmod(hh, DIL_HEADS)
                sec = DIL_SECTION_SLOT[sec]
                n = chunk // dilation
                dst = pl.ds(m * n, n)
                if dilation == 1:
                    o_ref[sec, head, 0, dst, :] = y.astype(BF16)
                elif dilation <= MAX_SINGLE_OP_STRIDE:
                    slab_ref = slab[0]
                    slab_ref[hh] = y
                    for c in range(dilation):
                        o_ref[sec, head, c, dst, :] = slab_ref[hh, pl.ds(c, n, stride=dilation), :].astype(BF16)
                else:
                    slab_ref, slab2_ref = slab
                    step = MAX_SINGLE_OP_STRIDE
                    assert dilation == step * step
                    n1 = chunk // step
                    slab_ref[hh] = y
                    for c_lo in range(step):
                        slab2_ref[hh, c_lo * n1:(c_lo + 1) * n1, :] = slab_ref[hh, pl.ds(c_lo, n1, stride=step), :]
                    for c_lo in range(step):
                        for c_hi in range(step):
                            o_ref[sec, head, c_lo + step * c_hi, dst, :] = slab2_ref[
                                hh, pl.ds(c_lo * n1 + c_hi, n, stride=step), :].astype(BF16)

    return compute, finish


def _proj_kernel(h_ref, *refs, kinds, chunk, dilation):
    n_parts = len(kinds) // PROJ_PART_HEADS
    gain_ref, cos_ref, sin_ref, o_ref, y_ref = refs[n_parts:n_parts + 5]
    compute, finish = _proj_stages(h_ref, refs[:n_parts], gain_ref, cos_ref, sin_ref, o_ref, y_ref,
                                   refs[n_parts + 5:], kinds=kinds, chunk=chunk, dilation=dilation)
    n_chunks = h_ref.shape[0] // chunk
    compute(0, 0)
    for m in range(n_chunks - 1):
        finish(m, m % 2)
        compute(m + 1, (m + 1) % 2)
    finish(n_chunks - 1, (n_chunks - 1) % 2)


def _weight_specs(parts):
    return [pl.BlockSpec((D_MODEL, PROJ_PART_WIDTH), functools.partial(lambda blk, i: (0, blk), blk),
                         pipeline_mode=pl.Buffered(1)) for blk in parts]


def _proj(h, w, gains, cos, sin, *, parts, kinds, tm, name, dil=None):
    T = h.shape[0]
    width = len(kinds) * HEAD_DIM
    assert width == len(parts) * PROJ_PART_WIDTH and w.shape[1] % PROJ_PART_WIDTH == 0
    chunk = PROJ_CHUNK
    scratch = [pltpu.VMEM((2, len(kinds), chunk, HEAD_DIM), F32)]
    if dil is None:
        dilation = None
        out_spec = pl.BlockSpec((tm, width), lambda i: (i, 0))
        out_shape = jax.ShapeDtypeStruct((T, width), BF16)
    else:
        dilation, batch, seq = dil
        n_i = seq // tm
        out_spec = pl.BlockSpec((3, None, DIL_HEADS, dilation, tm // dilation, HEAD_DIM),
                                lambda i: (0, i // n_i, 0, 0, i % n_i, 0))
        out_shape = jax.ShapeDtypeStruct((3, batch, DIL_HEADS, dilation, seq // dilation, HEAD_DIM), BF16)
        if dilation > 1:
            scratch.append(pltpu.VMEM((len(kinds), chunk, HEAD_DIM), F32))
        if dilation > MAX_SINGLE_OP_STRIDE:
            scratch.append(pltpu.VMEM((len(kinds), chunk, HEAD_DIM), F32))
    tok = lambda cols: pl.BlockSpec((tm, cols), lambda i: (i, 0))
    return pl.pallas_call(
        functools.partial(_proj_kernel, kinds=kinds, chunk=chunk, dilation=dilation),
        grid=(T // tm,),
        in_specs=[tok(D_MODEL), *_weight_specs(parts), pl.BlockSpec((2, HEAD_DIM), lambda i: (0, 0)),
                  tok(HEAD_DIM), tok(HEAD_DIM)],
        out_specs=out_spec,
        out_shape=out_shape,
        scratch_shapes=scratch,
        compiler_params=_params("parallel"),
        name=name,
    )(h, *([w] * len(parts)), gains, cos, sin)


def _dil_attn_stages(ins, o_ref, scr, i):
    steps = DIL_BLOCK // BAND

    row = lax.broadcasted_iota(jnp.int32, (BAND, 2 * BAND), 0)
    col = lax.broadcasted_iota(jnp.int32, (BAND, 2 * BAND), 1)
    band_bias = jnp.where((col >= row) & (col <= row + BAND), 0.0, NEG_BIG)
    start_bias = jnp.where(i == 0, jnp.where(col >= BAND, band_bias, NEG_BIG), band_bias)

    groups = []
    for g, r in enumerate(DILATIONS):
        cur_ref, prev_ref = ins[2 * g:2 * g + 2]
        q_ref, kc_ref, vc_ref = (cur_ref.at[DIL_SECTION_SLOT[sec]] for sec in range(3))
        kp_ref, vp_ref = (prev_ref.at[DIL_SECTION_SLOT[sec]] for sec in (1, 2))
        kx_ref, vx_ref, on_ref, ln_ref = scr[4 * g:4 * g + 4]
        kx_ref[:, :BAND, :] = kp_ref[...]
        kx_ref[:, BAND:, :] = kc_ref[...]
        vx_ref[:, :BAND, :HEAD_DIM] = vp_ref[...]
        vx_ref[:, BAND:, :HEAD_DIM] = vc_ref[...]
        vx_ref[:, :, HEAD_DIM:] = jnp.ones((r, vx_ref.shape[1], HEAD_DIM), BF16)
        groups.append((r, q_ref, kx_ref, vx_ref, on_ref, ln_ref))

    def tile(t):
        for r, q_ref, kx_ref, vx_ref, on_ref, ln_ref in groups:
            tiles = DIL_BLOCK // r // BAND
            c, mb = (0, t) if tiles == steps else (t, 0) if tiles == 1 else (t % r, t // r)
            m0 = mb * BAND
            q = q_ref[c, pl.ds(m0, BAND), :]
            k = kx_ref[c, pl.ds(m0, 2 * BAND), :]
            v = vx_ref[c, pl.ds(m0, 2 * BAND), :]
            s = _dot_nt(q, k) + (start_bias if mb == 0 else band_bias)
            m = jnp.max(s, axis=-1, keepdims=True)
            p = jnp.exp(s - m)
            od = _dot(p.astype(BF16), v)
            den = od[:, HEAD_DIM:]
            dst = pl.ds(mb * (BAND * r) + c, BAND, stride=r) if r > 1 else pl.ds(m0, BAND)
            on_ref[dst, :] = od[:, :HEAD_DIM] / den
            ln_ref[dst, :] = m + jnp.log(den)

    def mix():
        lses = [grp[5][...] for grp in groups]
        top = jnp.maximum(jnp.maximum(lses[0], lses[1]), lses[2])
        ws = [jnp.exp(l - top) for l in lses]
        num = ws[0] * groups[0][4][...] + ws[1] * groups[1][4][...] + ws[2] * groups[2][4][...]
        o_ref[...] = (num / (ws[0] + ws[1] + ws[2])).astype(BF16)

    return tile, mix


def _dil_attn_specs(unit):
    in_specs, scratch = [], []
    for r in DILATIONS:
        nq = DIL_BLOCK // r
        per = nq // BAND

        def cur(*ids):
            b, h, i = unit(*ids)
            return (0, b, h, 0, i, 0)

        def prev(*ids, per=per):
            b, h, i = unit(*ids)
            return (0, b, h, 0, jnp.maximum(i * per - 1, 0), 0)

        in_specs += [pl.BlockSpec((3, None, None, r, nq, HEAD_DIM), cur),
                     pl.BlockSpec((2, None, None, r, BAND, HEAD_DIM), prev)]
        scratch += [pltpu.VMEM((r, BAND + nq, HEAD_DIM), BF16), pltpu.VMEM((r, BAND + nq, 2 * HEAD_DIM), BF16),
                    pltpu.VMEM((DIL_BLOCK, HEAD_DIM), F32), pltpu.VMEM((DIL_BLOCK, HEAD_DIM), F32)]

    def out_index(*ids):
        b, h, i = unit(*ids)
        return (b, i, h)

    return in_specs, pl.BlockSpec((None, DIL_BLOCK, HEAD_DIM), out_index), scratch


def _gate_dil_kernel(h_ref, *refs, kinds, chunk, n_i):
    n_parts = len(kinds) // PROJ_PART_HEADS
    n_dil = 2 * N_DIL_GROUPS
    w_refs, dil_ins = refs[:n_parts], refs[n_parts:n_parts + n_dil]
    vg_ref, od_ref, y_ref = refs[n_parts + n_dil:n_parts + n_dil + 3]
    dil_scr = refs[n_parts + n_dil + 3:]
    compute, finish = _proj_stages(h_ref, w_refs, None, None, None, vg_ref, y_ref, (), kinds=kinds, chunk=chunk,
                                   dilation=None)
    tile, mix = _dil_attn_stages(dil_ins, od_ref, dil_scr, pl.program_id(0) % n_i)
    n_chunks = h_ref.shape[0] // chunk
    per_chunk = (DIL_BLOCK // BAND) // n_chunks
    compute(0, 0)
    for m in range(n_chunks):
        for t in range(m * per_chunk, (m + 1) * per_chunk):
            tile(t)
        finish(m, m % 2)
        if m + 1 < n_chunks:
            compute(m + 1, (m + 1) % 2)
    mix()


def _gate_dil(h, w, qkvs, *, parts, kinds, batch, seq):
    T = h.shape[0]
    tm, chunk = PROJ_GATE_TM, PROJ_CHUNK
    n_i = seq // DIL_BLOCK
    steps = T // tm
    assert steps == batch * DIL_HEADS * n_i
    width = len(kinds) * HEAD_DIM
    unit = lambda s: (s // (DIL_HEADS * n_i), (s // n_i) % DIL_HEADS, s % n_i)
    dil_specs, od_spec, dil_scratch = _dil_attn_specs(unit)
    dil_args = [qkvs[g] for g in range(N_DIL_GROUPS) for _ in range(2)]
    return pl.pallas_call(
        functools.partial(_gate_dil_kernel, kinds=kinds, chunk=chunk, n_i=n_i),
        grid=(steps,),
        in_specs=[pl.BlockSpec((tm, D_MODEL), lambda s: (s, 0)), *_weight_specs(parts), *dil_specs],
        out_specs=[pl.BlockSpec((tm, width), lambda s: (s, 0)), od_spec],
        out_shape=[jax.ShapeDtypeStruct((T, width), BF16),
                   jax.ShapeDtypeStruct((batch, seq, DIL_GROUP_WIDTH), BF16)],
        scratch_shapes=[pltpu.VMEM((2, len(kinds), chunk, HEAD_DIM), F32), *dil_scratch],
        compiler_params=_params("arbitrary"),
        name="gate_dil",
    )(h, *([w] * len(parts)), *dil_args)


def _diff_attn_kernel(lq1_ref, lk1_ref, lq2_ref, lk2_ref, q_ref, k_ref, v_ref, sub_ref, *refs, lam_init, n_casts):
    cast_in, (o_ref, *cast_out) = refs[:n_casts], refs[n_casts:2 * n_casts + 1]
    vt_ref, acc_ref, s0_ref, s1_ref = refs[2 * n_casts + 1:]
    for src_ref, dst_ref in zip(cast_in, cast_out):
        dst_ref[...] = src_ref[...].astype(BF16)

    qi = pl.program_id(2)
    tq = q_ref.shape[0]
    tk = vt_ref.shape[2]
    dv = DIFF_HEAD_WIDTH
    per_q = tq // tk

    @pl.when(qi == 0)
    def _():
        for ch in range(vt_ref.shape[0]):
            vt_ref[ch, :dv, :] = v_ref[ch * tk:(ch + 1) * tk, :].astype(F32).T.astype(BF16)
            vt_ref[ch, dv:, :] = jnp.ones((vt_ref.shape[1] - dv, tk), BF16)

    q = q_ref[...]
    acc_ref[...] = jnp.zeros(acc_ref.shape, F32)

    def scores(kb, s_ref):
        k = k_ref[pl.ds(pl.multiple_of(kb * tk, tk), tk), :]
        for c in range(2):
            lanes = slice(c * HEAD_DIM, (c + 1) * HEAD_DIM)
            s_ref[c] = _dot_nt(k[:, lanes], q[:, lanes])

    def update(kb, s_ref, maxes, qcols=slice(None), mask=None):
        vt = vt_ref[kb]
        out = []
        for c in range(2):
            s = s_ref[c, :, qcols]
            if mask is not None:
                s = jnp.where(mask, s, NEG_BIG)
            m_new = jnp.maximum(maxes[c], jnp.max(s, axis=0, keepdims=True))
            alpha = jnp.exp2(maxes[c] - m_new)
            p = jnp.exp2(s - m_new)
            out.append(m_new)
            acc_ref[c, :, qcols] = alpha * acc_ref[c, :, qcols] + _dot(vt, p.astype(BF16))
        return tuple(out)

    def pair(j, maxes):
        kb = per_q * j
        scores(kb + 1, s1_ref)
        maxes = update(kb, s0_ref, maxes)
        scores(kb + 2, s0_ref)
        return update(kb + 1, s1_ref, maxes)

    row0 = jnp.full((1, tq), NEG_BIG, F32)
    scores(0, s0_ref)
    maxes = lax.fori_loop(0, qi, pair, (row0, row0))

    kd = per_q * qi
    lo, hi = slice(0, tk), slice(tk, tq)
    key = lax.broadcasted_iota(jnp.int32, (tk, tk), 0)
    qry = lax.broadcasted_iota(jnp.int32, (tk, tk), 1)
    tri = key <= qry
    k_last = k_ref[pl.ds(pl.multiple_of((kd + 1) * tk, tk), tk), :]
    for c in range(2):
        lanes = slice(c * HEAD_DIM, (c + 1) * HEAD_DIM)
        s1_ref[c, :, hi] = _dot_nt(k_last[:, lanes], q[tk:, lanes])
    update(kd, s0_ref, tuple(m[:, lo] for m in maxes), lo, tri)
    m_hi = update(kd, s0_ref, tuple(m[:, hi] for m in maxes), hi)
    update(kd + 1, s1_ref, m_hi, hi, tri)

    lam = (jnp.exp(jnp.sum(lq1_ref[...] * lk1_ref[...], axis=-1, keepdims=True))
           - jnp.exp(jnp.sum(lq2_ref[...] * lk2_ref[...], axis=-1, keepdims=True)) + lam_init)
    o = (acc_ref[0, :dv] / acc_ref[0, dv:dv + 1] - lam * (acc_ref[1, :dv] / acc_ref[1, dv:dv + 1])).T
    o_ref[...] = (o * _rms_scale(o) * sub_ref[...] * (1.0 - lam_init)).astype(BF16)


def _diff_attn(qk, vg, lq1, lk1, lq2, lk2, subln, casts, *, batch, seq, lam_init):
    tq, tk = DIFF_TQ, DIFF_TK
    assert tq == 2 * tk
    q_blk, k_blk, v_blk = 0, DIFF_HEADS, 0
    n_q = seq // tq
    steps = batch * DIFF_HEADS * n_q
    vec = pl.BlockSpec((1, HEAD_DIM), lambda b, h, i: (0, 0))

    cast_specs = []
    for w in casts:
        share = steps
        while w.shape[0] % (share * BF16_SUBLANES):
            share //= 2
        cast_specs.append(pl.BlockSpec(
            (w.shape[0] // share, w.shape[1]),
            functools.partial(lambda rep, b, h, i: (((b * DIFF_HEADS + h) * n_q + i) // rep, 0), steps // share)))
    o, *cast = pl.pallas_call(
        functools.partial(_diff_attn_kernel, lam_init=lam_init, n_casts=len(casts)),
        grid=(batch, DIFF_HEADS, n_q),
        in_specs=[
            vec, vec, vec, vec,
            pl.BlockSpec((None, tq, DIFF_HEAD_WIDTH), lambda b, h, i: (b, i, q_blk + h)),
            pl.BlockSpec((None, seq, DIFF_HEAD_WIDTH), lambda b, h, i: (b, 0, k_blk + h)),
            pl.BlockSpec((None, seq, DIFF_HEAD_WIDTH), lambda b, h, i: (b, 0, v_blk + h)),
            pl.BlockSpec((1, DIFF_HEAD_WIDTH), lambda b, h, i: (0, 0)),
            *cast_specs,
        ],
        out_specs=[pl.BlockSpec((None, tq, DIFF_HEAD_WIDTH), lambda b, h, i: (b, i, h)), *cast_specs],
        out_shape=[jax.ShapeDtypeStruct((batch, seq, DIFF_WIDTH), BF16),
                   *(jax.ShapeDtypeStruct(w.shape, BF16) for w in casts)],
        scratch_shapes=[pltpu.VMEM((seq // tk, DIFF_HEAD_WIDTH + DIFF_ONES_ROWS, tk), BF16),
                        pltpu.VMEM((2, DIFF_HEAD_WIDTH + DIFF_ONES_ROWS, tq), F32),
                        pltpu.VMEM((2, tk, tq), F32),
                        pltpu.VMEM((2, tk, tq), F32)],
        compiler_params=_params("arbitrary", "arbitrary", "arbitrary"),
        name="diff_attn",
    )(lq1, lk1, lq2, lk2, qk, qk, vg, subln, *casts)
    return o, cast


def _merge_kernel(x_ref, od_ref, of_ref, gd0_ref, gd1_ref, gf0_ref, gf1_ref, wa_ref, wb_ref, wo_ref, o_ref):
    gd = jnp.concatenate([gd0_ref[...], gd1_ref[...]], axis=1).astype(F32)
    gf = jnp.concatenate([gf0_ref[...], gf1_ref[...]], axis=1).astype(F32)
    y = gd * _dot(od_ref[...], wa_ref[...]) + gf * _dot(of_ref[...], wb_ref[...])
    o_ref[...] = x_ref[...] + _dot(y.astype(BF16), wo_ref[...])


def _merge(x, o_dil, o_diff, vg, wa, wb, wo):
    T = x.shape[0]
    tm = MERGE_TM
    half = D_MODEL // 2

    def tok(width, blk=0):
        return pl.BlockSpec((tm, width), lambda i: (i, blk))

    def resident(shape):
        return pl.BlockSpec(shape, lambda i: (0, 0), pipeline_mode=pl.Buffered(1))

    gate0 = DIFF_WIDTH // half
    return pl.pallas_call(
        _merge_kernel,
        grid=(T // tm,),
        in_specs=[tok(D_MODEL), tok(DIL_GROUP_WIDTH), tok(DIFF_WIDTH),
                  tok(half, gate0), tok(half, gate0 + 1), tok(half, gate0 + 2), tok(half, gate0 + 3),
                  resident(wa.shape), resident(wb.shape), resident(wo.shape)],
        out_specs=tok(D_MODEL),
        out_shape=jax.ShapeDtypeStruct((T, D_MODEL), F32),
        compiler_params=_params("parallel"),
        name="merge",
    )(x, o_dil, o_diff, vg, vg, vg, vg, wa, wb, wo)


def _rope_tables(positions):
    lane = jnp.arange(HEAD_DIM)
    freq = ROPE_THETA ** (-(2 * (lane % ROPE_HALF)).astype(F32) / ROPE_DIM)
    inv = jnp.where(lane < ROPE_DIM, freq, 0.0)
    sign = jnp.where(lane < ROPE_HALF, -1.0, jnp.where(lane < ROPE_DIM, 1.0, 0.0)).astype(F32)
    ang = positions.astype(F32).reshape(-1, 1) * inv
    return jnp.cos(ang), jnp.sin(ang) * sign


def _layer(x, tables, layer, ffn1_norm, ffn1_w_gate, ffn1_w_up, ffn1_w_down, mix_norm, w_in,
           dil_q_norm, dil_k_norm, diff_q_norm, diff_k_norm, diff_lq1, diff_lk1, diff_lq2, diff_lk2,
           diff_subln, w_dil_branch, w_diff_branch, w_out, ffn2_norm, ffn2_w_gate, ffn2_w_up, ffn2_w_down,
           *, batch, seq):
    cos, sin = tables
    row = lambda v: v.reshape(1, -1).astype(F32)
    bf = lambda w: w.astype(BF16)
    qk_scale = HEAD_DIM ** -0.5
    lam_init = 0.8 - 0.6 * math.exp(-0.3 * layer)

    x1, h, wg1, wu1, wd1 = _ffn(x, row(ffn1_norm), ffn1_w_gate, ffn1_w_up, ffn1_w_down, row(mix_norm), n_tiles=1)
    x1, h = _ffn(x, row(ffn1_norm), wg1, wu1, wd1, row(mix_norm), first_tile=1, partial=(x1, h))
    w_in_bf = bf(w_in)

    blocks = lambda start, width: tuple(range(start // PROJ_PART_WIDTH, (start + width) // PROJ_PART_WIDTH))
    diff0 = 3 * DIL_WIDTH
    fv0 = diff0 + 2 * DIFF_WIDTH
    gains = lambda gq, gk: jnp.stack([gq * qk_scale, gk]).astype(F32)

    dil_kinds = (QUERY,) * DIL_HEADS + (KEY,) * DIL_HEADS + (PLAIN,) * DIL_HEADS
    qkvs = [_proj(h, w_in_bf, gains(dil_q_norm, dil_k_norm), cos, sin,
                  parts=tuple(sec * N_DIL_GROUPS + g for sec in range(3)),
                  kinds=dil_kinds, tm=PROJ_TM, name=f"dil_proj{g}", dil=(r, batch, seq))
            for g, r in enumerate(DILATIONS)]

    n_diff = DIFF_WIDTH // HEAD_DIM
    qk = _proj(h, w_in_bf, gains(diff_q_norm * LOG2_E, diff_k_norm), cos, sin, parts=blocks(diff0, 2 * DIFF_WIDTH),
               kinds=(QUERY,) * n_diff + (KEY,) * n_diff, tm=PROJ_TM, name="diff_qk_proj")
    n_gate = 2 * D_MODEL // HEAD_DIM
    vg, o_dil = _gate_dil(h, w_in_bf, qkvs, parts=blocks(fv0, DIFF_WIDTH + 2 * D_MODEL),
                          kinds=(PLAIN,) * n_diff + (GATE,) * n_gate, batch=batch, seq=seq)

    later = [w_dil_branch, w_diff_branch, w_out, ffn2_w_gate, ffn2_w_up, ffn2_w_down]
    o_diff, (wa, wb, wo, wg2, wu2, wd2) = _diff_attn(
        qk.reshape(batch, seq, -1), vg.reshape(batch, seq, -1), row(diff_lq1), row(diff_lk1), row(diff_lq2),
        row(diff_lk2), row(diff_subln), later, batch=batch, seq=seq, lam_init=lam_init)

    x2 = _merge(x1, o_dil.reshape(batch * seq, -1), o_diff.reshape(batch * seq, -1), vg, wa, wb, wo)
    (out,) = _ffn(x2, row(ffn2_norm), wg2, wu2, wd2, None)
    return out


def kernel(x, positions, ffn1_norm, ffn1_w_gate, ffn1_w_up, ffn1_w_down, mix_norm, w_in, dil_q_norm, dil_k_norm, diff_q_norm, diff_k_norm, diff_lq1, diff_lk1, diff_lq2, diff_lk2, diff_subln, w_dil_branch, w_diff_branch, w_out, ffn2_norm, ffn2_w_gate, ffn2_w_up, ffn2_w_down):
    batch, seq, d_model = x.shape
    assert d_model == D_MODEL and seq % DIL_BLOCK == 0
    weights = (ffn1_norm, ffn1_w_gate, ffn1_w_up, ffn1_w_down, mix_norm, w_in, dil_q_norm, dil_k_norm,
               diff_q_norm, diff_k_norm, diff_lq1, diff_lk1, diff_lq2, diff_lk2, diff_subln, w_dil_branch,
               w_diff_branch, w_out, ffn2_norm, ffn2_w_gate, ffn2_w_up, ffn2_w_down)
    tables = _rope_tables(positions)
    y = x.reshape(batch * seq, d_model)
    for layer in range(ffn1_norm.shape[0]):
        y = _layer(y, tables, layer, *(w[layer] for w in weights), batch=batch, seq=seq)
    return y.reshape(batch, seq, d_model)
```

```python
import functools
import math

import jax
import jax.numpy as jnp
from jax import lax
from jax.experimental import pallas as pl
from jax.experimental.pallas import tpu as pltpu

F32 = jnp.float32
BF16 = jnp.bfloat16

D_MODEL = 2048
D_FF = 5632
HEAD_DIM = 128
ROPE_DIM = HEAD_DIM // 4
ROPE_HALF = ROPE_DIM // 2
ROPE_THETA = 500000.0
RMS_EPS = 1e-6
BAND = 128
DILATIONS = (1, 4, 16)
N_DIL_GROUPS = len(DILATIONS)
DIL_HEADS = 4
DIL_GROUP_WIDTH = DIL_HEADS * HEAD_DIM
DIL_WIDTH = N_DIL_GROUPS * DIL_GROUP_WIDTH
DIFF_HEADS = 4
DIFF_HEAD_WIDTH = 2 * HEAD_DIM
DIFF_WIDTH = DIFF_HEADS * DIFF_HEAD_WIDTH
DIL_SECTION_SLOT = (2, 0, 1)
DIL_BLOCK = BAND * DILATIONS[-1]
LOG2_E = math.log2(math.e)
NEG_BIG = -1e30

VMEM_LIMIT_BYTES = 62 * 1024 * 1024

FFN_TM = 1024
FFN_TF = 512
FFN_HEAD_TF = 256
PROJ_TM = 1024
PROJ_GATE_TM = 512
PROJ_CHUNK = 256
PROJ_PART_HEADS = 4
PROJ_PART_WIDTH = PROJ_PART_HEADS * HEAD_DIM
MAX_SINGLE_OP_STRIDE = 4
MERGE_TM = 512
DIFF_TQ = 1024
DIFF_TK = 512
BF16_SUBLANES = 16
DIFF_ONES_ROWS = BF16_SUBLANES


def _params(*sem):
    return pltpu.CompilerParams(dimension_semantics=sem, vmem_limit_bytes=VMEM_LIMIT_BYTES)


def _rms_scale(x):
    return lax.rsqrt(jnp.mean(x * x, axis=-1, keepdims=True) + RMS_EPS)


def _dot(a, b):
    return jnp.dot(a, b, preferred_element_type=F32)


def _dot_nt(a, b):
    return lax.dot_general(a, b, (((1,), (1,)), ((), ())), preferred_element_type=F32)


def _ffn_kernel(*refs, emit_norm, emit_weights, n_head):
    x_ref, gin_ref, wg_ref, wu_ref, wd_ref = refs[:5]
    gout_ref = refs[5] if emit_norm else None
    head_refs = refs[5 + emit_norm:5 + emit_norm + n_head]
    outs = list(refs[5 + emit_norm + n_head:])
    o_ref = outs.pop(0)
    hn_ref = outs.pop(0) if emit_norm else None
    w_outs = [outs.pop(0) for _ in range(3)] if emit_weights else None
    (h_ref,) = outs
    i = pl.program_id(0)
    f = pl.program_id(1)
    last = pl.num_programs(1) - 1
    live = i >= (1 if n_head else 0)

    def step(h):
        wg, wu, wd = wg_ref[...], wu_ref[...], wd_ref[...]
        if emit_weights:
            wg, wu, wd = wg.astype(BF16), wu.astype(BF16), wd.astype(BF16)
            for w_out, w in zip(w_outs, (wg, wu, wd)):
                w_out[...] = w
        g = _dot(h, wg)
        u = _dot(h, wu)
        a = (g * jax.nn.sigmoid(g)) * u * 0.5
        o_ref[...] += _dot(a.astype(BF16), wd)

    def finish():
        if emit_norm:
            y = o_ref[...]
            hn_ref[...] = (y * _rms_scale(y) * gout_ref[...]).astype(BF16)

    @pl.when(live & (f == 0))
    def _():
        x = x_ref[...]
        h = (x * _rms_scale(x) * gin_ref[...]).astype(BF16)
        h_ref[...] = h
        o_ref[...] = x
        step(h)

    @pl.when(live & (f > 0) & (f < last))
    def _():
        step(h_ref[...])

    @pl.when(live & (f == last))
    def _():
        step(h_ref[...])
        finish()

    if n_head:
        @pl.when((i == 0) & (f == 0))
        def _():
            for src, dst in zip(head_refs, (o_ref, hn_ref)):
                pltpu.sync_copy(src, dst)


def _ffn(x, g_in, wg, wu, wd, g_out, *, head=()):
    T = x.shape[0]
    tm = FFN_TM
    emit_norm = g_out is not None
    emit_weights = wg.dtype == F32
    tf = FFN_HEAD_TF if emit_weights else FFN_TF
    n_tiles = 1 if emit_weights else T // tm
    skip = 1 if head else 0
    mode = dict(pipeline_mode=pl.Buffered(1)) if n_tiles == 1 else {}
    tok = pl.BlockSpec((tm, D_MODEL), lambda i, f: (i, 0), **mode)
    vec = pl.BlockSpec((1, D_MODEL), lambda i, f: (0, 0))
    chunk = lambda i, f: jnp.where(i < skip, 0, f)
    w_up_spec = pl.BlockSpec((D_MODEL, tf), lambda i, f: (0, chunk(i, f)))
    w_down_spec = pl.BlockSpec((tf, D_MODEL), lambda i, f: (chunk(i, f), 0))
    in_specs = [tok, vec, w_up_spec, w_up_spec, w_down_spec]
    args = [x, g_in, wg, wu, wd]
    out_specs = [tok]
    out_shape = [jax.ShapeDtypeStruct((n_tiles * tm, D_MODEL), F32)]
    if emit_norm:
        in_specs.append(vec)
        args.append(g_out)
        out_specs.append(tok)
        out_shape.append(jax.ShapeDtypeStruct((n_tiles * tm, D_MODEL), BF16))
    if emit_weights:
        out_specs += [w_up_spec, w_up_spec, w_down_spec]
        out_shape += [jax.ShapeDtypeStruct(w.shape, BF16) for w in (wg, wu, wd)]
    in_specs += [pl.BlockSpec(memory_space=pl.ANY)] * len(head)
    args += list(head)
    return pl.pallas_call(
        functools.partial(_ffn_kernel, emit_norm=emit_norm, emit_weights=emit_weights, n_head=len(head)),
        grid=(n_tiles, D_FF // tf),
        in_specs=in_specs,
        out_specs=out_specs,
        out_shape=out_shape,
        scratch_shapes=[pltpu.VMEM((tm, D_MODEL), BF16)],
        compiler_params=_params("arbitrary", "arbitrary"),
        name="ffn_head" if emit_weights else "ffn",
    )(*args)


QUERY, KEY, PLAIN, GATE = "query", "key", "plain", "gate"


def _sigmoid(y):
    return 0.5 * jnp.tanh(0.5 * y) + 0.5


def _norm_rope(y, gain, cos, sin):
    y = y * _rms_scale(y) * gain
    lane = lax.broadcasted_iota(jnp.int32, y.shape, 1)
    partner = jnp.where(lane < ROPE_HALF, pltpu.roll(y, HEAD_DIM - ROPE_HALF, 1), pltpu.roll(y, ROPE_HALF, 1))
    return y * cos + partner * sin


def _proj_stages(h_ref, w_refs, gain_ref, cos_ref, sin_ref, o_ref, y_ref, slab, *, kinds, chunk, dilation):
    def compute(m, slot):
        h = h_ref[pl.ds(m * chunk, chunk), :]
        for p, w_ref in enumerate(w_refs):
            y = _dot(h, w_ref[...])
            for j in range(PROJ_PART_HEADS):
                y_ref[slot, p * PROJ_PART_HEADS + j] = y[:, j * HEAD_DIM:(j + 1) * HEAD_DIM]

    def finish(m, slot):
        rows = pl.ds(m * chunk, chunk)
        for hh, kind in enumerate(kinds):
            y = y_ref[slot, hh]
            if kind in (QUERY, KEY):
                gain = gain_ref[0:1, :] if kind == QUERY else gain_ref[1:2, :]
                y = _norm_rope(y, gain, cos_ref[rows, :], sin_ref[rows, :])
            elif kind == GATE:
                y = _sigmoid(y)
            if dilation is None:
                o_ref[rows, hh * HEAD_DIM:(hh + 1) * HEAD_DIM] = y.astype(BF16)
            else:
                sec, head = divmod(hh, DIL_HEADS)
                sec = DIL_SECTION_SLOT[sec]
                n = chunk // dilation
                dst = pl.ds(m * n, n)
                if dilation == 1:
                    o_ref[sec, head, 0, dst, :] = y.astype(BF16)
                elif dilation <= MAX_SINGLE_OP_STRIDE:
                    slab_ref = slab[0]
                    slab_ref[hh] = y
                    for c in range(dilation):
                        o_ref[sec, head, c, dst, :] = slab_ref[hh, pl.ds(c, n, stride=dilation), :].astype(BF16)
                else:
                    slab_ref, slab2_ref = slab
                    step = MAX_SINGLE_OP_STRIDE
                    assert dilation == step * step
                    n1 = chunk // step
                    slab_ref[hh] = y
                    for c_lo in range(step):
                        slab2_ref[hh, c_lo * n1:(c_lo + 1) * n1, :] = slab_ref[hh, pl.ds(c_lo, n1, stride=step), :]
                    for c_lo in range(step):
                        for c_hi in range(step):
                            o_ref[sec, head, c_lo + step * c_hi, dst, :] = slab2_ref[
                                hh, pl.ds(c_lo * n1 + c_hi, n, stride=step), :].astype(BF16)

    return compute, finish


def _proj_kernel(h_ref, *refs, kinds, chunk, dilation):
    n_parts = len(kinds) // PROJ_PART_HEADS
    gain_ref, cos_ref, sin_ref, o_ref, y_ref = refs[n_parts:n_parts + 5]
    compute, finish = _proj_stages(h_ref, refs[:n_parts], gain_ref, cos_ref, sin_ref, o_ref, y_ref,
                                   refs[n_parts + 5:], kinds=kinds, chunk=chunk, dilation=dilation)
    n_chunks = h_ref.shape[0] // chunk
    compute(0, 0)
    for m in range(n_chunks - 1):
        finish(m, m % 2)
        compute(m + 1, (m + 1) % 2)
    finish(n_chunks - 1, (n_chunks - 1) % 2)


def _weight_specs(parts):
    return [pl.BlockSpec((D_MODEL, PROJ_PART_WIDTH), functools.partial(lambda blk, i: (0, blk), blk),
                         pipeline_mode=pl.Buffered(1)) for blk in parts]


def _proj(h, w, gains, cos, sin, *, parts, kinds, tm, name, dil=None):
    T = h.shape[0]
    width = len(kinds) * HEAD_DIM
    assert width == len(parts) * PROJ_PART_WIDTH and w.shape[1] % PROJ_PART_WIDTH == 0
    chunk = PROJ_CHUNK
    scratch = [pltpu.VMEM((2, len(kinds), chunk, HEAD_DIM), F32)]
    if dil is None:
        dilation = None
        out_spec = pl.BlockSpec((tm, width), lambda i: (i, 0))
        out_shape = jax.ShapeDtypeStruct((T, width), BF16)
    else:
        dilation, batch, seq = dil
        n_i = seq // tm
        out_spec = pl.BlockSpec((3, None, DIL_HEADS, dilation, tm // dilation, HEAD_DIM),
                                lambda i: (0, i // n_i, 0, 0, i % n_i, 0))
        out_shape = jax.ShapeDtypeStruct((3, batch, DIL_HEADS, dilation, seq // dilation, HEAD_DIM), BF16)
        if dilation > 1:
            scratch.append(pltpu.VMEM((len(kinds), chunk, HEAD_DIM), F32))
        if dilation > MAX_SINGLE_OP_STRIDE:
            scratch.append(pltpu.VMEM((len(kinds), chunk, HEAD_DIM), F32))
    tok = lambda cols: pl.BlockSpec((tm, cols), lambda i: (i, 0))
    return pl.pallas_call(
        functools.partial(_proj_kernel, kinds=kinds, chunk=chunk, dilation=dilation),
        grid=(T // tm,),
        in_specs=[tok(D_MODEL), *_weight_specs(parts), pl.BlockSpec((2, HEAD_DIM), lambda i: (0, 0)),
                  tok(HEAD_DIM), tok(HEAD_DIM)],
        out_specs=out_spec,
        out_shape=out_shape,
        scratch_shapes=scratch,
        compiler_params=_params("parallel"),
        name=name,
    )(h, *([w] * len(parts)), gains, cos, sin)


def _dil_attn_stages(ins, o_ref, scr, i):
    steps = DIL_BLOCK // BAND

    row = lax.broadcasted_iota(jnp.int32, (BAND, 2 * BAND), 0)
    col = lax.broadcasted_iota(jnp.int32, (BAND, 2 * BAND), 1)
    band_bias = jnp.where((col >= row) & (col <= row + BAND), 0.0, NEG_BIG)
    start_bias = jnp.where(i == 0, jnp.where(col >= BAND, band_bias, NEG_BIG), band_bias)

    groups = []
    for g, r in enumerate(DILATIONS):
        cur_ref, prev_ref = ins[2 * g:2 * g + 2]
        q_ref, kc_ref, vc_ref = (cur_ref.at[DIL_SECTION_SLOT[sec]] for sec in range(3))
        kp_ref, vp_ref = (prev_ref.at[DIL_SECTION_SLOT[sec]] for sec in (1, 2))
        kx_ref, vx_ref, on_ref, ln_ref = scr[4 * g:4 * g + 4]
        kx_ref[:, :BAND, :] = kp_ref[...]
        kx_ref[:, BAND:, :] = kc_ref[...]
        vx_ref[:, :BAND, :HEAD_DIM] = vp_ref[...]
        vx_ref[:, BAND:, :HEAD_DIM] = vc_ref[...]
        vx_ref[:, :, HEAD_DIM:] = jnp.ones((r, vx_ref.shape[1], HEAD_DIM), BF16)
        groups.append((r, q_ref, kx_ref, vx_ref, on_ref, ln_ref))

    def tile(t):
        for r, q_ref, kx_ref, vx_ref, on_ref, ln_ref in groups:
            tiles = DIL_BLOCK // r // BAND
            c, mb = (0, t) if tiles == steps else (t, 0) if tiles == 1 else (t % r, t // r)
            m0 = mb * BAND
            q = q_ref[c, pl.ds(m0, BAND), :]
            k = kx_ref[c, pl.ds(m0, 2 * BAND), :]
            v = vx_ref[c, pl.ds(m0, 2 * BAND), :]
            s = _dot_nt(q, k) + (start_bias if mb == 0 else band_bias)
            m = jnp.max(s, axis=-1, keepdims=True)
            p = jnp.exp(s - m)
            od = _dot(p.astype(BF16), v)
            den = od[:, HEAD_DIM:]
            dst = pl.ds(mb * (BAND * r) + c, BAND, stride=r) if r > 1 else pl.ds(m0, BAND)
            on_ref[dst, :] = od[:, :HEAD_DIM] / den
            ln_ref[dst, :] = m + jnp.log(den)

    def mix():
        lses = [grp[5][...] for grp in groups]
        top = jnp.maximum(jnp.maximum(lses[0], lses[1]), lses[2])
        ws = [jnp.exp(l - top) for l in lses]
        num = ws[0] * groups[0][4][...] + ws[1] * groups[1][4][...] + ws[2] * groups[2][4][...]
        o_ref[...] = (num / (ws[0] + ws[1] + ws[2])).astype(BF16)

    return tile, mix


def _dil_attn_specs(unit):
    in_specs, scratch = [], []
    for r in DILATIONS:
        nq = DIL_BLOCK // r
        per = nq // BAND

        def cur(*ids):
            b, h, i = unit(*ids)
            return (0, b, h, 0, i, 0)

        def prev(*ids, per=per):
            b, h, i = unit(*ids)
            return (0, b, h, 0, jnp.maximum(i * per - 1, 0), 0)

        in_specs += [pl.BlockSpec((3, None, None, r, nq, HEAD_DIM), cur),
                     pl.BlockSpec((2, None, None, r, BAND, HEAD_DIM), prev)]
        scratch += [pltpu.VMEM((r, BAND + nq, HEAD_DIM), BF16), pltpu.VMEM((r, BAND + nq, 2 * HEAD_DIM), BF16),
                    pltpu.VMEM((DIL_BLOCK, HEAD_DIM), F32), pltpu.VMEM((DIL_BLOCK, HEAD_DIM), F32)]

    def out_index(*ids):
        b, h, i = unit(*ids)
        return (b, i, h)

    return in_specs, pl.BlockSpec((None, DIL_BLOCK, HEAD_DIM), out_index), scratch


def _gate_dil_kernel(h_ref, *refs, kinds, chunk, n_i):
    n_parts = len(kinds) // PROJ_PART_HEADS
    n_dil = 2 * N_DIL_GROUPS
    w_refs, dil_ins = refs[:n_parts], refs[n_parts:n_parts + n_dil]
    vg_ref, od_ref, y_ref = refs[n_parts + n_dil:n_parts + n_dil + 3]
    dil_scr = refs[n_parts + n_dil + 3:]
    compute, finish = _proj_stages(h_ref, w_refs, None, None, None, vg_ref, y_ref, (), kinds=kinds, chunk=chunk,
                                   dilation=None)
    tile, mix = _dil_attn_stages(dil_ins, od_ref, dil_scr, pl.program_id(0) % n_i)
    n_chunks = h_ref.shape[0] // chunk
    per_chunk = (DIL_BLOCK // BAND) // n_chunks
    compute(0, 0)
    for m in range(n_chunks):
        for t in range(m * per_chunk, (m + 1) * per_chunk):
            tile(t)
        finish(m, m % 2)
        if m + 1 < n_chunks:
            compute(m + 1, (m + 1) % 2)
    mix()


def _gate_dil(h, w, qkvs, *, parts, kinds, batch, seq):
    T = h.shape[0]
    tm, chunk = PROJ_GATE_TM, PROJ_CHUNK
    n_i = seq // DIL_BLOCK
    steps = T // tm
    assert steps == batch * DIL_HEADS * n_i
    width = len(kinds) * HEAD_DIM
    unit = lambda s: (s // (DIL_HEADS * n_i), (s // n_i) % DIL_HEADS, s % n_i)
    dil_specs, od_spec, dil_scratch = _dil_attn_specs(unit)
    dil_args = [qkvs[g] for g in range(N_DIL_GROUPS) for _ in range(2)]
    return pl.pallas_call(
        functools.partial(_gate_dil_kernel, kinds=kinds, chunk=chunk, n_i=n_i),
        grid=(steps,),
        in_specs=[pl.BlockSpec((tm, D_MODEL), lambda s: (s, 0)), *_weight_specs(parts), *dil_specs],
        out_specs=[pl.BlockSpec((tm, width), lambda s: (s, 0)), od_spec],
        out_shape=[jax.ShapeDtypeStruct((T, width), BF16),
                   jax.ShapeDtypeStruct((batch, seq, DIL_GROUP_WIDTH), BF16)],
        scratch_shapes=[pltpu.VMEM((2, len(kinds), chunk, HEAD_DIM), F32), *dil_scratch],
        compiler_params=_params("arbitrary"),
        name="gate_dil",
    )(h, *([w] * len(parts)), *dil_args)


def _diff_attn_kernel(lq1_ref, lk1_ref, lq2_ref, lk2_ref, q_ref, k_ref, v_ref, sub_ref, *refs, lam_init, n_casts):
    cast_in, (o_ref, *cast_out) = refs[:n_casts], refs[n_casts:2 * n_casts + 1]
    vt_ref, acc_ref, s0_ref, s1_ref = refs[2 * n_casts + 1:]
    for src_ref, dst_ref in zip(cast_in, cast_out):
        dst_ref[...] = src_ref[...].astype(BF16)

    qi = pl.program_id(2)
    tq = q_ref.shape[0]
    tk = vt_ref.shape[2]
    dv = DIFF_HEAD_WIDTH
    per_q = tq // tk

    @pl.when(qi == 0)
    def _():
        for ch in range(vt_ref.shape[0]):
            vt_ref[ch, :dv, :] = v_ref[ch * tk:(ch + 1) * tk, :].astype(F32).T.astype(BF16)
            vt_ref[ch, dv:, :] = jnp.ones((vt_ref.shape[1] - dv, tk), BF16)

    q = q_ref[...]
    acc_ref[...] = jnp.zeros(acc_ref.shape, F32)

    def scores(kb, s_ref):
        k = k_ref[pl.ds(pl.multiple_of(kb * tk, tk), tk), :]
        for c in range(2):
            lanes = slice(c * HEAD_DIM, (c + 1) * HEAD_DIM)
            s_ref[c] = _dot_nt(k[:, lanes], q[:, lanes])

    def update(kb, s_ref, maxes, qcols=slice(None), mask=None):
        vt = vt_ref[kb]
        out = []
        for c in range(2):
            s = s_ref[c, :, qcols]
            if mask is not None:
                s = jnp.where(mask, s, NEG_BIG)
            m_new = jnp.maximum(maxes[c], jnp.max(s, axis=0, keepdims=True))
            alpha = jnp.exp2(maxes[c] - m_new)
            p = jnp.exp2(s - m_new)
            out.append(m_new)
            acc_ref[c, :, qcols] = alpha * acc_ref[c, :, qcols] + _dot(vt, p.astype(BF16))
        return tuple(out)

    def pair(j, maxes):
        kb = per_q * j
        scores(kb + 1, s1_ref)
        maxes = update(kb, s0_ref, maxes)
        scores(kb + 2, s0_ref)
        return update(kb + 1, s1_ref, maxes)

    row0 = jnp.full((1, tq), NEG_BIG, F32)
    scores(0, s0_ref)
    maxes = lax.fori_loop(0, qi, pair, (row0, row0))

    kd = per_q * qi
    lo, hi = slice(0, tk), slice(tk, tq)
    key = lax.broadcasted_iota(jnp.int32, (tk, tk), 0)
    qry = lax.broadcasted_iota(jnp.int32, (tk, tk), 1)
    tri = key <= qry
    k_last = k_ref[pl.ds(pl.multiple_of((kd + 1) * tk, tk), tk), :]
    for c in range(2):
        lanes = slice(c * HEAD_DIM, (c + 1) * HEAD_DIM)
        s1_ref[c, :, hi] = _dot_nt(k_last[:, lanes], q[tk:, lanes])
    update(kd, s0_ref, tuple(m[:, lo] for m in maxes), lo, tri)
    m_hi = update(kd, s0_ref, tuple(m[:, hi] for m in maxes), hi)
    update(kd + 1, s1_ref, m_hi, hi, tri)

    lam = (jnp.exp(jnp.sum(lq1_ref[...] * lk1_ref[...], axis=-1, keepdims=True))
           - jnp.exp(jnp.sum(lq2_ref[...] * lk2_ref[...], axis=-1, keepdims=True)) + lam_init)
    o = (acc_ref[0, :dv] / acc_ref[0, dv:dv + 1] - lam * (acc_ref[1, :dv] / acc_ref[1, dv:dv + 1])).T
    o_ref[...] = (o * _rms_scale(o) * sub_ref[...] * (1.0 - lam_init)).astype(BF16)


def _diff_attn(qk, vg, lq1, lk1, lq2, lk2, subln, casts, *, batch, seq, lam_init):
    tq, tk = DIFF_TQ, DIFF_TK
    assert tq == 2 * tk
    q_blk, k_blk, v_blk = 0, DIFF_HEADS, 0
    n_q = seq // tq
    steps = batch * DIFF_HEADS * n_q
    vec = pl.BlockSpec((1, HEAD_DIM), lambda b, h, i: (0, 0))

    cast_specs = []
    for w in casts:
        share = steps
        while w.shape[0] % (share * BF16_SUBLANES):
            share //= 2
        cast_specs.append(pl.BlockSpec(
            (w.shape[0] // share, w.shape[1]),
            functools.partial(lambda rep, b, h, i: (((b * DIFF_HEADS + h) * n_q + i) // rep, 0), steps // share)))
    o, *cast = pl.pallas_call(
        functools.partial(_diff_attn_kernel, lam_init=lam_init, n_casts=len(casts)),
        grid=(batch, DIFF_HEADS, n_q),
        in_specs=[
            vec, vec, vec, vec,
            pl.BlockSpec((None, tq, DIFF_HEAD_WIDTH), lambda b, h, i: (b, i, q_blk + h)),
            pl.BlockSpec((None, seq, DIFF_HEAD_WIDTH), lambda b, h, i: (b, 0, k_blk + h)),
            pl.BlockSpec((None, seq, DIFF_HEAD_WIDTH), lambda b, h, i: (b, 0, v_blk + h)),
            pl.BlockSpec((1, DIFF_HEAD_WIDTH), lambda b, h, i: (0, 0)),
            *cast_specs,
        ],
        out_specs=[pl.BlockSpec((None, tq, DIFF_HEAD_WIDTH), lambda b, h, i: (b, i, h)), *cast_specs],
        out_shape=[jax.ShapeDtypeStruct((batch, seq, DIFF_WIDTH), BF16),
                   *(jax.ShapeDtypeStruct(w.shape, BF16) for w in casts)],
        scratch_shapes=[pltpu.VMEM((seq // tk, DIFF_HEAD_WIDTH + DIFF_ONES_ROWS, tk), BF16),
                        pltpu.VMEM((2, DIFF_HEAD_WIDTH + DIFF_ONES_ROWS, tq), F32),
                        pltpu.VMEM((2, tk, tq), F32),
                        pltpu.VMEM((2, tk, tq), F32)],
        compiler_params=_params("arbitrary", "arbitrary", "arbitrary"),
        name="diff_attn",
    )(lq1, lk1, lq2, lk2, qk, qk, vg, subln, *casts)
    return o, cast


def _merge_kernel(x_ref, od_ref, of_ref, gd0_ref, gd1_ref, gf0_ref, gf1_ref, wa_ref, wb_ref, wo_ref, o_ref):
    gd = jnp.concatenate([gd0_ref[...], gd1_ref[...]], axis=1).astype(F32)
    gf = jnp.concatenate([gf0_ref[...], gf1_ref[...]], axis=1).astype(F32)
    y = gd * _dot(od_ref[...], wa_ref[...]) + gf * _dot(of_ref[...], wb_ref[...])
    o_ref[...] = x_ref[...] + _dot(y.astype(BF16), wo_ref[...])


def _merge(x, o_dil, o_diff, vg, wa, wb, wo):
    T = x.shape[0]
    tm = MERGE_TM
    half = D_MODEL // 2

    def tok(width, blk=0):
        return pl.BlockSpec((tm, width), lambda i: (i, blk))

    def resident(shape):
        return pl.BlockSpec(shape, lambda i: (0, 0), pipeline_mode=pl.Buffered(1))

    gate0 = DIFF_WIDTH // half
    return pl.pallas_call(
        _merge_kernel,
        grid=(T // tm,),
        in_specs=[tok(D_MODEL), tok(DIL_GROUP_WIDTH), tok(DIFF_WIDTH),
                  tok(half, gate0), tok(half, gate0 + 1), tok(half, gate0 + 2), tok(half, gate0 + 3),
                  resident(wa.shape), resident(wb.shape), resident(wo.shape)],
        out_specs=tok(D_MODEL),
        out_shape=jax.ShapeDtypeStruct((T, D_MODEL), F32),
        compiler_params=_params("parallel"),
        name="merge",
    )(x, o_dil, o_diff, vg, vg, vg, vg, wa, wb, wo)


def _rope_tables(positions):
    lane = jnp.arange(HEAD_DIM)
    freq = ROPE_THETA ** (-(2 * (lane % ROPE_HALF)).astype(F32) / ROPE_DIM)
    inv = jnp.where(lane < ROPE_DIM, freq, 0.0)
    sign = jnp.where(lane < ROPE_HALF, -1.0, jnp.where(lane < ROPE_DIM, 1.0, 0.0)).astype(F32)
    ang = positions.astype(F32).reshape(-1, 1) * inv
    return jnp.cos(ang), jnp.sin(ang) * sign


def _layer(x, tables, layer, ffn1_norm, ffn1_w_gate, ffn1_w_up, ffn1_w_down, mix_norm, w_in,
           dil_q_norm, dil_k_norm, diff_q_norm, diff_k_norm, diff_lq1, diff_lk1, diff_lq2, diff_lk2,
           diff_subln, w_dil_branch, w_diff_branch, w_out, ffn2_norm, ffn2_w_gate, ffn2_w_up, ffn2_w_down,
           *, batch, seq):
    cos, sin = tables
    row = lambda v: v.reshape(1, -1).astype(F32)
    bf = lambda w: w.astype(BF16)
    qk_scale = HEAD_DIM ** -0.5
    lam_init = 0.8 - 0.6 * math.exp(-0.3 * layer)

    x1_0, h_0, wg1, wu1, wd1 = _ffn(x, row(ffn1_norm), ffn1_w_gate, ffn1_w_up, ffn1_w_down, row(mix_norm))
    x1, h = _ffn(x, row(ffn1_norm), wg1, wu1, wd1, row(mix_norm), head=(x1_0, h_0))
    w_in_bf = bf(w_in)

    blocks = lambda start, width: tuple(range(start // PROJ_PART_WIDTH, (start + width) // PROJ_PART_WIDTH))
    diff0 = 3 * DIL_WIDTH
    fv0 = diff0 + 2 * DIFF_WIDTH
    gains = lambda gq, gk: jnp.stack([gq * qk_scale, gk]).astype(F32)

    dil_kinds = (QUERY,) * DIL_HEADS + (KEY,) * DIL_HEADS + (PLAIN,) * DIL_HEADS
    qkvs = [_proj(h, w_in_bf, gains(dil_q_norm, dil_k_norm), cos, sin,
                  parts=tuple(sec * N_DIL_GROUPS + g for sec in range(3)),
                  kinds=dil_kinds, tm=PROJ_TM, name=f"dil_proj{g}", dil=(r, batch, seq))
            for g, r in enumerate(DILATIONS)]

    n_diff = DIFF_WIDTH // HEAD_DIM
    qk = _proj(h, w_in_bf, gains(diff_q_norm * LOG2_E, diff_k_norm), cos, sin, parts=blocks(diff0, 2 * DIFF_WIDTH),
               kinds=(QUERY,) * n_diff + (KEY,) * n_diff, tm=PROJ_TM, name="diff_qk_proj")
    n_gate = 2 * D_MODEL // HEAD_DIM
    vg, o_dil = _gate_dil(h, w_in_bf, qkvs, parts=blocks(fv0, DIFF_WIDTH + 2 * D_MODEL),
                          kinds=(PLAIN,) * n_diff + (GATE,) * n_gate, batch=batch, seq=seq)

    later = [w_dil_branch, w_diff_branch, w_out, ffn2_w_gate, ffn2_w_up, ffn2_w_down]
    o_diff, (wa, wb, wo, wg2, wu2, wd2) = _diff_attn(
        qk.reshape(batch, seq, -1), vg.reshape(batch, seq, -1), row(diff_lq1), row(diff_lk1), row(diff_lq2),
        row(diff_lk2), row(diff_subln), later, batch=batch, seq=seq, lam_init=lam_init)

    x2 = _merge(x1, o_dil.reshape(batch * seq, -1), o_diff.reshape(batch * seq, -1), vg, wa, wb, wo)
    (out,) = _ffn(x2, row(ffn2_norm), wg2, wu2, wd2, None)
    return out


def kernel(x, positions, ffn1_norm, ffn1_w_gate, ffn1_w_up, ffn1_w_down, mix_norm, w_in, dil_q_norm, dil_k_norm, diff_q_norm, diff_k_norm, diff_lq1, diff_lk1, diff_lq2, diff_lk2, diff_subln, w_dil_branch, w_diff_branch, w_out, ffn2_norm, ffn2_w_gate, ffn2_w_up, ffn2_w_down):
    batch, seq, d_model = x.shape
    assert d_model == D_MODEL and seq % DIL_BLOCK == 0
    weights = (ffn1_norm, ffn1_w_gate, ffn1_w_up, ffn1_w_down, mix_norm, w_in, dil_q_norm, dil_k_norm,
               diff_q_norm, diff_k_norm, diff_lq1, diff_lk1, diff_lq2, diff_lk2, diff_subln, w_dil_branch,
               w_diff_branch, w_out, ffn2_norm, ffn2_w_gate, ffn2_w_up, ffn2_w_down)
    tables = _rope_tables(positions)
    y = x.reshape(batch * seq, d_model)
    for layer in range(ffn1_norm.shape[0]):
        y = _layer(y, tables, layer, *(w[layer] for w in weights), batch=batch, seq=seq)
    return y.reshape(batch, seq, d_model)
```

```python
import functools
import math

import jax
import jax.numpy as jnp
from jax import lax
from jax.experimental import pallas as pl
from jax.experimental.pallas import tpu as pltpu

F32 = jnp.float32
BF16 = jnp.bfloat16

D_MODEL = 2048
D_FF = 5632
HEAD_DIM = 128
ROPE_DIM = HEAD_DIM // 4
ROPE_HALF = ROPE_DIM // 2
ROPE_THETA = 500000.0
RMS_EPS = 1e-6
BAND = 128
DILATIONS = (1, 4, 16)
N_DIL_GROUPS = len(DILATIONS)
DIL_HEADS = 4
DIL_GROUP_WIDTH = DIL_HEADS * HEAD_DIM
DIL_WIDTH = N_DIL_GROUPS * DIL_GROUP_WIDTH
DIFF_HEADS = 4
DIFF_HEAD_WIDTH = 2 * HEAD_DIM
DIFF_WIDTH = DIFF_HEADS * DIFF_HEAD_WIDTH
DIL_SECTION_SLOT = (2, 0, 1)
DIL_BLOCK = BAND * DILATIONS[-1]
LOG2_E = math.log2(math.e)
NEG_BIG = -1e30

VMEM_LIMIT_BYTES = 62 * 1024 * 1024

FFN_TM = 1024
FFN_TF = 512
FFN_HEAD_TF = 256
PROJ_TM = 1024
PROJ_GATE_TM = 512
PROJ_CHUNK = 256
PROJ_PART_HEADS = 4
PROJ_PART_WIDTH = PROJ_PART_HEADS * HEAD_DIM
MAX_SINGLE_OP_STRIDE = 4
MERGE_TM = 512
DIFF_TQ = 1024
DIFF_TK = 512
BF16_SUBLANES = 16
DIFF_ONES_ROWS = BF16_SUBLANES


def _params(*sem):
    return pltpu.CompilerParams(dimension_semantics=sem, vmem_limit_bytes=VMEM_LIMIT_BYTES)


def _rms_scale(x):
    return lax.rsqrt(jnp.mean(x * x, axis=-1, keepdims=True) + RMS_EPS)


def _dot(a, b):
    return jnp.dot(a, b, preferred_element_type=F32)


def _dot_nt(a, b):
    return lax.dot_general(a, b, (((1,), (1,)), ((), ())), preferred_element_type=F32)


def _ffn_kernel(*refs, emit_norm, emit_weights, n_head):
    x_ref, gin_ref, wg_ref, wu_ref, wd_ref = refs[:5]
    gout_ref = refs[5] if emit_norm else None
    head_refs = refs[5 + emit_norm:5 + emit_norm + n_head]
    outs = list(refs[5 + emit_norm + n_head:])
    o_ref = outs.pop(0)
    hn_ref = outs.pop(0) if emit_norm else None
    w_outs = [outs.pop(0) for _ in range(3)] if emit_weights else None
    (h_ref,) = outs
    i = pl.program_id(0)
    f = pl.program_id(1)
    last = pl.num_programs(1) - 1
    live = i >= (1 if n_head else 0)

    def step(h):
        wg, wu, wd = wg_ref[...], wu_ref[...], wd_ref[...]
        if emit_weights:
            wg, wu, wd = wg.astype(BF16), wu.astype(BF16), wd.astype(BF16)
            for w_out, w in zip(w_outs, (wg, wu, wd)):
                w_out[...] = w
        g = _dot(h, wg)
        u = _dot(h, wu)
        a = (g * jax.nn.sigmoid(g)) * u * 0.5
        o_ref[...] += _dot(a.astype(BF16), wd)

    def finish():
        if emit_norm:
            y = o_ref[...]
            hn_ref[...] = (y * _rms_scale(y) * gout_ref[...]).astype(BF16)

    @pl.when(live & (f == 0))
    def _():
        x = x_ref[...]
        h = (x * _rms_scale(x) * gin_ref[...]).astype(BF16)
        h_ref[...] = h
        o_ref[...] = x
        step(h)

    @pl.when(live & (f > 0) & (f < last))
    def _():
        step(h_ref[...])

    @pl.when(live & (f == last))
    def _():
        step(h_ref[...])
        finish()

    if n_head:
        @pl.when((i == 0) & (f == 0))
        def _():
            for src, dst in zip(head_refs, (o_ref, hn_ref)):
                pltpu.sync_copy(src, dst)


def _ffn(x, g_in, wg, wu, wd, g_out, *, head=()):
    T = x.shape[0]
    tm = FFN_TM
    emit_norm = g_out is not None
    emit_weights = wg.dtype == F32
    tf = FFN_HEAD_TF if emit_weights else FFN_TF
    n_tiles = 1 if emit_weights else T // tm
    skip = 1 if head else 0
    mode = dict(pipeline_mode=pl.Buffered(1)) if n_tiles == 1 else {}
    tok = pl.BlockSpec((tm, D_MODEL), lambda i, f: (i, 0), **mode)
    vec = pl.BlockSpec((1, D_MODEL), lambda i, f: (0, 0))
    chunk = lambda i, f: jnp.where(i < skip, 0, f)
    w_up_spec = pl.BlockSpec((D_MODEL, tf), lambda i, f: (0, chunk(i, f)))
    w_down_spec = pl.BlockSpec((tf, D_MODEL), lambda i, f: (chunk(i, f), 0))
    in_specs = [tok, vec, w_up_spec, w_up_spec, w_down_spec]
    args = [x, g_in, wg, wu, wd]
    out_specs = [tok]
    out_shape = [jax.ShapeDtypeStruct((n_tiles * tm, D_MODEL), F32)]
    if emit_norm:
        in_specs.append(vec)
        args.append(g_out)
        out_specs.append(tok)
        out_shape.append(jax.ShapeDtypeStruct((n_tiles * tm, D_MODEL), BF16))
    if emit_weights:
        out_specs += [w_up_spec, w_up_spec, w_down_spec]
        out_shape += [jax.ShapeDtypeStruct(w.shape, BF16) for w in (wg, wu, wd)]
    in_specs += [pl.BlockSpec(memory_space=pl.ANY)] * len(head)
    args += list(head)
    return pl.pallas_call(
        functools.partial(_ffn_kernel, emit_norm=emit_norm, emit_weights=emit_weights, n_head=len(head)),
        grid=(n_tiles, D_FF // tf),
        in_specs=in_specs,
        out_specs=out_specs,
        out_shape=out_shape,
        scratch_shapes=[pltpu.VMEM((tm, D_MODEL), BF16)],
        compiler_params=_params("arbitrary", "arbitrary"),
        name="ffn_head" if emit_weights else "ffn",
    )(*args)


QUERY, KEY, PLAIN, GATE = "query", "key", "plain", "gate"


def _sigmoid(y):
    return 0.5 * jnp.tanh(0.5 * y) + 0.5


def _norm_rope(y, gain, cos, sin):
    y = y * _rms_scale(y) * gain
    lane = lax.broadcasted_iota(jnp.int32, y.shape, 1)
    partner = jnp.where(lane < ROPE_HALF, pltpu.roll(y, HEAD_DIM - ROPE_HALF, 1), pltpu.roll(y, ROPE_HALF, 1))
    return y * cos + partner * sin


def _proj_stages(h_ref, w_refs, gain_ref, cos_ref, sin_ref, o_ref, y_ref, slab, *, kinds, chunk, dilation):
    def compute(m, slot):
        h = h_ref[pl.ds(m * chunk, chunk), :]
        for p, w_ref in enumerate(w_refs):
            y = _dot(h, w_ref[...])
            for j in range(PROJ_PART_HEADS):
                y_ref[slot, p * PROJ_PART_HEADS + j] = y[:, j * HEAD_DIM:(j + 1) * HEAD_DIM]

    def finish(m, slot):
        rows = pl.ds(m * chunk, chunk)
        for hh, kind in enumerate(kinds):
            y = y_ref[slot, hh]
            if kind in (QUERY, KEY):
                gain = gain_ref[0:1, :] if kind == QUERY else gain_ref[1:2, :]
                y = _norm_rope(y, gain, cos_ref[rows, :], sin_ref[rows, :])
            elif kind == GATE:
                y = _sigmoid(y)
            if dilation is None:
                o_ref[rows, hh * HEAD_DIM:(hh + 1) * HEAD_DIM] = y.astype(BF16)
            else:
                sec, head = divmod(hh, DIL_HEADS)
                sec = DIL_SECTION_SLOT[sec]
                n = chunk // dilation
                dst = pl.ds(m * n, n)
                if dilation == 1:
                    o_ref[sec, head, 0, dst, :] = y.astype(BF16)
                elif dilation <= MAX_SINGLE_OP_STRIDE:
                    slab_ref = slab[0]
                    slab_ref[hh] = y
                    for c in range(dilation):
                        o_ref[sec, head, c, dst, :] = slab_ref[hh, pl.ds(c, n, stride=dilation), :].astype(BF16)
                else:
                    slab_ref, slab2_ref = slab
                    step = MAX_SINGLE_OP_STRIDE
                    assert dilation == step * step
                    n1 = chunk // step
                    slab_ref[hh] = y
                    for c_lo in range(step):
                        slab2_ref[hh, c_lo * n1:(c_lo + 1) * n1, :] = slab_ref[hh, pl.ds(c_lo, n1, stride=step), :]
                    for c_lo in range(step):
                        for c_hi in range(step):
                            o_ref[sec, head, c_lo + step * c_hi, dst, :] = slab2_ref[
                                hh, pl.ds(c_lo * n1 + c_hi, n, stride=step), :].astype(BF16)

    return compute, finish


def _proj_kernel(h_ref, *refs, kinds, chunk, dilation, convert):
    n_parts = len(kinds) // PROJ_PART_HEADS
    w_refs = refs[:n_parts]
    gain_ref, cos_ref, sin_ref = refs[n_parts:n_parts + 3]
    refs = refs[n_parts + 3:]
    if convert:
        src_ref, o_ref, dst_ref, y_ref, wbf_ref = refs[:5]
        refs = refs[5:]

        @pl.when(pl.program_id(0) == 0)
        def _():
            for p, w_ref in enumerate(w_refs):
                wbf_ref[p] = w_ref[...].astype(BF16)

        dst_ref[...] = src_ref[...].astype(BF16)
        w_refs = [wbf_ref.at[p] for p in range(n_parts)]
    else:
        o_ref, y_ref = refs[:2]
        refs = refs[2:]
    compute, finish = _proj_stages(h_ref, w_refs, gain_ref, cos_ref, sin_ref, o_ref, y_ref, refs,
                                   kinds=kinds, chunk=chunk, dilation=dilation)
    n_chunks = h_ref.shape[0] // chunk
    compute(0, 0)
    for m in range(n_chunks - 1):
        finish(m, m % 2)
        compute(m + 1, (m + 1) % 2)
    finish(n_chunks - 1, (n_chunks - 1) % 2)


def _weight_specs(parts):
    return [pl.BlockSpec((D_MODEL, PROJ_PART_WIDTH), functools.partial(lambda blk, i: (0, blk), blk),
                         pipeline_mode=pl.Buffered(1)) for blk in parts]


def _proj(h, w, gains, cos, sin, *, parts, kinds, tm, name, dil=None):
    T = h.shape[0]
    width = len(kinds) * HEAD_DIM
    assert width == len(parts) * PROJ_PART_WIDTH and w.shape[1] % PROJ_PART_WIDTH == 0
    chunk = PROJ_CHUNK
    steps = T // tm
    convert = w.dtype == F32
    scratch = [pltpu.VMEM((2, len(kinds), chunk, HEAD_DIM), F32)]
    if convert:
        scratch.append(pltpu.VMEM((len(parts), D_MODEL, PROJ_PART_WIDTH), BF16))
    if dil is None:
        dilation = None
        out_specs = [pl.BlockSpec((tm, width), lambda i: (i, 0))]
        out_shape = [jax.ShapeDtypeStruct((T, width), BF16)]
    else:
        dilation, batch, seq = dil
        n_i = seq // tm
        out_specs = [pl.BlockSpec((3, None, DIL_HEADS, dilation, tm // dilation, HEAD_DIM),
                                  lambda i: (0, i // n_i, 0, 0, i % n_i, 0))]
        out_shape = [jax.ShapeDtypeStruct((3, batch, DIL_HEADS, dilation, seq // dilation, HEAD_DIM), BF16)]
        if dilation > 1:
            scratch.append(pltpu.VMEM((len(kinds), chunk, HEAD_DIM), F32))
        if dilation > MAX_SINGLE_OP_STRIDE:
            scratch.append(pltpu.VMEM((len(kinds), chunk, HEAD_DIM), F32))
    tok = lambda cols: pl.BlockSpec((tm, cols), lambda i: (i, 0))
    in_specs = [tok(D_MODEL), *_weight_specs(parts), pl.BlockSpec((2, HEAD_DIM), lambda i: (0, 0)),
                tok(HEAD_DIM), tok(HEAD_DIM)]
    args = [h, *([w] * len(parts)), gains, cos, sin]
    if convert:
        rows_spec = pl.BlockSpec((w.shape[0] // steps, w.shape[1]), lambda i: (i, 0))
        in_specs.append(rows_spec)
        args.append(w)
        out_specs.append(rows_spec)
        out_shape.append(jax.ShapeDtypeStruct(w.shape, BF16))
    return pl.pallas_call(
        functools.partial(_proj_kernel, kinds=kinds, chunk=chunk, dilation=dilation, convert=convert),
        grid=(steps,),
        in_specs=in_specs,
        out_specs=out_specs,
        out_shape=out_shape,
        scratch_shapes=scratch,
        compiler_params=_params("arbitrary"),
        name=name,
    )(*args)


def _dil_attn_stages(ins, o_ref, scr, i):
    steps = DIL_BLOCK // BAND

    row = lax.broadcasted_iota(jnp.int32, (BAND, 2 * BAND), 0)
    col = lax.broadcasted_iota(jnp.int32, (BAND, 2 * BAND), 1)
    band_bias = jnp.where((col >= row) & (col <= row + BAND), 0.0, NEG_BIG)
    start_bias = jnp.where(i == 0, jnp.where(col >= BAND, band_bias, NEG_BIG), band_bias)

    groups = []
    for g, r in enumerate(DILATIONS):
        cur_ref, prev_ref = ins[2 * g:2 * g + 2]
        q_ref, kc_ref, vc_ref = (cur_ref.at[DIL_SECTION_SLOT[sec]] for sec in range(3))
        kp_ref, vp_ref = (prev_ref.at[DIL_SECTION_SLOT[sec]] for sec in (1, 2))
        kx_ref, vx_ref, on_ref, ln_ref = scr[4 * g:4 * g + 4]
        kx_ref[:, :BAND, :] = kp_ref[...]
        kx_ref[:, BAND:, :] = kc_ref[...]
        vx_ref[:, :BAND, :HEAD_DIM] = vp_ref[...]
        vx_ref[:, BAND:, :HEAD_DIM] = vc_ref[...]
        vx_ref[:, :, HEAD_DIM:] = jnp.ones((r, vx_ref.shape[1], HEAD_DIM), BF16)
        groups.append((r, q_ref, kx_ref, vx_ref, on_ref, ln_ref))

    def tile(t):
        for r, q_ref, kx_ref, vx_ref, on_ref, ln_ref in groups:
            tiles = DIL_BLOCK // r // BAND
            c, mb = (0, t) if tiles == steps else (t, 0) if tiles == 1 else (t % r, t // r)
            m0 = mb * BAND
            q = q_ref[c, pl.ds(m0, BAND), :]
            k = kx_ref[c, pl.ds(m0, 2 * BAND), :]
            v = vx_ref[c, pl.ds(m0, 2 * BAND), :]
            s = _dot_nt(q, k) + (start_bias if mb == 0 else band_bias)
            m = jnp.max(s, axis=-1, keepdims=True)
            p = jnp.exp(s - m)
            od = _dot(p.astype(BF16), v)
            den = od[:, HEAD_DIM:]
            dst = pl.ds(mb * (BAND * r) + c, BAND, stride=r) if r > 1 else pl.ds(m0, BAND)
            on_ref[dst, :] = od[:, :HEAD_DIM] / den
            ln_ref[dst, :] = m + jnp.log(den)

    def mix():
        lses = [grp[5][...] for grp in groups]
        top = jnp.maximum(jnp.maximum(lses[0], lses[1]), lses[2])
        ws = [jnp.exp(l - top) for l in lses]
        num = ws[0] * groups[0][4][...] + ws[1] * groups[1][4][...] + ws[2] * groups[2][4][...]
        o_ref[...] = (num / (ws[0] + ws[1] + ws[2])).astype(BF16)

    return tile, mix


def _dil_attn_specs(unit):
    in_specs, scratch = [], []
    for r in DILATIONS:
        nq = DIL_BLOCK // r
        per = nq // BAND

        def cur(*ids):
            b, h, i = unit(*ids)
            return (0, b, h, 0, i, 0)

        def prev(*ids, per=per):
            b, h, i = unit(*ids)
            return (0, b, h, 0, jnp.maximum(i * per - 1, 0), 0)

        in_specs += [pl.BlockSpec((3, None, None, r, nq, HEAD_DIM), cur),
                     pl.BlockSpec((2, None, None, r, BAND, HEAD_DIM), prev)]
        scratch += [pltpu.VMEM((r, BAND + nq, HEAD_DIM), BF16), pltpu.VMEM((r, BAND + nq, 2 * HEAD_DIM), BF16),
                    pltpu.VMEM((DIL_BLOCK, HEAD_DIM), F32), pltpu.VMEM((DIL_BLOCK, HEAD_DIM), F32)]

    def out_index(*ids):
        b, h, i = unit(*ids)
        return (b, i, h)

    return in_specs, pl.BlockSpec((None, DIL_BLOCK, HEAD_DIM), out_index), scratch


def _gate_dil_kernel(h_ref, *refs, kinds, chunk, n_i):
    n_parts = len(kinds) // PROJ_PART_HEADS
    n_dil = 2 * N_DIL_GROUPS
    w_refs, dil_ins = refs[:n_parts], refs[n_parts:n_parts + n_dil]
    vg_ref, od_ref, y_ref = refs[n_parts + n_dil:n_parts + n_dil + 3]
    dil_scr = refs[n_parts + n_dil + 3:]
    compute, finish = _proj_stages(h_ref, w_refs, None, None, None, vg_ref, y_ref, (), kinds=kinds, chunk=chunk,
                                   dilation=None)
    tile, mix = _dil_attn_stages(dil_ins, od_ref, dil_scr, pl.program_id(0) % n_i)
    n_chunks = h_ref.shape[0] // chunk
    per_chunk = (DIL_BLOCK // BAND) // n_chunks
    compute(0, 0)
    for m in range(n_chunks):
        for t in range(m * per_chunk, (m + 1) * per_chunk):
            tile(t)
        finish(m, m % 2)
        if m + 1 < n_chunks:
            compute(m + 1, (m + 1) % 2)
    mix()


def _gate_dil(h, w, qkvs, *, parts, kinds, batch, seq):
    T = h.shape[0]
    tm, chunk = PROJ_GATE_TM, PROJ_CHUNK
    n_i = seq // DIL_BLOCK
    steps = T // tm
    assert steps == batch * DIL_HEADS * n_i
    width = len(kinds) * HEAD_DIM
    unit = lambda s: (s // (DIL_HEADS * n_i), (s // n_i) % DIL_HEADS, s % n_i)
    dil_specs, od_spec, dil_scratch = _dil_attn_specs(unit)
    dil_args = [qkvs[g] for g in range(N_DIL_GROUPS) for _ in range(2)]
    return pl.pallas_call(
        functools.partial(_gate_dil_kernel, kinds=kinds, chunk=chunk, n_i=n_i),
        grid=(steps,),
        in_specs=[pl.BlockSpec((tm, D_MODEL), lambda s: (s, 0)), *_weight_specs(parts), *dil_specs],
        out_specs=[pl.BlockSpec((tm, width), lambda s: (s, 0)), od_spec],
        out_shape=[jax.ShapeDtypeStruct((T, width), BF16),
                   jax.ShapeDtypeStruct((batch, seq, DIL_GROUP_WIDTH), BF16)],
        scratch_shapes=[pltpu.VMEM((2, len(kinds), chunk, HEAD_DIM), F32), *dil_scratch],
        compiler_params=_params("arbitrary"),
        name="gate_dil",
    )(h, *([w] * len(parts)), *dil_args)


def _diff_attn_kernel(lq1_ref, lk1_ref, lq2_ref, lk2_ref, q_ref, k_ref, v_ref, sub_ref, *refs, lam_init, n_casts):
    cast_in, (o_ref, *cast_out) = refs[:n_casts], refs[n_casts:2 * n_casts + 1]
    vt_ref, acc_ref, s0_ref, s1_ref = refs[2 * n_casts + 1:]
    for src_ref, dst_ref in zip(cast_in, cast_out):
        dst_ref[...] = src_ref[...].astype(BF16)

    qi = pl.program_id(2)
    tq = q_ref.shape[0]
    tk = vt_ref.shape[2]
    dv = DIFF_HEAD_WIDTH
    per_q = tq // tk

    @pl.when(qi == 0)
    def _():
        for ch in range(vt_ref.shape[0]):
            vt_ref[ch, :dv, :] = v_ref[ch * tk:(ch + 1) * tk, :].astype(F32).T.astype(BF16)
            vt_ref[ch, dv:, :] = jnp.ones((vt_ref.shape[1] - dv, tk), BF16)

    q = q_ref[...]
    acc_ref[...] = jnp.zeros(acc_ref.shape, F32)

    def scores(kb, s_ref):
        k = k_ref[pl.ds(pl.multiple_of(kb * tk, tk), tk), :]
        for c in range(2):
            lanes = slice(c * HEAD_DIM, (c + 1) * HEAD_DIM)
            s_ref[c] = _dot_nt(k[:, lanes], q[:, lanes])

    def update(kb, s_ref, maxes, qcols=slice(None), mask=None):
        vt = vt_ref[kb]
        out = []
        for c in range(2):
            s = s_ref[c, :, qcols]
            if mask is not None:
                s = jnp.where(mask, s, NEG_BIG)
            m_new = jnp.maximum(maxes[c], jnp.max(s, axis=0, keepdims=True))
            alpha = jnp.exp2(maxes[c] - m_new)
            p = jnp.exp2(s - m_new)
            out.append(m_new)
            acc_ref[c, :, qcols] = alpha * acc_ref[c, :, qcols] + _dot(vt, p.astype(BF16))
        return tuple(out)

    def pair(j, maxes):
        kb = per_q * j
        scores(kb + 1, s1_ref)
        maxes = update(kb, s0_ref, maxes)
        scores(kb + 2, s0_ref)
        return update(kb + 1, s1_ref, maxes)

    row0 = jnp.full((1, tq), NEG_BIG, F32)
    scores(0, s0_ref)
    maxes = lax.fori_loop(0, qi, pair, (row0, row0))

    kd = per_q * qi
    lo, hi = slice(0, tk), slice(tk, tq)
    key = lax.broadcasted_iota(jnp.int32, (tk, tk), 0)
    qry = lax.broadcasted_iota(jnp.int32, (tk, tk), 1)
    tri = key <= qry
    k_last = k_ref[pl.ds(pl.multiple_of((kd + 1) * tk, tk), tk), :]
    for c in range(2):
        lanes = slice(c * HEAD_DIM, (c + 1) * HEAD_DIM)
        s1_ref[c, :, hi] = _dot_nt(k_last[:, lanes], q[tk:, lanes])
    update(kd, s0_ref, tuple(m[:, lo] for m in maxes), lo, tri)
    m_hi = update(kd, s0_ref, tuple(m[:, hi] for m in maxes), hi)
    update(kd + 1, s1_ref, m_hi, hi, tri)

    lam = (jnp.exp(jnp.sum(lq1_ref[...] * lk1_ref[...], axis=-1, keepdims=True))
           - jnp.exp(jnp.sum(lq2_ref[...] * lk2_ref[...], axis=-1, keepdims=True)) + lam_init)
    o = (acc_ref[0, :dv] / acc_ref[0, dv:dv + 1] - lam * (acc_ref[1, :dv] / acc_ref[1, dv:dv + 1])).T
    o_ref[...] = (o * _rms_scale(o) * sub_ref[...] * (1.0 - lam_init)).astype(BF16)


def _diff_attn(qk, vg, lq1, lk1, lq2, lk2, subln, casts, *, batch, seq, lam_init):
    tq, tk = DIFF_TQ, DIFF_TK
    assert tq == 2 * tk
    q_blk, k_blk, v_blk = 0, DIFF_HEADS, 0
    n_q = seq // tq
    steps = batch * DIFF_HEADS * n_q
    vec = pl.BlockSpec((1, HEAD_DIM), lambda b, h, i: (0, 0))

    cast_specs = []
    for w in casts:
        share = steps
        while w.shape[0] % (share * BF16_SUBLANES):
            share //= 2
        cast_specs.append(pl.BlockSpec(
            (w.shape[0] // share, w.shape[1]),
            functools.partial(lambda rep, b, h, i: (((b * DIFF_HEADS + h) * n_q + i) // rep, 0), steps // share)))
    o, *cast = pl.pallas_call(
        functools.partial(_diff_attn_kernel, lam_init=lam_init, n_casts=len(casts)),
        grid=(batch, DIFF_HEADS, n_q),
        in_specs=[
            vec, vec, vec, vec,
            pl.BlockSpec((None, tq, DIFF_HEAD_WIDTH), lambda b, h, i: (b, i, q_blk + h)),
            pl.BlockSpec((None, seq, DIFF_HEAD_WIDTH), lambda b, h, i: (b, 0, k_blk + h)),
            pl.BlockSpec((None, seq, DIFF_HEAD_WIDTH), lambda b, h, i: (b, 0, v_blk + h)),
            pl.BlockSpec((1, DIFF_HEAD_WIDTH), lambda b, h, i: (0, 0)),
            *cast_specs,
        ],
        out_specs=[pl.BlockSpec((None, tq, DIFF_HEAD_WIDTH), lambda b, h, i: (b, i, h)), *cast_specs],
        out_shape=[jax.ShapeDtypeStruct((batch, seq, DIFF_WIDTH), BF16),
                   *(jax.ShapeDtypeStruct(w.shape, BF16) for w in casts)],
        scratch_shapes=[pltpu.VMEM((seq // tk, DIFF_HEAD_WIDTH + DIFF_ONES_ROWS, tk), BF16),
                        pltpu.VMEM((2, DIFF_HEAD_WIDTH + DIFF_ONES_ROWS, tq), F32),
                        pltpu.VMEM((2, tk, tq), F32),
                        pltpu.VMEM((2, tk, tq), F32)],
        compiler_params=_params("arbitrary", "arbitrary", "arbitrary"),
        name="diff_attn",
    )(lq1, lk1, lq2, lk2, qk, qk, vg, subln, *casts)
    return o, cast


def _merge_kernel(x_ref, od_ref, of_ref, gd0_ref, gd1_ref, gf0_ref, gf1_ref, wa_ref, wb_ref, wo_ref, o_ref):
    gd = jnp.concatenate([gd0_ref[...], gd1_ref[...]], axis=1).astype(F32)
    gf = jnp.concatenate([gf0_ref[...], gf1_ref[...]], axis=1).astype(F32)
    y = gd * _dot(od_ref[...], wa_ref[...]) + gf * _dot(of_ref[...], wb_ref[...])
    o_ref[...] = x_ref[...] + _dot(y.astype(BF16), wo_ref[...])


def _merge(x, o_dil, o_diff, vg, wa, wb, wo):
    T = x.shape[0]
    tm = MERGE_TM
    half = D_MODEL // 2

    def tok(width, blk=0):
        return pl.BlockSpec((tm, width), lambda i: (i, blk))

    def resident(shape):
        return pl.BlockSpec(shape, lambda i: (0, 0), pipeline_mode=pl.Buffered(1))

    gate0 = DIFF_WIDTH // half
    return pl.pallas_call(
        _merge_kernel,
        grid=(T // tm,),
        in_specs=[tok(D_MODEL), tok(DIL_GROUP_WIDTH), tok(DIFF_WIDTH),
                  tok(half, gate0), tok(half, gate0 + 1), tok(half, gate0 + 2), tok(half, gate0 + 3),
                  resident(wa.shape), resident(wb.shape), resident(wo.shape)],
        out_specs=tok(D_MODEL),
        out_shape=jax.ShapeDtypeStruct((T, D_MODEL), F32),
        compiler_params=_params("parallel"),
        name="merge",
    )(x, o_dil, o_diff, vg, vg, vg, vg, wa, wb, wo)


def _rope_tables(positions):
    lane = jnp.arange(HEAD_DIM)
    freq = ROPE_THETA ** (-(2 * (lane % ROPE_HALF)).astype(F32) / ROPE_DIM)
    inv = jnp.where(lane < ROPE_DIM, freq, 0.0)
    sign = jnp.where(lane < ROPE_HALF, -1.0, jnp.where(lane < ROPE_DIM, 1.0, 0.0)).astype(F32)
    ang = positions.astype(F32).reshape(-1, 1) * inv
    return jnp.cos(ang), jnp.sin(ang) * sign


def _layer(x, tables, layer, ffn1_norm, ffn1_w_gate, ffn1_w_up, ffn1_w_down, mix_norm, w_in,
           dil_q_norm, dil_k_norm, diff_q_norm, diff_k_norm, diff_lq1, diff_lk1, diff_lq2, diff_lk2,
           diff_subln, w_dil_branch, w_diff_branch, w_out, ffn2_norm, ffn2_w_gate, ffn2_w_up, ffn2_w_down,
           *, batch, seq):
    cos, sin = tables
    row = lambda v: v.reshape(1, -1).astype(F32)
    qk_scale = HEAD_DIM ** -0.5
    lam_init = 0.8 - 0.6 * math.exp(-0.3 * layer)

    x1_0, h_0, wg1, wu1, wd1 = _ffn(x, row(ffn1_norm), ffn1_w_gate, ffn1_w_up, ffn1_w_down, row(mix_norm))
    x1, h = _ffn(x, row(ffn1_norm), wg1, wu1, wd1, row(mix_norm), head=(x1_0, h_0))

    blocks = lambda start, width: tuple(range(start // PROJ_PART_WIDTH, (start + width) // PROJ_PART_WIDTH))
    diff0 = 3 * DIL_WIDTH
    fv0 = diff0 + 2 * DIFF_WIDTH
    gains = lambda gq, gk: jnp.stack([gq * qk_scale, gk]).astype(F32)

    dil_kinds = (QUERY,) * DIL_HEADS + (KEY,) * DIL_HEADS + (PLAIN,) * DIL_HEADS
    qkvs, w_in_bf = [], w_in
    for g, r in enumerate(DILATIONS):
        out = _proj(h, w_in_bf, gains(dil_q_norm, dil_k_norm), cos, sin,
                    parts=tuple(sec * N_DIL_GROUPS + g for sec in range(3)),
                    kinds=dil_kinds, tm=PROJ_TM, name=f"dil_proj{g}", dil=(r, batch, seq))
        if g == 0:
            out, w_in_bf = out
        else:
            (out,) = out
        qkvs.append(out)

    n_diff = DIFF_WIDTH // HEAD_DIM
    (qk,) = _proj(h, w_in_bf, gains(diff_q_norm * LOG2_E, diff_k_norm), cos, sin, parts=blocks(diff0, 2 * DIFF_WIDTH),
                  kinds=(QUERY,) * n_diff + (KEY,) * n_diff, tm=PROJ_TM, name="diff_qk_proj")
    n_gate = 2 * D_MODEL // HEAD_DIM
    vg, o_dil = _gate_dil(h, w_in_bf, qkvs, parts=blocks(fv0, DIFF_WIDTH + 2 * D_MODEL),
                          kinds=(PLAIN,) * n_diff + (GATE,) * n_gate, batch=batch, seq=seq)

    later = [w_dil_branch, w_diff_branch, w_out, ffn2_w_gate, ffn2_w_up, ffn2_w_down]
    o_diff, (wa, wb, wo, wg2, wu2, wd2) = _diff_attn(
        qk.reshape(batch, seq, -1), vg.reshape(batch, seq, -1), row(diff_lq1), row(diff_lk1), row(diff_lq2),
        row(diff_lk2), row(diff_subln), later, batch=batch, seq=seq, lam_init=lam_init)

    x2 = _merge(x1, o_dil.reshape(batch * seq, -1), o_diff.reshape(batch * seq, -1), vg, wa, wb, wo)
    (out,) = _ffn(x2, row(ffn2_norm), wg2, wu2, wd2, None)
    return out


def kernel(x, positions, ffn1_norm, ffn1_w_gate, ffn1_w_up, ffn1_w_down, mix_norm, w_in, dil_q_norm, dil_k_norm, diff_q_norm, diff_k_norm, diff_lq1, diff_lk1, diff_lq2, diff_lk2, diff_subln, w_dil_branch, w_diff_branch, w_out, ffn2_norm, ffn2_w_gate, ffn2_w_up, ffn2_w_down):
    batch, seq, d_model = x.shape
    assert d_model == D_MODEL and seq % DIL_BLOCK == 0
    weights = (ffn1_norm, ffn1_w_gate, ffn1_w_up, ffn1_w_down, mix_norm, w_in, dil_q_norm, dil_k_norm,
               diff_q_norm, diff_k_norm, diff_lq1, diff_lk1, diff_lq2, diff_lk2, diff_subln, w_dil_branch,
               w_diff_branch, w_out, ffn2_norm, ffn2_w_gate, ffn2_w_up, ffn2_w_down)
    tables = _rope_tables(positions)
    y = x.reshape(batch * seq, d_model)
    for layer in range(ffn1_norm.shape[0]):
        y = _layer(y, tables, layer, *(w[layer] for w in weights), batch=batch, seq=seq)
    return y.reshape(batch, seq, d_model)
```

```python
import functools
import math

import jax
import jax.numpy as jnp
from jax import lax
from jax.experimental import pallas as pl
from jax.experimental.pallas import tpu as pltpu

F32 = jnp.float32
BF16 = jnp.bfloat16

D_MODEL = 2048
D_FF = 5632
HEAD_DIM = 128
ROPE_DIM = HEAD_DIM // 4
ROPE_HALF = ROPE_DIM // 2
ROPE_THETA = 500000.0
RMS_EPS = 1e-6
BAND = 128
DILATIONS = (1, 4, 16)
N_DIL_GROUPS = len(DILATIONS)
DIL_HEADS = 4
DIL_GROUP_WIDTH = DIL_HEADS * HEAD_DIM
DIL_WIDTH = N_DIL_GROUPS * DIL_GROUP_WIDTH
DIFF_HEADS = 4
DIFF_HEAD_WIDTH = 2 * HEAD_DIM
DIFF_WIDTH = DIFF_HEADS * DIFF_HEAD_WIDTH
DIL_SECTION_SLOT = (2, 0, 1)
DIL_BLOCK = BAND * DILATIONS[-1]
LOG2_E = math.log2(math.e)
NEG_BIG = -1e30

VMEM_LIMIT_BYTES = 62 * 1024 * 1024

FFN_TM = 1024
FFN_TF = 512
FFN_HEAD_TF = 256
PROJ_TM = 1024
PROJ_WIDE_TM = 2048
PROJ_GATE_TM = 512
PROJ_CHUNK = 256
PROJ_PART_HEADS = 4
PROJ_PART_WIDTH = PROJ_PART_HEADS * HEAD_DIM
MAX_SINGLE_OP_STRIDE = 4
MERGE_TM = 512
DIFF_TQ = 1024
DIFF_TK = 512
BF16_SUBLANES = 16
DIFF_ONES_ROWS = BF16_SUBLANES


def _params(*sem):
    return pltpu.CompilerParams(dimension_semantics=sem, vmem_limit_bytes=VMEM_LIMIT_BYTES)


def _rms_scale(x):
    return lax.rsqrt(jnp.mean(x * x, axis=-1, keepdims=True) + RMS_EPS)


def _dot(a, b):
    return jnp.dot(a, b, preferred_element_type=F32)


def _dot_nt(a, b):
    return lax.dot_general(a, b, (((1,), (1,)), ((), ())), preferred_element_type=F32)


def _ffn_kernel(*refs, emit_norm, emit_weights, n_head):
    x_ref, gin_ref, wg_ref, wu_ref, wd_ref = refs[:5]
    gout_ref = refs[5] if emit_norm else None
    head_refs = refs[5 + emit_norm:5 + emit_norm + n_head]
    outs = list(refs[5 + emit_norm + n_head:])
    o_ref = outs.pop(0)
    hn_ref = outs.pop(0) if emit_norm else None
    w_outs = [outs.pop(0) for _ in range(3)] if emit_weights else None
    (h_ref,) = outs
    i = pl.program_id(0)
    f = pl.program_id(1)
    last = pl.num_programs(1) - 1
    live = i >= (1 if n_head else 0)

    def step(h):
        wg, wu, wd = wg_ref[...], wu_ref[...], wd_ref[...]
        if emit_weights:
            wg, wu, wd = wg.astype(BF16), wu.astype(BF16), wd.astype(BF16)
            for w_out, w in zip(w_outs, (wg, wu, wd)):
                w_out[...] = w
        g = _dot(h, wg)
        u = _dot(h, wu)
        a = (g * jax.nn.sigmoid(g)) * u * 0.5
        o_ref[...] += _dot(a.astype(BF16), wd)

    def finish():
        if emit_norm:
            y = o_ref[...]
            hn_ref[...] = (y * _rms_scale(y) * gout_ref[...]).astype(BF16)

    @pl.when(live & (f == 0))
    def _():
        x = x_ref[...]
        h = (x * _rms_scale(x) * gin_ref[...]).astype(BF16)
        h_ref[...] = h
        o_ref[...] = x
        step(h)

    @pl.when(live & (f > 0) & (f < last))
    def _():
        step(h_ref[...])

    @pl.when(live & (f == last))
    def _():
        step(h_ref[...])
        finish()

    if n_head:
        @pl.when((i == 0) & (f == 0))
        def _():
            for src, dst in zip(head_refs, (o_ref, hn_ref)):
                pltpu.sync_copy(src, dst)


def _ffn(x, g_in, wg, wu, wd, g_out, *, head=()):
    T = x.shape[0]
    tm = FFN_TM
    emit_norm = g_out is not None
    emit_weights = wg.dtype == F32
    tf = FFN_HEAD_TF if emit_weights else FFN_TF
    n_tiles = 1 if emit_weights else T // tm
    skip = 1 if head else 0
    mode = dict(pipeline_mode=pl.Buffered(1)) if n_tiles == 1 else {}
    tok = pl.BlockSpec((tm, D_MODEL), lambda i, f: (i, 0), **mode)
    vec = pl.BlockSpec((1, D_MODEL), lambda i, f: (0, 0))
    chunk = lambda i, f: jnp.where(i < skip, 0, f)
    w_up_spec = pl.BlockSpec((D_MODEL, tf), lambda i, f: (0, chunk(i, f)))
    w_down_spec = pl.BlockSpec((tf, D_MODEL), lambda i, f: (chunk(i, f), 0))
    in_specs = [tok, vec, w_up_spec, w_up_spec, w_down_spec]
    args = [x, g_in, wg, wu, wd]
    out_specs = [tok]
    out_shape = [jax.ShapeDtypeStruct((n_tiles * tm, D_MODEL), F32)]
    if emit_norm:
        in_specs.append(vec)
        args.append(g_out)
        out_specs.append(tok)
        out_shape.append(jax.ShapeDtypeStruct((n_tiles * tm, D_MODEL), BF16))
    if emit_weights:
        out_specs += [w_up_spec, w_up_spec, w_down_spec]
        out_shape += [jax.ShapeDtypeStruct(w.shape, BF16) for w in (wg, wu, wd)]
    in_specs += [pl.BlockSpec(memory_space=pl.ANY)] * len(head)
    args += list(head)
    return pl.pallas_call(
        functools.partial(_ffn_kernel, emit_norm=emit_norm, emit_weights=emit_weights, n_head=len(head)),
        grid=(n_tiles, D_FF // tf),
        in_specs=in_specs,
        out_specs=out_specs,
        out_shape=out_shape,
        scratch_shapes=[pltpu.VMEM((tm, D_MODEL), BF16)],
        compiler_params=_params("arbitrary", "arbitrary"),
        name="ffn_head" if emit_weights else "ffn",
    )(*args)


QUERY, KEY, PLAIN, GATE = "query", "key", "plain", "gate"


def _sigmoid(y):
    return 0.5 * jnp.tanh(0.5 * y) + 0.5


def _norm_rope(y, gain, cos, sin):
    y = y * _rms_scale(y) * gain
    lane = lax.broadcasted_iota(jnp.int32, y.shape, 1)
    partner = jnp.where(lane < ROPE_HALF, pltpu.roll(y, HEAD_DIM - ROPE_HALF, 1), pltpu.roll(y, ROPE_HALF, 1))
    return y * cos + partner * sin


def _proj_stages(h_ref, w_refs, gain_ref, cos_ref, sin_ref, o_ref, y_ref, slab, *, kinds, chunk, dilation):
    def compute(m, slot):
        h = h_ref[pl.ds(m * chunk, chunk), :]
        for p, w_ref in enumerate(w_refs):
            y = _dot(h, w_ref[...])
            for j in range(PROJ_PART_HEADS):
                y_ref[slot, p * PROJ_PART_HEADS + j] = y[:, j * HEAD_DIM:(j + 1) * HEAD_DIM]

    def finish(m, slot):
        rows = pl.ds(m * chunk, chunk)
        for hh, kind in enumerate(kinds):
            y = y_ref[slot, hh]
            if kind in (QUERY, KEY):
                gain = gain_ref[0:1, :] if kind == QUERY else gain_ref[1:2, :]
                y = _norm_rope(y, gain, cos_ref[rows, :], sin_ref[rows, :])
            elif kind == GATE:
                y = _sigmoid(y)
            if dilation is None:
                o_ref[rows, hh * HEAD_DIM:(hh + 1) * HEAD_DIM] = y.astype(BF16)
            else:
                sec, head = divmod(hh, DIL_HEADS)
                sec = DIL_SECTION_SLOT[sec]
                n = chunk // dilation
                dst = pl.ds(m * n, n)
                if dilation == 1:
                    o_ref[sec, head, 0, dst, :] = y.astype(BF16)
                elif dilation <= MAX_SINGLE_OP_STRIDE:
                    slab_ref = slab[0]
                    slab_ref[hh] = y
                    for c in range(dilation):
                        o_ref[sec, head, c, dst, :] = slab_ref[hh, pl.ds(c, n, stride=dilation), :].astype(BF16)
                else:
                    slab_ref, slab2_ref = slab
                    step = MAX_SINGLE_OP_STRIDE
                    assert dilation == step * step
                    n1 = chunk // step
                    slab_ref[hh] = y
                    for c_lo in range(step):
                        slab2_ref[hh, c_lo * n1:(c_lo + 1) * n1, :] = slab_ref[hh, pl.ds(c_lo, n1, stride=step), :]
                    for c_lo in range(step):
                        for c_hi in range(step):
                            o_ref[sec, head, c_lo + step * c_hi, dst, :] = slab2_ref[
                                hh, pl.ds(c_lo * n1 + c_hi, n, stride=step), :].astype(BF16)

    return compute, finish


def _proj_kernel(h_ref, *refs, kinds, chunk, dilation, convert):
    n_parts = len(kinds) // PROJ_PART_HEADS
    w_refs = refs[:n_parts]
    gain_ref, cos_ref, sin_ref = refs[n_parts:n_parts + 3]
    refs = refs[n_parts + 3:]
    if convert:
        src_ref, o_ref, dst_ref, y_ref, wbf_ref = refs[:5]
        refs = refs[5:]

        @pl.when(pl.program_id(0) == 0)
        def _():
            for p, w_ref in enumerate(w_refs):
                wbf_ref[p] = w_ref[...].astype(BF16)

        dst_ref[...] = src_ref[...].astype(BF16)
        w_refs = [wbf_ref.at[p] for p in range(n_parts)]
    else:
        o_ref, y_ref = refs[:2]
        refs = refs[2:]
    compute, finish = _proj_stages(h_ref, w_refs, gain_ref, cos_ref, sin_ref, o_ref, y_ref, refs,
                                   kinds=kinds, chunk=chunk, dilation=dilation)
    n_chunks = h_ref.shape[0] // chunk
    compute(0, 0)
    for m in range(n_chunks - 1):
        finish(m, m % 2)
        compute(m + 1, (m + 1) % 2)
    finish(n_chunks - 1, (n_chunks - 1) % 2)


def _weight_specs(parts):
    return [pl.BlockSpec((D_MODEL, PROJ_PART_WIDTH), functools.partial(lambda blk, i: (0, blk), blk),
                         pipeline_mode=pl.Buffered(1)) for blk in parts]


def _proj(h, w, gains, cos, sin, *, parts, kinds, tm, name, dil=None):
    T = h.shape[0]
    width = len(kinds) * HEAD_DIM
    assert width == len(parts) * PROJ_PART_WIDTH and w.shape[1] % PROJ_PART_WIDTH == 0
    chunk = PROJ_CHUNK
    steps = T // tm
    convert = w.dtype == F32
    scratch = [pltpu.VMEM((2, len(kinds), chunk, HEAD_DIM), F32)]
    if convert:
        scratch.append(pltpu.VMEM((len(parts), D_MODEL, PROJ_PART_WIDTH), BF16))
    if dil is None:
        dilation = None
        out_specs = [pl.BlockSpec((tm, width), lambda i: (i, 0))]
        out_shape = [jax.ShapeDtypeStruct((T, width), BF16)]
    else:
        dilation, batch, seq = dil
        n_i = seq // tm
        out_specs = [pl.BlockSpec((3, None, DIL_HEADS, dilation, tm // dilation, HEAD_DIM),
                                  lambda i: (0, i // n_i, 0, 0, i % n_i, 0))]
        out_shape = [jax.ShapeDtypeStruct((3, batch, DIL_HEADS, dilation, seq // dilation, HEAD_DIM), BF16)]
        if dilation > 1:
            scratch.append(pltpu.VMEM((len(kinds), chunk, HEAD_DIM), F32))
        if dilation > MAX_SINGLE_OP_STRIDE:
            scratch.append(pltpu.VMEM((len(kinds), chunk, HEAD_DIM), F32))
    tok = lambda cols: pl.BlockSpec((tm, cols), lambda i: (i, 0))
    in_specs = [tok(D_MODEL), *_weight_specs(parts), pl.BlockSpec((2, HEAD_DIM), lambda i: (0, 0)),
                tok(HEAD_DIM), tok(HEAD_DIM)]
    args = [h, *([w] * len(parts)), gains, cos, sin]
    if convert:
        rows_spec = pl.BlockSpec((w.shape[0] // steps, w.shape[1]), lambda i: (i, 0))
        in_specs.append(rows_spec)
        args.append(w)
        out_specs.append(rows_spec)
        out_shape.append(jax.ShapeDtypeStruct(w.shape, BF16))
    return pl.pallas_call(
        functools.partial(_proj_kernel, kinds=kinds, chunk=chunk, dilation=dilation, convert=convert),
        grid=(steps,),
        in_specs=in_specs,
        out_specs=out_specs,
        out_shape=out_shape,
        scratch_shapes=scratch,
        compiler_params=_params("arbitrary"),
        name=name,
    )(*args)


def _dil_attn_stages(ins, o_ref, scr, i):
    steps = DIL_BLOCK // BAND

    row = lax.broadcasted_iota(jnp.int32, (BAND, 2 * BAND), 0)
    col = lax.broadcasted_iota(jnp.int32, (BAND, 2 * BAND), 1)
    band_bias = jnp.where((col >= row) & (col <= row + BAND), 0.0, NEG_BIG)
    start_bias = jnp.where(i == 0, jnp.where(col >= BAND, band_bias, NEG_BIG), band_bias)

    groups = []
    for g, r in enumerate(DILATIONS):
        cur_ref, prev_ref = ins[2 * g:2 * g + 2]
        q_ref, kc_ref, vc_ref = (cur_ref.at[DIL_SECTION_SLOT[sec]] for sec in range(3))
        kp_ref, vp_ref = (prev_ref.at[DIL_SECTION_SLOT[sec]] for sec in (1, 2))
        kx_ref, vx_ref, on_ref, ln_ref = scr[4 * g:4 * g + 4]
        kx_ref[:, :BAND, :] = kp_ref[...]
        kx_ref[:, BAND:, :] = kc_ref[...]
        vx_ref[:, :BAND, :HEAD_DIM] = vp_ref[...]
        vx_ref[:, BAND:, :HEAD_DIM] = vc_ref[...]
        vx_ref[:, :, HEAD_DIM:] = jnp.ones((r, vx_ref.shape[1], HEAD_DIM), BF16)
        groups.append((r, q_ref, kx_ref, vx_ref, on_ref, ln_ref))

    def tile(t):
        for r, q_ref, kx_ref, vx_ref, on_ref, ln_ref in groups:
            tiles = DIL_BLOCK // r // BAND
            c, mb = (0, t) if tiles == steps else (t, 0) if tiles == 1 else (t % r, t // r)
            m0 = mb * BAND
            q = q_ref[c, pl.ds(m0, BAND), :]
            k = kx_ref[c, pl.ds(m0, 2 * BAND), :]
            v = vx_ref[c, pl.ds(m0, 2 * BAND), :]
            s = _dot_nt(q, k) + (start_bias if mb == 0 else band_bias)
            m = jnp.max(s, axis=-1, keepdims=True)
            p = jnp.exp(s - m)
            od = _dot(p.astype(BF16), v)
            den = od[:, HEAD_DIM:]
            dst = pl.ds(mb * (BAND * r) + c, BAND, stride=r) if r > 1 else pl.ds(m0, BAND)
            on_ref[dst, :] = od[:, :HEAD_DIM] / den
            ln_ref[dst, :] = m + jnp.log(den)

    def mix():
        lses = [grp[5][...] for grp in groups]
        top = jnp.maximum(jnp.maximum(lses[0], lses[1]), lses[2])
        ws = [jnp.exp(l - top) for l in lses]
        num = ws[0] * groups[0][4][...] + ws[1] * groups[1][4][...] + ws[2] * groups[2][4][...]
        o_ref[...] = (num / (ws[0] + ws[1] + ws[2])).astype(BF16)

    return tile, mix


def _dil_attn_specs(unit):
    in_specs, scratch = [], []
    for r in DILATIONS:
        nq = DIL_BLOCK // r
        per = nq // BAND

        def cur(*ids):
            b, h, i = unit(*ids)
            return (0, b, h, 0, i, 0)

        def prev(*ids, per=per):
            b, h, i = unit(*ids)
            return (0, b, h, 0, jnp.maximum(i * per - 1, 0), 0)

        in_specs += [pl.BlockSpec((3, None, None, r, nq, HEAD_DIM), cur),
                     pl.BlockSpec((2, None, None, r, BAND, HEAD_DIM), prev)]
        scratch += [pltpu.VMEM((r, BAND + nq, HEAD_DIM), BF16), pltpu.VMEM((r, BAND + nq, 2 * HEAD_DIM), BF16),
                    pltpu.VMEM((DIL_BLOCK, HEAD_DIM), F32), pltpu.VMEM((DIL_BLOCK, HEAD_DIM), F32)]

    def out_index(*ids):
        b, h, i = unit(*ids)
        return (b, i, h)

    return in_specs, pl.BlockSpec((None, DIL_BLOCK, HEAD_DIM), out_index), scratch


def _gate_dil_kernel(h_ref, *refs, kinds, chunk, n_i):
    n_parts = len(kinds) // PROJ_PART_HEADS
    n_dil = 2 * N_DIL_GROUPS
    w_refs, dil_ins = refs[:n_parts], refs[n_parts:n_parts + n_dil]
    vg_ref, od_ref, y_ref = refs[n_parts + n_dil:n_parts + n_dil + 3]
    dil_scr = refs[n_parts + n_dil + 3:]
    compute, finish = _proj_stages(h_ref, w_refs, None, None, None, vg_ref, y_ref, (), kinds=kinds, chunk=chunk,
                                   dilation=None)
    tile, mix = _dil_attn_stages(dil_ins, od_ref, dil_scr, pl.program_id(0) % n_i)
    n_chunks = h_ref.shape[0] // chunk
    per_chunk = (DIL_BLOCK // BAND) // n_chunks
    compute(0, 0)
    for m in range(n_chunks):
        for t in range(m * per_chunk, (m + 1) * per_chunk):
            tile(t)
        finish(m, m % 2)
        if m + 1 < n_chunks:
            compute(m + 1, (m + 1) % 2)
    mix()


def _gate_dil(h, w, qkvs, *, parts, kinds, batch, seq):
    T = h.shape[0]
    tm, chunk = PROJ_GATE_TM, PROJ_CHUNK
    n_i = seq // DIL_BLOCK
    steps = T // tm
    assert steps == batch * DIL_HEADS * n_i
    width = len(kinds) * HEAD_DIM
    unit = lambda s: (s // (DIL_HEADS * n_i), (s // n_i) % DIL_HEADS, s % n_i)
    dil_specs, od_spec, dil_scratch = _dil_attn_specs(unit)
    dil_args = [qkvs[g] for g in range(N_DIL_GROUPS) for _ in range(2)]
    return pl.pallas_call(
        functools.partial(_gate_dil_kernel, kinds=kinds, chunk=chunk, n_i=n_i),
        grid=(steps,),
        in_specs=[pl.BlockSpec((tm, D_MODEL), lambda s: (s, 0)), *_weight_specs(parts), *dil_specs],
        out_specs=[pl.BlockSpec((tm, width), lambda s: (s, 0)), od_spec],
        out_shape=[jax.ShapeDtypeStruct((T, width), BF16),
                   jax.ShapeDtypeStruct((batch, seq, DIL_GROUP_WIDTH), BF16)],
        scratch_shapes=[pltpu.VMEM((2, len(kinds), chunk, HEAD_DIM), F32), *dil_scratch],
        compiler_params=_params("arbitrary"),
        name="gate_dil",
    )(h, *([w] * len(parts)), *dil_args)


def _diff_attn_kernel(lq1_ref, lk1_ref, lq2_ref, lk2_ref, q_ref, k_ref, v_ref, sub_ref, *refs, lam_init, n_casts):
    cast_in, (o_ref, *cast_out) = refs[:n_casts], refs[n_casts:2 * n_casts + 1]
    vt_ref, acc_ref, s0_ref, s1_ref = refs[2 * n_casts + 1:]
    for src_ref, dst_ref in zip(cast_in, cast_out):
        dst_ref[...] = src_ref[...].astype(BF16)

    qi = pl.program_id(2)
    tq = q_ref.shape[0]
    tk = vt_ref.shape[2]
    dv = DIFF_HEAD_WIDTH
    per_q = tq // tk

    q = q_ref[...]
    acc_ref[...] = jnp.zeros(acc_ref.shape, F32)

    def scores(kb, s_ref):
        k = k_ref[pl.ds(pl.multiple_of(kb * tk, tk), tk), :]
        for c in range(2):
            lanes = slice(c * HEAD_DIM, (c + 1) * HEAD_DIM)
            s_ref[c] = _dot_nt(k[:, lanes], q[:, lanes])

    def update(kb, s_ref, maxes, qcols=slice(None), mask=None):
        vt = vt_ref[kb]
        out = []
        for c in range(2):
            s = s_ref[c, :, qcols]
            if mask is not None:
                s = jnp.where(mask, s, NEG_BIG)
            m_new = jnp.maximum(maxes[c], jnp.max(s, axis=0, keepdims=True))
            alpha = jnp.exp2(maxes[c] - m_new)
            p = jnp.exp2(s - m_new)
            out.append(m_new)
            acc_ref[c, :, qcols] = alpha * acc_ref[c, :, qcols] + _dot(vt, p.astype(BF16))
        return tuple(out)

    def pair(j, maxes):
        kb = per_q * j
        scores(kb + 1, s1_ref)
        maxes = update(kb, s0_ref, maxes)
        scores(kb + 2, s0_ref)
        return update(kb + 1, s1_ref, maxes)

    @pl.when(qi == 0)
    def _():
        for ch in range(vt_ref.shape[0]):
            vt_ref[ch, :dv, :] = v_ref[ch * tk:(ch + 1) * tk, :].astype(F32).T.astype(BF16)
            vt_ref[ch, dv:, :] = jnp.ones((vt_ref.shape[1] - dv, tk), BF16)
        scores(0, s0_ref)

    @pl.when(qi > 0)
    def _():
        scores(0, s0_ref)

    row0 = jnp.full((1, tq), NEG_BIG, F32)
    maxes = lax.fori_loop(0, qi, pair, (row0, row0))

    kd = per_q * qi
    lo, hi = slice(0, tk), slice(tk, tq)
    key = lax.broadcasted_iota(jnp.int32, (tk, tk), 0)
    qry = lax.broadcasted_iota(jnp.int32, (tk, tk), 1)
    tri = key <= qry
    k_last = k_ref[pl.ds(pl.multiple_of((kd + 1) * tk, tk), tk), :]
    for c in range(2):
        lanes = slice(c * HEAD_DIM, (c + 1) * HEAD_DIM)
        s1_ref[c, :, hi] = _dot_nt(k_last[:, lanes], q[tk:, lanes])
    update(kd, s0_ref, tuple(m[:, lo] for m in maxes), lo, tri)
    m_hi = update(kd, s0_ref, tuple(m[:, hi] for m in maxes), hi)
    update(kd + 1, s1_ref, m_hi, hi, tri)

    lam = (jnp.exp(jnp.sum(lq1_ref[...] * lk1_ref[...], axis=-1, keepdims=True))
           - jnp.exp(jnp.sum(lq2_ref[...] * lk2_ref[...], axis=-1, keepdims=True)) + lam_init)
    o = (acc_ref[0, :dv] / acc_ref[0, dv:dv + 1] - lam * (acc_ref[1, :dv] / acc_ref[1, dv:dv + 1])).T
    o_ref[...] = (o * _rms_scale(o) * sub_ref[...] * (1.0 - lam_init)).astype(BF16)


def _diff_attn(qk, vg, lq1, lk1, lq2, lk2, subln, casts, *, batch, seq, lam_init):
    tq, tk = DIFF_TQ, DIFF_TK
    assert tq == 2 * tk
    q_blk, k_blk, v_blk = 0, DIFF_HEADS, 0
    n_q = seq // tq
    steps = batch * DIFF_HEADS * n_q
    vec = pl.BlockSpec((1, HEAD_DIM), lambda b, h, i: (0, 0))

    cast_specs = []
    for w in casts:
        share = steps
        while w.shape[0] % (share * BF16_SUBLANES):
            share //= 2
        cast_specs.append(pl.BlockSpec(
            (w.shape[0] // share, w.shape[1]),
            functools.partial(lambda rep, b, h, i: (((b * DIFF_HEADS + h) * n_q + i) // rep, 0), steps // share)))
    o, *cast = pl.pallas_call(
        functools.partial(_diff_attn_kernel, lam_init=lam_init, n_casts=len(casts)),
        grid=(batch, DIFF_HEADS, n_q),
        in_specs=[
            vec, vec, vec, vec,
            pl.BlockSpec((None, tq, DIFF_HEAD_WIDTH), lambda b, h, i: (b, i, q_blk + h)),
            pl.BlockSpec((None, seq, DIFF_HEAD_WIDTH), lambda b, h, i: (b, 0, k_blk + h)),
            pl.BlockSpec((None, seq, DIFF_HEAD_WIDTH), lambda b, h, i: (b, 0, v_blk + h)),
            pl.BlockSpec((1, DIFF_HEAD_WIDTH), lambda b, h, i: (0, 0)),
            *cast_specs,
        ],
        out_specs=[pl.BlockSpec((None, tq, DIFF_HEAD_WIDTH), lambda b, h, i: (b, i, h)), *cast_specs],
        out_shape=[jax.ShapeDtypeStruct((batch, seq, DIFF_WIDTH), BF16),
                   *(jax.ShapeDtypeStruct(w.shape, BF16) for w in casts)],
        scratch_shapes=[pltpu.VMEM((seq // tk, DIFF_HEAD_WIDTH + DIFF_ONES_ROWS, tk), BF16),
                        pltpu.VMEM((2, DIFF_HEAD_WIDTH + DIFF_ONES_ROWS, tq), F32),
                        pltpu.VMEM((2, tk, tq), F32),
                        pltpu.VMEM((2, tk, tq), F32)],
        compiler_params=_params("arbitrary", "arbitrary", "arbitrary"),
        name="diff_attn",
    )(lq1, lk1, lq2, lk2, qk, qk, vg, subln, *casts)
    return o, cast


def _merge_kernel(x_ref, od_ref, of_ref, gd0_ref, gd1_ref, gf0_ref, gf1_ref, wa_ref, wb_ref, wo_ref, o_ref):
    gd = jnp.concatenate([gd0_ref[...], gd1_ref[...]], axis=1).astype(F32)
    gf = jnp.concatenate([gf0_ref[...], gf1_ref[...]], axis=1).astype(F32)
    y = gd * _dot(od_ref[...], wa_ref[...]) + gf * _dot(of_ref[...], wb_ref[...])
    o_ref[...] = x_ref[...] + _dot(y.astype(BF16), wo_ref[...])


def _merge(x, o_dil, o_diff, vg, wa, wb, wo):
    T = x.shape[0]
    tm = MERGE_TM
    half = D_MODEL // 2

    def tok(width, blk=0):
        return pl.BlockSpec((tm, width), lambda i: (i, blk))

    def resident(shape):
        return pl.BlockSpec(shape, lambda i: (0, 0), pipeline_mode=pl.Buffered(1))

    gate0 = DIFF_WIDTH // half
    return pl.pallas_call(
        _merge_kernel,
        grid=(T // tm,),
        in_specs=[tok(D_MODEL), tok(DIL_GROUP_WIDTH), tok(DIFF_WIDTH),
                  tok(half, gate0), tok(half, gate0 + 1), tok(half, gate0 + 2), tok(half, gate0 + 3),
                  resident(wa.shape), resident(wb.shape), resident(wo.shape)],
        out_specs=tok(D_MODEL),
        out_shape=jax.ShapeDtypeStruct((T, D_MODEL), F32),
        compiler_params=_params("parallel"),
        name="merge",
    )(x, o_dil, o_diff, vg, vg, vg, vg, wa, wb, wo)


def _rope_tables(positions):
    lane = jnp.arange(HEAD_DIM)
    freq = ROPE_THETA ** (-(2 * (lane % ROPE_HALF)).astype(F32) / ROPE_DIM)
    inv = jnp.where(lane < ROPE_DIM, freq, 0.0)
    sign = jnp.where(lane < ROPE_HALF, -1.0, jnp.where(lane < ROPE_DIM, 1.0, 0.0)).astype(F32)
    ang = positions.astype(F32).reshape(-1, 1) * inv
    return jnp.cos(ang), jnp.sin(ang) * sign


def _layer(x, tables, layer, ffn1_norm, ffn1_w_gate, ffn1_w_up, ffn1_w_down, mix_norm, w_in,
           dil_q_norm, dil_k_norm, diff_q_norm, diff_k_norm, diff_lq1, diff_lk1, diff_lq2, diff_lk2,
           diff_subln, w_dil_branch, w_diff_branch, w_out, ffn2_norm, ffn2_w_gate, ffn2_w_up, ffn2_w_down,
           *, batch, seq):
    cos, sin = tables
    row = lambda v: v.reshape(1, -1).astype(F32)
    qk_scale = HEAD_DIM ** -0.5
    lam_init = 0.8 - 0.6 * math.exp(-0.3 * layer)

    x1_0, h_0, wg1, wu1, wd1 = _ffn(x, row(ffn1_norm), ffn1_w_gate, ffn1_w_up, ffn1_w_down, row(mix_norm))
    x1, h = _ffn(x, row(ffn1_norm), wg1, wu1, wd1, row(mix_norm), head=(x1_0, h_0))

    blocks = lambda start, width: tuple(range(start // PROJ_PART_WIDTH, (start + width) // PROJ_PART_WIDTH))
    diff0 = 3 * DIL_WIDTH
    fv0 = diff0 + 2 * DIFF_WIDTH
    gains = lambda gq, gk: jnp.stack([gq * qk_scale, gk]).astype(F32)

    dil_kinds = (QUERY,) * DIL_HEADS + (KEY,) * DIL_HEADS + (PLAIN,) * DIL_HEADS
    qkvs, w_in_bf = [], w_in
    for g, r in enumerate(DILATIONS):
        out = _proj(h, w_in_bf, gains(dil_q_norm, dil_k_norm), cos, sin,
                    parts=tuple(sec * N_DIL_GROUPS + g for sec in range(3)),
                    kinds=dil_kinds, tm=PROJ_TM if g == 0 else PROJ_WIDE_TM, name=f"dil_proj{g}",
                    dil=(r, batch, seq))
        if g == 0:
            out, w_in_bf = out
        else:
            (out,) = out
        qkvs.append(out)

    n_diff = DIFF_WIDTH // HEAD_DIM
    (qk,) = _proj(h, w_in_bf, gains(diff_q_norm * LOG2_E, diff_k_norm), cos, sin, parts=blocks(diff0, 2 * DIFF_WIDTH),
                  kinds=(QUERY,) * n_diff + (KEY,) * n_diff, tm=PROJ_WIDE_TM, name="diff_qk_proj")
    n_gate = 2 * D_MODEL // HEAD_DIM
    vg, o_dil = _gate_dil(h, w_in_bf, qkvs, parts=blocks(fv0, DIFF_WIDTH + 2 * D_MODEL),
                          kinds=(PLAIN,) * n_diff + (GATE,) * n_gate, batch=batch, seq=seq)

    later = [w_dil_branch, w_diff_branch, w_out, ffn2_w_gate, ffn2_w_up, ffn2_w_down]
    o_diff, (wa, wb, wo, wg2, wu2, wd2) = _diff_attn(
        qk.reshape(batch, seq, -1), vg.reshape(batch, seq, -1), row(diff_lq1), row(diff_lk1), row(diff_lq2),
        row(diff_lk2), row(diff_subln), later, batch=batch, seq=seq, lam_init=lam_init)

    x2 = _merge(x1, o_dil.reshape(batch * seq, -1), o_diff.reshape(batch * seq, -1), vg, wa, wb, wo)
    (out,) = _ffn(x2, row(ffn2_norm), wg2, wu2, wd2, None)
    return out


def kernel(x, positions, ffn1_norm, ffn1_w_gate, ffn1_w_up, ffn1_w_down, mix_norm, w_in, dil_q_norm, dil_k_norm, diff_q_norm, diff_k_norm, diff_lq1, diff_lk1, diff_lq2, diff_lk2, diff_subln, w_dil_branch, w_diff_branch, w_out, ffn2_norm, ffn2_w_gate, ffn2_w_up, ffn2_w_down):
    batch, seq, d_model = x.shape
    assert d_model == D_MODEL and seq % DIL_BLOCK == 0
    weights = (ffn1_norm, ffn1_w_gate, ffn1_w_up, ffn1_w_down, mix_norm, w_in, dil_q_norm, dil_k_norm,
               diff_q_norm, diff_k_norm, diff_lq1, diff_lk1, diff_lq2, diff_lk2, diff_subln, w_dil_branch,
               w_diff_branch, w_out, ffn2_norm, ffn2_w_gate, ffn2_w_up, ffn2_w_down)
    tables = _rope_tables(positions)
    y = x.reshape(batch * seq, d_model)
    for layer in range(ffn1_norm.shape[0]):
        y = _layer(y, tables, layer, *(w[layer] for w in weights), batch=batch, seq=seq)
    return y.reshape(batch, seq, d_model)
```

```python
import functools
import math

import jax
import jax.numpy as jnp
from jax import lax
from jax.experimental import pallas as pl
from jax.experimental.pallas import tpu as pltpu

F32 = jnp.float32
BF16 = jnp.bfloat16

D_MODEL = 2048
D_FF = 5632
HEAD_DIM = 128
ROPE_DIM = HEAD_DIM // 4
ROPE_HALF = ROPE_DIM // 2
ROPE_THETA = 500000.0
RMS_EPS = 1e-6
BAND = 128
DILATIONS = (1, 4, 16)
N_DIL_GROUPS = len(DILATIONS)
DIL_HEADS = 4
DIL_GROUP_WIDTH = DIL_HEADS * HEAD_DIM
DIL_WIDTH = N_DIL_GROUPS * DIL_GROUP_WIDTH
DIFF_HEADS = 4
DIFF_HEAD_WIDTH = 2 * HEAD_DIM
DIFF_WIDTH = DIFF_HEADS * DIFF_HEAD_WIDTH
DIL_SECTION_SLOT = (2, 0, 1)
DIL_BLOCK = BAND * DILATIONS[-1]
LOG2_E = math.log2(math.e)
NEG_BIG = -1e30

VMEM_LIMIT_BYTES = 62 * 1024 * 1024

FFN_TM = 1024
FFN_TF = 512
FFN_CHUNKS = D_FF // FFN_TF
FFN_HEAD_TF = 256
PROJ_TM = 1024
PROJ_GATE_TM = 512
PROJ_CHUNK = 256
PROJ_PART_HEADS = 4
PROJ_PART_WIDTH = PROJ_PART_HEADS * HEAD_DIM
MAX_SINGLE_OP_STRIDE = 4
MERGE_TM = 512
DIFF_TQ = 1024
DIFF_TK = 512
BF16_SUBLANES = 16
DIFF_ONES_ROWS = BF16_SUBLANES


def _params(*sem):
    return pltpu.CompilerParams(dimension_semantics=sem, vmem_limit_bytes=VMEM_LIMIT_BYTES)


def _rms_scale(x):
    return lax.rsqrt(jnp.mean(x * x, axis=-1, keepdims=True) + RMS_EPS)


def _dot(a, b):
    return jnp.dot(a, b, preferred_element_type=F32)


def _dot_nt(a, b):
    return lax.dot_general(a, b, (((1,), (1,)), ((), ())), preferred_element_type=F32)


def _ffn_kernel(*refs, emit_norm, emit_weights, n_head):
    x_ref, gin_ref, wg_ref, wu_ref, wd_ref = refs[:5]
    gout_ref = refs[5] if emit_norm else None
    head_refs = refs[5 + emit_norm:5 + emit_norm + n_head]
    outs = list(refs[5 + emit_norm + n_head:])
    o_ref = outs.pop(0)
    hn_ref = outs.pop(0) if emit_norm else None
    w_outs = [outs.pop(0) for _ in range(3)] if emit_weights else None
    (h_ref,) = outs
    i = pl.program_id(0)
    f = pl.program_id(1)
    last = pl.num_programs(1) - 1
    live = i >= (1 if n_head else 0)

    def step(h):
        wg, wu, wd = wg_ref[...], wu_ref[...], wd_ref[...]
        if emit_weights:
            wg, wu, wd = wg.astype(BF16), wu.astype(BF16), wd.astype(BF16)
            for w_out, w in zip(w_outs, (wg, wu, wd)):
                w_out[...] = w
        g = _dot(h, wg)
        u = _dot(h, wu)
        a = (g * jax.nn.sigmoid(g)) * u * 0.5
        o_ref[...] += _dot(a.astype(BF16), wd)

    def finish():
        if emit_norm:
            y = o_ref[...]
            hn_ref[...] = (y * _rms_scale(y) * gout_ref[...]).astype(BF16)

    @pl.when(live & (f == 0))
    def _():
        x = x_ref[...]
        h = (x * _rms_scale(x) * gin_ref[...]).astype(BF16)
        h_ref[...] = h
        o_ref[...] = x
        step(h)

    @pl.when(live & (f > 0) & (f < last))
    def _():
        step(h_ref[...])

    @pl.when(live & (f == last))
    def _():
        step(h_ref[...])
        finish()

    if n_head:
        @pl.when((i == 0) & (f == 0))
        def _():
            for src, dst in zip(head_refs, (o_ref, hn_ref)):
                pltpu.sync_copy(src, dst)


def _ffn(x, g_in, wg, wu, wd, g_out, *, head=()):
    T = x.shape[0]
    tm = FFN_TM
    emit_norm = g_out is not None
    emit_weights = wg.dtype == F32
    tf = FFN_HEAD_TF if emit_weights else FFN_TF
    n_tiles = 1 if emit_weights else T // tm
    skip = 1 if head else 0
    mode = dict(pipeline_mode=pl.Buffered(1)) if n_tiles == 1 else {}
    tok = pl.BlockSpec((tm, D_MODEL), lambda i, f: (i, 0), **mode)
    vec = pl.BlockSpec((1, D_MODEL), lambda i, f: (0, 0))
    chunk = lambda i, f: jnp.where(i < skip, 0, f)
    w_up_spec = pl.BlockSpec((D_MODEL, tf), lambda i, f: (0, chunk(i, f)))
    w_down_spec = pl.BlockSpec((tf, D_MODEL), lambda i, f: (chunk(i, f), 0))
    in_specs = [tok, vec, w_up_spec, w_up_spec, w_down_spec]
    args = [x, g_in, wg, wu, wd]
    out_specs = [tok]
    out_shape = [jax.ShapeDtypeStruct((n_tiles * tm, D_MODEL), F32)]
    if emit_norm:
        in_specs.append(vec)
        args.append(g_out)
        out_specs.append(tok)
        out_shape.append(jax.ShapeDtypeStruct((n_tiles * tm, D_MODEL), BF16))
    if emit_weights:
        out_specs += [w_up_spec, w_up_spec, w_down_spec]
        out_shape += [jax.ShapeDtypeStruct(w.shape, BF16) for w in (wg, wu, wd)]
    in_specs += [pl.BlockSpec(memory_space=pl.ANY)] * len(head)
    args += list(head)
    return pl.pallas_call(
        functools.partial(_ffn_kernel, emit_norm=emit_norm, emit_weights=emit_weights, n_head=len(head)),
        grid=(n_tiles, D_FF // tf),
        in_specs=in_specs,
        out_specs=out_specs,
        out_shape=out_shape,
        scratch_shapes=[pltpu.VMEM((tm, D_MODEL), BF16)],
        compiler_params=_params("arbitrary", "arbitrary"),
        name="ffn_head" if emit_weights else "ffn",
    )(*args)


def _ffn_ring_kernel(x_ref, gin_ref, wg_hbm, wu_hbm, wd_hbm, o_ref, h_ref, wg_buf, wu_buf, wd_buf, sems):
    tf = wg_buf.shape[2]

    def copies(f, slot):
        cols = pl.ds(pl.multiple_of(f * tf, tf), tf)
        return (pltpu.make_async_copy(wg_hbm.at[:, cols], wg_buf.at[slot], sems.at[0, slot]),
                pltpu.make_async_copy(wu_hbm.at[:, cols], wu_buf.at[slot], sems.at[1, slot]),
                pltpu.make_async_copy(wd_hbm.at[cols, :], wd_buf.at[slot], sems.at[2, slot]))

    def start(f, slot):
        for c in copies(f, slot):
            c.start()

    def chunk(f, slot, then_start=None):
        for c in copies(f, slot):
            c.wait()
        h = h_ref[...]
        g = _dot(h, wg_buf[slot])
        u = _dot(h, wu_buf[slot])
        a = (g * jax.nn.sigmoid(g)) * u * 0.5
        o_ref[...] += _dot(a.astype(BF16), wd_buf[slot])
        if then_start is not None:
            start(then_start, slot)

    start(0, 0)
    start(1, 1)
    x = x_ref[...]
    h_ref[...] = (x * _rms_scale(x) * gin_ref[...]).astype(BF16)
    o_ref[...] = x

    looped = (FFN_CHUNKS - 3) // 2

    def pair(p, carry):
        chunk(2 * p, 0, then_start=2 * p + 2)
        chunk(2 * p + 1, 1, then_start=2 * p + 3)
        return carry

    lax.fori_loop(0, looped, pair, 0)
    for f in range(2 * looped, FFN_CHUNKS):
        chunk(f, f % 2, then_start=f + 2 if f + 2 < FFN_CHUNKS else None)


def _ffn_ring(x, g_in, wg, wu, wd):
    T = x.shape[0]
    tm, tf = FFN_TM, FFN_TF
    tok = pl.BlockSpec((tm, D_MODEL), lambda i: (i, 0))
    hbm = pl.BlockSpec(memory_space=pl.ANY)
    return pl.pallas_call(
        _ffn_ring_kernel,
        grid=(T // tm,),
        in_specs=[tok, pl.BlockSpec((1, D_MODEL), lambda i: (0, 0)), hbm, hbm, hbm],
        out_specs=tok,
        out_shape=jax.ShapeDtypeStruct((T, D_MODEL), F32),
        scratch_shapes=[pltpu.VMEM((tm, D_MODEL), BF16),
                        pltpu.VMEM((2, D_MODEL, tf), BF16), pltpu.VMEM((2, D_MODEL, tf), BF16),
                        pltpu.VMEM((2, tf, D_MODEL), BF16), pltpu.SemaphoreType.DMA((3, 2))],
        compiler_params=_params("arbitrary"),
        name="ffn_ring",
    )(x, g_in, wg, wu, wd)


QUERY, KEY, PLAIN, GATE = "query", "key", "plain", "gate"


def _sigmoid(y):
    return 0.5 * jnp.tanh(0.5 * y) + 0.5


def _norm_rope(y, gain, cos, sin):
    y = y * _rms_scale(y) * gain
    lane = lax.broadcasted_iota(jnp.int32, y.shape, 1)
    partner = jnp.where(lane < ROPE_HALF, pltpu.roll(y, HEAD_DIM - ROPE_HALF, 1), pltpu.roll(y, ROPE_HALF, 1))
    return y * cos + partner * sin


def _proj_stages(h_ref, w_refs, gain_ref, cos_ref, sin_ref, o_ref, y_ref, slab, *, kinds, chunk, dilation):
    def compute(m, slot):
        h = h_ref[pl.ds(m * chunk, chunk), :]
        for p, w_ref in enumerate(w_refs):
            y = _dot(h, w_ref[...])
            for j in range(PROJ_PART_HEADS):
                y_ref[slot, p * PROJ_PART_HEADS + j] = y[:, j * HEAD_DIM:(j + 1) * HEAD_DIM]

    def finish(m, slot):
        rows = pl.ds(m * chunk, chunk)
        for hh, kind in enumerate(kinds):
            y = y_ref[slot, hh]
            if kind in (QUERY, KEY):
                gain = gain_ref[0:1, :] if kind == QUERY else gain_ref[1:2, :]
                y = _norm_rope(y, gain, cos_ref[rows, :], sin_ref[rows, :])
            elif kind == GATE:
                y = _sigmoid(y)
            if dilation is None:
                o_ref[rows, hh * HEAD_DIM:(hh + 1) * HEAD_DIM] = y.astype(BF16)
            else:
                sec, head = divmod(hh, DIL_HEADS)
                sec = DIL_SECTION_SLOT[sec]
                n = chunk // dilation
                dst = pl.ds(m * n, n)
                if dilation == 1:
                    o_ref[sec, head, 0, dst, :] = y.astype(BF16)
                elif dilation <= MAX_SINGLE_OP_STRIDE:
                    slab_ref = slab[0]
                    slab_ref[hh] = y
                    for c in range(dilation):
                        o_ref[sec, head, c, dst, :] = slab_ref[hh, pl.ds(c, n, stride=dilation), :].astype(BF16)
                else:
                    slab_ref, slab2_ref = slab
                    step = MAX_SINGLE_OP_STRIDE
                    assert dilation == step * step
                    n1 = chunk // step
                    slab_ref[hh] = y
                    for c_lo in range(step):
                        slab2_ref[hh, c_lo * n1:(c_lo + 1) * n1, :] = slab_ref[hh, pl.ds(c_lo, n1, stride=step), :]
                    for c_lo in range(step):
                        for c_hi in range(step):
                            o_ref[sec, head, c_lo + step * c_hi, dst, :] = slab2_ref[
                                hh, pl.ds(c_lo * n1 + c_hi, n, stride=step), :].astype(BF16)

    return compute, finish


def _proj_kernel(h_ref, *refs, kinds, chunk, dilation, convert):
    n_parts = len(kinds) // PROJ_PART_HEADS
    w_refs = refs[:n_parts]
    gain_ref, cos_ref, sin_ref = refs[n_parts:n_parts + 3]
    refs = refs[n_parts + 3:]
    if convert:
        src_ref, o_ref, dst_ref, y_ref, wbf_ref = refs[:5]
        refs = refs[5:]

        @pl.when(pl.program_id(0) == 0)
        def _():
            for p, w_ref in enumerate(w_refs):
                wbf_ref[p] = w_ref[...].astype(BF16)

        dst_ref[...] = src_ref[...].astype(BF16)
        w_refs = [wbf_ref.at[p] for p in range(n_parts)]
    else:
        o_ref, y_ref = refs[:2]
        refs = refs[2:]
    compute, finish = _proj_stages(h_ref, w_refs, gain_ref, cos_ref, sin_ref, o_ref, y_ref, refs,
                                   kinds=kinds, chunk=chunk, dilation=dilation)
    n_chunks = h_ref.shape[0] // chunk
    compute(0, 0)
    for m in range(n_chunks - 1):
        finish(m, m % 2)
        compute(m + 1, (m + 1) % 2)
    finish(n_chunks - 1, (n_chunks - 1) % 2)


def _weight_specs(parts):
    return [pl.BlockSpec((D_MODEL, PROJ_PART_WIDTH), functools.partial(lambda blk, i: (0, blk), blk),
                         pipeline_mode=pl.Buffered(1)) for blk in parts]


def _proj(h, w, gains, cos, sin, *, parts, kinds, tm, name, dil=None):
    T = h.shape[0]
    width = len(kinds) * HEAD_DIM
    assert width == len(parts) * PROJ_PART_WIDTH and w.shape[1] % PROJ_PART_WIDTH == 0
    chunk = PROJ_CHUNK
    steps = T // tm
    convert = w.dtype == F32
    scratch = [pltpu.VMEM((2, len(kinds), chunk, HEAD_DIM), F32)]
    if convert:
        scratch.append(pltpu.VMEM((len(parts), D_MODEL, PROJ_PART_WIDTH), BF16))
    if dil is None:
        dilation = None
        out_specs = [pl.BlockSpec((tm, width), lambda i: (i, 0))]
        out_shape = [jax.ShapeDtypeStruct((T, width), BF16)]
    else:
        dilation, batch, seq = dil
        n_i = seq // tm
        out_specs = [pl.BlockSpec((3, None, DIL_HEADS, dilation, tm // dilation, HEAD_DIM),
                                  lambda i: (0, i // n_i, 0, 0, i % n_i, 0))]
        out_shape = [jax.ShapeDtypeStruct((3, batch, DIL_HEADS, dilation, seq // dilation, HEAD_DIM), BF16)]
        if dilation > 1:
            scratch.append(pltpu.VMEM((len(kinds), chunk, HEAD_DIM), F32))
        if dilation > MAX_SINGLE_OP_STRIDE:
            scratch.append(pltpu.VMEM((len(kinds), chunk, HEAD_DIM), F32))
    tok = lambda cols: pl.BlockSpec((tm, cols), lambda i: (i, 0))
    in_specs = [tok(D_MODEL), *_weight_specs(parts), pl.BlockSpec((2, HEAD_DIM), lambda i: (0, 0)),
                tok(HEAD_DIM), tok(HEAD_DIM)]
    args = [h, *([w] * len(parts)), gains, cos, sin]
    if convert:
        rows_spec = pl.BlockSpec((w.shape[0] // steps, w.shape[1]), lambda i: (i, 0))
        in_specs.append(rows_spec)
        args.append(w)
        out_specs.append(rows_spec)
        out_shape.append(jax.ShapeDtypeStruct(w.shape, BF16))
    return pl.pallas_call(
        functools.partial(_proj_kernel, kinds=kinds, chunk=chunk, dilation=dilation, convert=convert),
        grid=(steps,),
        in_specs=in_specs,
        out_specs=out_specs,
        out_shape=out_shape,
        scratch_shapes=scratch,
        compiler_params=_params("arbitrary"),
        name=name,
    )(*args)


def _dil_attn_stages(ins, o_ref, scr, i):
    steps = DIL_BLOCK // BAND

    row = lax.broadcasted_iota(jnp.int32, (BAND, 2 * BAND), 0)
    col = lax.broadcasted_iota(jnp.int32, (BAND, 2 * BAND), 1)
    band_bias = jnp.where((col >= row) & (col <= row + BAND), 0.0, NEG_BIG)
    start_bias = jnp.where(i == 0, jnp.where(col >= BAND, band_bias, NEG_BIG), band_bias)

    groups = []
    for g, r in enumerate(DILATIONS):
        cur_ref, prev_ref = ins[2 * g:2 * g + 2]
        q_ref, kc_ref, vc_ref = (cur_ref.at[DIL_SECTION_SLOT[sec]] for sec in range(3))
        kp_ref, vp_ref = (prev_ref.at[DIL_SECTION_SLOT[sec]] for sec in (1, 2))
        kx_ref, vx_ref, on_ref, ln_ref = scr[4 * g:4 * g + 4]
        kx_ref[:, :BAND, :] = kp_ref[...]
        kx_ref[:, BAND:, :] = kc_ref[...]
        vx_ref[:, :BAND, :HEAD_DIM] = vp_ref[...]
        vx_ref[:, BAND:, :HEAD_DIM] = vc_ref[...]
        vx_ref[:, :, HEAD_DIM:] = jnp.ones((r, vx_ref.shape[1], HEAD_DIM), BF16)
        groups.append((r, q_ref, kx_ref, vx_ref, on_ref, ln_ref))

    def tile(t):
        for r, q_ref, kx_ref, vx_ref, on_ref, ln_ref in groups:
            tiles = DIL_BLOCK // r // BAND
            c, mb = (0, t) if tiles == steps else (t, 0) if tiles == 1 else (t % r, t // r)
            m0 = mb * BAND
            q = q_ref[c, pl.ds(m0, BAND), :]
            k = kx_ref[c, pl.ds(m0, 2 * BAND), :]
            v = vx_ref[c, pl.ds(m0, 2 * BAND), :]
            s = _dot_nt(q, k) + (start_bias if mb == 0 else band_bias)
            m = jnp.max(s, axis=-1, keepdims=True)
            p = jnp.exp(s - m)
            od = _dot(p.astype(BF16), v)
            den = od[:, HEAD_DIM:]
            dst = pl.ds(mb * (BAND * r) + c, BAND, stride=r) if r > 1 else pl.ds(m0, BAND)
            on_ref[dst, :] = od[:, :HEAD_DIM] / den
            ln_ref[dst, :] = m + jnp.log(den)

    def mix():
        lses = [grp[5][...] for grp in groups]
        top = jnp.maximum(jnp.maximum(lses[0], lses[1]), lses[2])
        ws = [jnp.exp(l - top) for l in lses]
        num = ws[0] * groups[0][4][...] + ws[1] * groups[1][4][...] + ws[2] * groups[2][4][...]
        o_ref[...] = (num / (ws[0] + ws[1] + ws[2])).astype(BF16)

    return tile, mix


def _dil_attn_specs(unit):
    in_specs, scratch = [], []
    for r in DILATIONS:
        nq = DIL_BLOCK // r
        per = nq // BAND

        def cur(*ids):
            b, h, i = unit(*ids)
            return (0, b, h, 0, i, 0)

        def prev(*ids, per=per):
            b, h, i = unit(*ids)
            return (0, b, h, 0, jnp.maximum(i * per - 1, 0), 0)

        in_specs += [pl.BlockSpec((3, None, None, r, nq, HEAD_DIM), cur),
                     pl.BlockSpec((2, None, None, r, BAND, HEAD_DIM), prev)]
        scratch += [pltpu.VMEM((r, BAND + nq, HEAD_DIM), BF16), pltpu.VMEM((r, BAND + nq, 2 * HEAD_DIM), BF16),
                    pltpu.VMEM((DIL_BLOCK, HEAD_DIM), F32), pltpu.VMEM((DIL_BLOCK, HEAD_DIM), F32)]

    def out_index(*ids):
        b, h, i = unit(*ids)
        return (b, i, h)

    return in_specs, pl.BlockSpec((None, DIL_BLOCK, HEAD_DIM), out_index), scratch


def _gate_dil_kernel(h_ref, *refs, kinds, chunk, n_i):
    n_parts = len(kinds) // PROJ_PART_HEADS
    n_dil = 2 * N_DIL_GROUPS
    w_refs, dil_ins = refs[:n_parts], refs[n_parts:n_parts + n_dil]
    vg_ref, od_ref, y_ref = refs[n_parts + n_dil:n_parts + n_dil + 3]
    dil_scr = refs[n_parts + n_dil + 3:]
    compute, finish = _proj_stages(h_ref, w_refs, None, None, None, vg_ref, y_ref, (), kinds=kinds, chunk=chunk,
                                   dilation=None)
    tile, mix = _dil_attn_stages(dil_ins, od_ref, dil_scr, pl.program_id(0) % n_i)
    n_chunks = h_ref.shape[0] // chunk
    per_chunk = (DIL_BLOCK // BAND) // n_chunks
    compute(0, 0)
    for m in range(n_chunks):
        for t in range(m * per_chunk, (m + 1) * per_chunk):
            tile(t)
        finish(m, m % 2)
        if m + 1 < n_chunks:
            compute(m + 1, (m + 1) % 2)
    mix()


def _gate_dil(h, w, qkvs, *, parts, kinds, batch, seq):
    T = h.shape[0]
    tm, chunk = PROJ_GATE_TM, PROJ_CHUNK
    n_i = seq // DIL_BLOCK
    steps = T // tm
    assert steps == batch * DIL_HEADS * n_i
    width = len(kinds) * HEAD_DIM
    unit = lambda s: (s // (DIL_HEADS * n_i), (s // n_i) % DIL_HEADS, s % n_i)
    dil_specs, od_spec, dil_scratch = _dil_attn_specs(unit)
    dil_args = [qkvs[g] for g in range(N_DIL_GROUPS) for _ in range(2)]
    return pl.pallas_call(
        functools.partial(_gate_dil_kernel, kinds=kinds, chunk=chunk, n_i=n_i),
        grid=(steps,),
        in_specs=[pl.BlockSpec((tm, D_MODEL), lambda s: (s, 0)), *_weight_specs(parts), *dil_specs],
        out_specs=[pl.BlockSpec((tm, width), lambda s: (s, 0)), od_spec],
        out_shape=[jax.ShapeDtypeStruct((T, width), BF16),
                   jax.ShapeDtypeStruct((batch, seq, DIL_GROUP_WIDTH), BF16)],
        scratch_shapes=[pltpu.VMEM((2, len(kinds), chunk, HEAD_DIM), F32), *dil_scratch],
        compiler_params=_params("arbitrary"),
        name="gate_dil",
    )(h, *([w] * len(parts)), *dil_args)


def _diff_attn_kernel(lq1_ref, lk1_ref, lq2_ref, lk2_ref, q_ref, k_ref, v_ref, sub_ref, *refs, lam_init, n_casts):
    cast_in, (o_ref, *cast_out) = refs[:n_casts], refs[n_casts:2 * n_casts + 1]
    vt_ref, acc_ref, s0_ref, s1_ref = refs[2 * n_casts + 1:]
    for src_ref, dst_ref in zip(cast_in, cast_out):
        dst_ref[...] = src_ref[...].astype(BF16)

    qi = pl.program_id(2)
    tq = q_ref.shape[0]
    tk = vt_ref.shape[2]
    dv = DIFF_HEAD_WIDTH
    per_q = tq // tk

    @pl.when(qi == 0)
    def _():
        for ch in range(vt_ref.shape[0]):
            vt_ref[ch, :dv, :] = v_ref[ch * tk:(ch + 1) * tk, :].astype(F32).T.astype(BF16)
            vt_ref[ch, dv:, :] = jnp.ones((vt_ref.shape[1] - dv, tk), BF16)

    q = q_ref[...]
    acc_ref[...] = jnp.zeros(acc_ref.shape, F32)

    def scores(kb, s_ref):
        k = k_ref[pl.ds(pl.multiple_of(kb * tk, tk), tk), :]
        for c in range(2):
            lanes = slice(c * HEAD_DIM, (c + 1) * HEAD_DIM)
            s_ref[c] = _dot_nt(k[:, lanes], q[:, lanes])

    def update(kb, s_ref, maxes, qcols=slice(None), mask=None):
        vt = vt_ref[kb]
        out = []
        for c in range(2):
            s = s_ref[c, :, qcols]
            if mask is not None:
                s = jnp.where(mask, s, NEG_BIG)
            m_new = jnp.maximum(maxes[c], jnp.max(s, axis=0, keepdims=True))
            alpha = jnp.exp2(maxes[c] - m_new)
            p = jnp.exp2(s - m_new)
            out.append(m_new)
            acc_ref[c, :, qcols] = alpha * acc_ref[c, :, qcols] + _dot(vt, p.astype(BF16))
        return tuple(out)

    def pair(j, maxes):
        kb = per_q * j
        scores(kb + 1, s1_ref)
        maxes = update(kb, s0_ref, maxes)
        scores(kb + 2, s0_ref)
        return update(kb + 1, s1_ref, maxes)

    row0 = jnp.full((1, tq), NEG_BIG, F32)
    scores(0, s0_ref)
    maxes = lax.fori_loop(0, qi, pair, (row0, row0))

    kd = per_q * qi
    lo, hi = slice(0, tk), slice(tk, tq)
    key = lax.broadcasted_iota(jnp.int32, (tk, tk), 0)
    qry = lax.broadcasted_iota(jnp.int32, (tk, tk), 1)
    tri = key <= qry
    k_last = k_ref[pl.ds(pl.multiple_of((kd + 1) * tk, tk), tk), :]
    for c in range(2):
        lanes = slice(c * HEAD_DIM, (c + 1) * HEAD_DIM)
        s1_ref[c, :, hi] = _dot_nt(k_last[:, lanes], q[tk:, lanes])
    update(kd, s0_ref, tuple(m[:, lo] for m in maxes), lo, tri)
    m_hi = update(kd, s0_ref, tuple(m[:, hi] for m in maxes), hi)
    update(kd + 1, s1_ref, m_hi, hi, tri)

    lam = (jnp.exp(jnp.sum(lq1_ref[...] * lk1_ref[...], axis=-1, keepdims=True))
           - jnp.exp(jnp.sum(lq2_ref[...] * lk2_ref[...], axis=-1, keepdims=True)) + lam_init)
    o = (acc_ref[0, :dv] / acc_ref[0, dv:dv + 1] - lam * (acc_ref[1, :dv] / acc_ref[1, dv:dv + 1])).T
    o_ref[...] = (o * _rms_scale(o) * sub_ref[...] * (1.0 - lam_init)).astype(BF16)


def _diff_attn(qk, vg, lq1, lk1, lq2, lk2, subln, casts, *, batch, seq, lam_init):
    tq, tk = DIFF_TQ, DIFF_TK
    assert tq == 2 * tk
    q_blk, k_blk, v_blk = 0, DIFF_HEADS, 0
    n_q = seq // tq
    steps = batch * DIFF_HEADS * n_q
    vec = pl.BlockSpec((1, HEAD_DIM), lambda b, h, i: (0, 0))

    cast_specs = []
    for w in casts:
        share = steps
        while w.shape[0] % (share * BF16_SUBLANES):
            share //= 2
        cast_specs.append(pl.BlockSpec(
            (w.shape[0] // share, w.shape[1]),
            functools.partial(lambda rep, b, h, i: (((b * DIFF_HEADS + h) * n_q + i) // rep, 0), steps // share)))
    o, *cast = pl.pallas_call(
        functools.partial(_diff_attn_kernel, lam_init=lam_init, n_casts=len(casts)),
        grid=(batch, DIFF_HEADS, n_q),
        in_specs=[
            vec, vec, vec, vec,
            pl.BlockSpec((None, tq, DIFF_HEAD_WIDTH), lambda b, h, i: (b, i, q_blk + h)),
            pl.BlockSpec((None, seq, DIFF_HEAD_WIDTH), lambda b, h, i: (b, 0, k_blk + h)),
            pl.BlockSpec((None, seq, DIFF_HEAD_WIDTH), lambda b, h, i: (b, 0, v_blk + h)),
            pl.BlockSpec((1, DIFF_HEAD_WIDTH), lambda b, h, i: (0, 0)),
            *cast_specs,
        ],
        out_specs=[pl.BlockSpec((None, tq, DIFF_HEAD_WIDTH), lambda b, h, i: (b, i, h)), *cast_specs],
        out_shape=[jax.ShapeDtypeStruct((batch, seq, DIFF_WIDTH), BF16),
                   *(jax.ShapeDtypeStruct(w.shape, BF16) for w in casts)],
        scratch_shapes=[pltpu.VMEM((seq // tk, DIFF_HEAD_WIDTH + DIFF_ONES_ROWS, tk), BF16),
                        pltpu.VMEM((2, DIFF_HEAD_WIDTH + DIFF_ONES_ROWS, tq), F32),
                        pltpu.VMEM((2, tk, tq), F32),
                        pltpu.VMEM((2, tk, tq), F32)],
        compiler_params=_params("arbitrary", "arbitrary", "arbitrary"),
        name="diff_attn",
    )(lq1, lk1, lq2, lk2, qk, qk, vg, subln, *casts)
    return o, cast


def _merge_kernel(x_ref, od_ref, of_ref, gd0_ref, gd1_ref, gf0_ref, gf1_ref, wa_ref, wb_ref, wo_ref, o_ref):
    gd = jnp.concatenate([gd0_ref[...], gd1_ref[...]], axis=1).astype(F32)
    gf = jnp.concatenate([gf0_ref[...], gf1_ref[...]], axis=1).astype(F32)
    y = gd * _dot(od_ref[...], wa_ref[...]) + gf * _dot(of_ref[...], wb_ref[...])
    o_ref[...] = x_ref[...] + _dot(y.astype(BF16), wo_ref[...])


def _merge(x, o_dil, o_diff, vg, wa, wb, wo):
    T = x.shape[0]
    tm = MERGE_TM
    half = D_MODEL // 2

    def tok(width, blk=0):
        return pl.BlockSpec((tm, width), lambda i: (i, blk))

    def resident(shape):
        return pl.BlockSpec(shape, lambda i: (0, 0), pipeline_mode=pl.Buffered(1))

    gate0 = DIFF_WIDTH // half
    return pl.pallas_call(
        _merge_kernel,
        grid=(T // tm,),
        in_specs=[tok(D_MODEL), tok(DIL_GROUP_WIDTH), tok(DIFF_WIDTH),
                  tok(half, gate0), tok(half, gate0 + 1), tok(half, gate0 + 2), tok(half, gate0 + 3),
                  resident(wa.shape), resident(wb.shape), resident(wo.shape)],
        out_specs=tok(D_MODEL),
        out_shape=jax.ShapeDtypeStruct((T, D_MODEL), F32),
        compiler_params=_params("parallel"),
        name="merge",
    )(x, o_dil, o_diff, vg, vg, vg, vg, wa, wb, wo)


def _rope_tables(positions):
    lane = jnp.arange(HEAD_DIM)
    freq = ROPE_THETA ** (-(2 * (lane % ROPE_HALF)).astype(F32) / ROPE_DIM)
    inv = jnp.where(lane < ROPE_DIM, freq, 0.0)
    sign = jnp.where(lane < ROPE_HALF, -1.0, jnp.where(lane < ROPE_DIM, 1.0, 0.0)).astype(F32)
    ang = positions.astype(F32).reshape(-1, 1) * inv
    return jnp.cos(ang), jnp.sin(ang) * sign


def _layer(x, tables, layer, ffn1_norm, ffn1_w_gate, ffn1_w_up, ffn1_w_down, mix_norm, w_in,
           dil_q_norm, dil_k_norm, diff_q_norm, diff_k_norm, diff_lq1, diff_lk1, diff_lq2, diff_lk2,
           diff_subln, w_dil_branch, w_diff_branch, w_out, ffn2_norm, ffn2_w_gate, ffn2_w_up, ffn2_w_down,
           *, batch, seq):
    cos, sin = tables
    row = lambda v: v.reshape(1, -1).astype(F32)
    qk_scale = HEAD_DIM ** -0.5
    lam_init = 0.8 - 0.6 * math.exp(-0.3 * layer)

    x1_0, h_0, wg1, wu1, wd1 = _ffn(x, row(ffn1_norm), ffn1_w_gate, ffn1_w_up, ffn1_w_down, row(mix_norm))
    x1, h = _ffn(x, row(ffn1_norm), wg1, wu1, wd1, row(mix_norm), head=(x1_0, h_0))

    blocks = lambda start, width: tuple(range(start // PROJ_PART_WIDTH, (start + width) // PROJ_PART_WIDTH))
    diff0 = 3 * DIL_WIDTH
    fv0 = diff0 + 2 * DIFF_WIDTH
    gains = lambda gq, gk: jnp.stack([gq * qk_scale, gk]).astype(F32)

    dil_kinds = (QUERY,) * DIL_HEADS + (KEY,) * DIL_HEADS + (PLAIN,) * DIL_HEADS
    qkvs, w_in_bf = [], w_in
    for g, r in enumerate(DILATIONS):
        out = _proj(h, w_in_bf, gains(dil_q_norm, dil_k_norm), cos, sin,
                    parts=tuple(sec * N_DIL_GROUPS + g for sec in range(3)),
                    kinds=dil_kinds, tm=PROJ_TM, name=f"dil_proj{g}", dil=(r, batch, seq))
        if g == 0:
            out, w_in_bf = out
        else:
            (out,) = out
        qkvs.append(out)

    n_diff = DIFF_WIDTH // HEAD_DIM
    (qk,) = _proj(h, w_in_bf, gains(diff_q_norm * LOG2_E, diff_k_norm), cos, sin, parts=blocks(diff0, 2 * DIFF_WIDTH),
                  kinds=(QUERY,) * n_diff + (KEY,) * n_diff, tm=PROJ_TM, name="diff_qk_proj")
    n_gate = 2 * D_MODEL // HEAD_DIM
    vg, o_dil = _gate_dil(h, w_in_bf, qkvs, parts=blocks(fv0, DIFF_WIDTH + 2 * D_MODEL),
                          kinds=(PLAIN,) * n_diff + (GATE,) * n_gate, batch=batch, seq=seq)

    later = [w_dil_branch, w_diff_branch, w_out, ffn2_w_gate, ffn2_w_up, ffn2_w_down]
    o_diff, (wa, wb, wo, wg2, wu2, wd2) = _diff_attn(
        qk.reshape(batch, seq, -1), vg.reshape(batch, seq, -1), row(diff_lq1), row(diff_lk1), row(diff_lq2),
        row(diff_lk2), row(diff_subln), later, batch=batch, seq=seq, lam_init=lam_init)

    x2 = _merge(x1, o_dil.reshape(batch * seq, -1), o_diff.reshape(batch * seq, -1), vg, wa, wb, wo)
    out = _ffn_ring(x2, row(ffn2_norm), wg2, wu2, wd2)
    return out


def kernel(x, positions, ffn1_norm, ffn1_w_gate, ffn1_w_up, ffn1_w_down, mix_norm, w_in, dil_q_norm, dil_k_norm, diff_q_norm, diff_k_norm, diff_lq1, diff_lk1, diff_lq2, diff_lk2, diff_subln, w_dil_branch, w_diff_branch, w_out, ffn2_norm, ffn2_w_gate, ffn2_w_up, ffn2_w_down):
    batch, seq, d_model = x.shape
    assert d_model == D_MODEL and seq % DIL_BLOCK == 0
    weights = (ffn1_norm, ffn1_w_gate, ffn1_w_up, ffn1_w_down, mix_norm, w_in, dil_q_norm, dil_k_norm,
               diff_q_norm, diff_k_norm, diff_lq1, diff_lk1, diff_lq2, diff_lk2, diff_subln, w_dil_branch,
               w_diff_branch, w_out, ffn2_norm, ffn2_w_gate, ffn2_w_up, ffn2_w_down)
    tables = _rope_tables(positions)
    y = x.reshape(batch * seq, d_model)
    for layer in range(ffn1_norm.shape[0]):
        y = _layer(y, tables, layer, *(w[layer] for w in weights), batch=batch, seq=seq)
    return y.reshape(batch, seq, d_model)
```

```python
import functools
import math

import jax
import jax.numpy as jnp
from jax import lax
from jax.experimental import pallas as pl
from jax.experimental.pallas import tpu as pltpu

F32 = jnp.float32
BF16 = jnp.bfloat16

D_MODEL = 2048
D_FF = 5632
HEAD_DIM = 128
ROPE_DIM = HEAD_DIM // 4
ROPE_HALF = ROPE_DIM // 2
ROPE_THETA = 500000.0
RMS_EPS = 1e-6
BAND = 128
DILATIONS = (1, 4, 16)
N_DIL_GROUPS = len(DILATIONS)
DIL_HEADS = 4
DIL_GROUP_WIDTH = DIL_HEADS * HEAD_DIM
DIL_WIDTH = N_DIL_GROUPS * DIL_GROUP_WIDTH
DIFF_HEADS = 4
DIFF_HEAD_WIDTH = 2 * HEAD_DIM
DIFF_WIDTH = DIFF_HEADS * DIFF_HEAD_WIDTH
DIL_SECTION_SLOT = (2, 0, 1)
DIL_BLOCK = BAND * DILATIONS[-1]
LOG2_E = math.log2(math.e)
NEG_BIG = -1e30

VMEM_LIMIT_BYTES = 62 * 1024 * 1024

FFN_TM = 1024
FFN_TF = 512
FFN_HEAD_TF = 256
PROJ_TM = 1024
PROJ_GATE_TM = 512
PROJ_CHUNK = 256
PROJ_PART_HEADS = 4
PROJ_PART_WIDTH = PROJ_PART_HEADS * HEAD_DIM
MAX_SINGLE_OP_STRIDE = 4
MERGE_TM = 512
DIFF_TQ = 1024
DIFF_TK = 512
BF16_SUBLANES = 16
DIFF_ONES_ROWS = BF16_SUBLANES


def _params(*sem):
    return pltpu.CompilerParams(dimension_semantics=sem, vmem_limit_bytes=VMEM_LIMIT_BYTES)


def _rms_scale(x):
    return lax.rsqrt(jnp.mean(x * x, axis=-1, keepdims=True) + RMS_EPS)


def _dot(a, b):
    return jnp.dot(a, b, preferred_element_type=F32)


def _dot_nt(a, b):
    return lax.dot_general(a, b, (((1,), (1,)), ((), ())), preferred_element_type=F32)


def _ffn_kernel(*refs, emit_norm, emit_weights, n_head, n_f):
    x_ref, gin_ref, wg_ref, wu_ref, wd_ref = refs[:5]
    gout_ref = refs[5] if emit_norm else None
    head_refs = refs[5 + emit_norm:5 + emit_norm + n_head]
    outs = list(refs[5 + emit_norm + n_head:])
    o_ref = outs.pop(0)
    hn_ref = outs.pop(0) if emit_norm else None
    w_outs = [outs.pop(0) for _ in range(3)] if emit_weights else None
    (h_ref,) = outs
    step_id = pl.program_id(0)
    offset = 1 if n_head else 0
    f = jnp.maximum(step_id - offset, 0) % n_f
    last = n_f - 1
    live = step_id >= offset

    def step(h):
        wg, wu, wd = wg_ref[...], wu_ref[...], wd_ref[...]
        if emit_weights:
            wg, wu, wd = wg.astype(BF16), wu.astype(BF16), wd.astype(BF16)
            for w_out, w in zip(w_outs, (wg, wu, wd)):
                w_out[...] = w
        g = _dot(h, wg)
        u = _dot(h, wu)
        a = (g * jax.nn.sigmoid(g)) * u * 0.5
        o_ref[...] += _dot(a.astype(BF16), wd)

    def finish():
        if emit_norm:
            y = o_ref[...]
            hn_ref[...] = (y * _rms_scale(y) * gout_ref[...]).astype(BF16)

    @pl.when(live & (f == 0))
    def _():
        x = x_ref[...]
        h = (x * _rms_scale(x) * gin_ref[...]).astype(BF16)
        h_ref[...] = h
        o_ref[...] = x
        step(h)

    @pl.when(live & (f > 0) & (f < last))
    def _():
        step(h_ref[...])

    @pl.when(live & (f == last))
    def _():
        step(h_ref[...])
        finish()

    if n_head:
        @pl.when(step_id == 0)
        def _():
            for src, dst in zip(head_refs, (o_ref, hn_ref)):
                pltpu.sync_copy(src, dst)


def _ffn(x, g_in, wg, wu, wd, g_out, *, head=()):
    T = x.shape[0]
    tm = FFN_TM
    emit_norm = g_out is not None
    emit_weights = wg.dtype == F32
    tf = FFN_HEAD_TF if emit_weights else FFN_TF
    n_tiles = 1 if emit_weights else T // tm
    skip = 1 if head else 0
    mode = dict(pipeline_mode=pl.Buffered(1)) if n_tiles == 1 else {}
    n_f = D_FF // tf
    computed = lambda s: jnp.maximum(s - skip, 0)
    tile = lambda s: jnp.where(s < skip, 0, computed(s) // n_f + skip)
    chunk = lambda s: computed(s) % n_f
    tok = pl.BlockSpec((tm, D_MODEL), lambda s: (tile(s), 0), **mode)
    vec = pl.BlockSpec((1, D_MODEL), lambda s: (0, 0))
    w_up_spec = pl.BlockSpec((D_MODEL, tf), lambda s: (0, chunk(s)))
    w_down_spec = pl.BlockSpec((tf, D_MODEL), lambda s: (chunk(s), 0))
    in_specs = [tok, vec, w_up_spec, w_up_spec, w_down_spec]
    args = [x, g_in, wg, wu, wd]
    out_specs = [tok]
    out_shape = [jax.ShapeDtypeStruct((n_tiles * tm, D_MODEL), F32)]
    if emit_norm:
        in_specs.append(vec)
        args.append(g_out)
        out_specs.append(tok)
        out_shape.append(jax.ShapeDtypeStruct((n_tiles * tm, D_MODEL), BF16))
    if emit_weights:
        out_specs += [w_up_spec, w_up_spec, w_down_spec]
        out_shape += [jax.ShapeDtypeStruct(w.shape, BF16) for w in (wg, wu, wd)]
    in_specs += [pl.BlockSpec(memory_space=pl.ANY)] * len(head)
    args += list(head)
    return pl.pallas_call(
        functools.partial(_ffn_kernel, emit_norm=emit_norm, emit_weights=emit_weights, n_head=len(head), n_f=n_f),
        grid=(skip + (n_tiles - skip) * n_f,),
        in_specs=in_specs,
        out_specs=out_specs,
        out_shape=out_shape,
        scratch_shapes=[pltpu.VMEM((tm, D_MODEL), BF16)],
        compiler_params=_params("arbitrary"),
        name="ffn_head" if emit_weights else "ffn",
    )(*args)


QUERY, KEY, PLAIN, GATE = "query", "key", "plain", "gate"


def _sigmoid(y):
    return 0.5 * jnp.tanh(0.5 * y) + 0.5


def _norm_rope(y, gain, cos, sin):
    y = y * _rms_scale(y) * gain
    lane = lax.broadcasted_iota(jnp.int32, y.shape, 1)
    partner = jnp.where(lane < ROPE_HALF, pltpu.roll(y, HEAD_DIM - ROPE_HALF, 1), pltpu.roll(y, ROPE_HALF, 1))
    return y * cos + partner * sin


def _proj_stages(h_ref, w_refs, gain_ref, cos_ref, sin_ref, o_ref, y_ref, slab, *, kinds, chunk, dilation):
    def compute(m, slot):
        h = h_ref[pl.ds(m * chunk, chunk), :]
        for p, w_ref in enumerate(w_refs):
            y = _dot(h, w_ref[...])
            for j in range(PROJ_PART_HEADS):
                y_ref[slot, p * PROJ_PART_HEADS + j] = y[:, j * HEAD_DIM:(j + 1) * HEAD_DIM]

    def finish(m, slot):
        rows = pl.ds(m * chunk, chunk)
        for hh, kind in enumerate(kinds):
            y = y_ref[slot, hh]
            if kind in (QUERY, KEY):
                gain = gain_ref[0:1, :] if kind == QUERY else gain_ref[1:2, :]
                y = _norm_rope(y, gain, cos_ref[rows, :], sin_ref[rows, :])
            elif kind == GATE:
                y = _sigmoid(y)
            if dilation is None:
                o_ref[rows, hh * HEAD_DIM:(hh + 1) * HEAD_DIM] = y.astype(BF16)
            else:
                sec, head = divmod(hh, DIL_HEADS)
                sec = DIL_SECTION_SLOT[sec]
                n = chunk // dilation
                dst = pl.ds(m * n, n)
                if dilation == 1:
                    o_ref[sec, head, 0, dst, :] = y.astype(BF16)
                elif dilation <= MAX_SINGLE_OP_STRIDE:
                    slab_ref = slab[0]
                    slab_ref[hh] = y
                    for c in range(dilation):
                        o_ref[sec, head, c, dst, :] = slab_ref[hh, pl.ds(c, n, stride=dilation), :].astype(BF16)
                else:
                    slab_ref, slab2_ref = slab
                    step = MAX_SINGLE_OP_STRIDE
                    assert dilation == step * step
                    n1 = chunk // step
                    slab_ref[hh] = y
                    for c_lo in range(step):
                        slab2_ref[hh, c_lo * n1:(c_lo + 1) * n1, :] = slab_ref[hh, pl.ds(c_lo, n1, stride=step), :]
                    for c_lo in range(step):
                        for c_hi in range(step):
                            o_ref[sec, head, c_lo + step * c_hi, dst, :] = slab2_ref[
                                hh, pl.ds(c_lo * n1 + c_hi, n, stride=step), :].astype(BF16)

    return compute, finish


def _proj_kernel(h_ref, *refs, kinds, chunk, dilation, convert):
    n_parts = len(kinds) // PROJ_PART_HEADS
    w_refs = refs[:n_parts]
    gain_ref, cos_ref, sin_ref = refs[n_parts:n_parts + 3]
    refs = refs[n_parts + 3:]
    if convert:
        src_ref, o_ref, dst_ref, y_ref, wbf_ref = refs[:5]
        refs = refs[5:]

        @pl.when(pl.program_id(0) == 0)
        def _():
            for p, w_ref in enumerate(w_refs):
                wbf_ref[p] = w_ref[...].astype(BF16)

        dst_ref[...] = src_ref[...].astype(BF16)
        w_refs = [wbf_ref.at[p] for p in range(n_parts)]
    else:
        o_ref, y_ref = refs[:2]
        refs = refs[2:]
    compute, finish = _proj_stages(h_ref, w_refs, gain_ref, cos_ref, sin_ref, o_ref, y_ref, refs,
                                   kinds=kinds, chunk=chunk, dilation=dilation)
    n_chunks = h_ref.shape[0] // chunk
    compute(0, 0)
    for m in range(n_chunks - 1):
        finish(m, m % 2)
        compute(m + 1, (m + 1) % 2)
    finish(n_chunks - 1, (n_chunks - 1) % 2)


def _weight_specs(parts):
    return [pl.BlockSpec((D_MODEL, PROJ_PART_WIDTH), functools.partial(lambda blk, i: (0, blk), blk),
                         pipeline_mode=pl.Buffered(1)) for blk in parts]


def _proj(h, w, gains, cos, sin, *, parts, kinds, tm, name, dil=None):
    T = h.shape[0]
    width = len(kinds) * HEAD_DIM
    assert width == len(parts) * PROJ_PART_WIDTH and w.shape[1] % PROJ_PART_WIDTH == 0
    chunk = PROJ_CHUNK
    steps = T // tm
    convert = w.dtype == F32
    scratch = [pltpu.VMEM((2, len(kinds), chunk, HEAD_DIM), F32)]
    if convert:
        scratch.append(pltpu.VMEM((len(parts), D_MODEL, PROJ_PART_WIDTH), BF16))
    if dil is None:
        dilation = None
        out_specs = [pl.BlockSpec((tm, width), lambda i: (i, 0))]
        out_shape = [jax.ShapeDtypeStruct((T, width), BF16)]
    else:
        dilation, batch, seq = dil
        n_i = seq // tm
        out_specs = [pl.BlockSpec((3, None, DIL_HEADS, dilation, tm // dilation, HEAD_DIM),
                                  lambda i: (0, i // n_i, 0, 0, i % n_i, 0))]
        out_shape = [jax.ShapeDtypeStruct((3, batch, DIL_HEADS, dilation, seq // dilation, HEAD_DIM), BF16)]
        if dilation > 1:
            scratch.append(pltpu.VMEM((len(kinds), chunk, HEAD_DIM), F32))
        if dilation > MAX_SINGLE_OP_STRIDE:
            scratch.append(pltpu.VMEM((len(kinds), chunk, HEAD_DIM), F32))
    tok = lambda cols: pl.BlockSpec((tm, cols), lambda i: (i, 0))
    in_specs = [tok(D_MODEL), *_weight_specs(parts), pl.BlockSpec((2, HEAD_DIM), lambda i: (0, 0)),
                tok(HEAD_DIM), tok(HEAD_DIM)]
    args = [h, *([w] * len(parts)), gains, cos, sin]
    if convert:
        rows_spec = pl.BlockSpec((w.shape[0] // steps, w.shape[1]), lambda i: (i, 0))
        in_specs.append(rows_spec)
        args.append(w)
        out_specs.append(rows_spec)
        out_shape.append(jax.ShapeDtypeStruct(w.shape, BF16))
    return pl.pallas_call(
        functools.partial(_proj_kernel, kinds=kinds, chunk=chunk, dilation=dilation, convert=convert),
        grid=(steps,),
        in_specs=in_specs,
        out_specs=out_specs,
        out_shape=out_shape,
        scratch_shapes=scratch,
        compiler_params=_params("arbitrary"),
        name=name,
    )(*args)


def _dil_attn_stages(ins, o_ref, scr, i):
    steps = DIL_BLOCK // BAND

    row = lax.broadcasted_iota(jnp.int32, (BAND, 2 * BAND), 0)
    col = lax.broadcasted_iota(jnp.int32, (BAND, 2 * BAND), 1)
    band_bias = jnp.where((col >= row) & (col <= row + BAND), 0.0, NEG_BIG)
    start_bias = jnp.where(i == 0, jnp.where(col >= BAND, band_bias, NEG_BIG), band_bias)

    groups = []
    for g, r in enumerate(DILATIONS):
        cur_ref, prev_ref = ins[2 * g:2 * g + 2]
        q_ref, kc_ref, vc_ref = (cur_ref.at[DIL_SECTION_SLOT[sec]] for sec in range(3))
        kp_ref, vp_ref = (prev_ref.at[DIL_SECTION_SLOT[sec]] for sec in (1, 2))
        kx_ref, vx_ref, on_ref, ln_ref = scr[4 * g:4 * g + 4]
        kx_ref[:, :BAND, :] = kp_ref[...]
        kx_ref[:, BAND:, :] = kc_ref[...]
        vx_ref[:, :BAND, :HEAD_DIM] = vp_ref[...]
        vx_ref[:, BAND:, :HEAD_DIM] = vc_ref[...]
        vx_ref[:, :, HEAD_DIM:] = jnp.ones((r, vx_ref.shape[1], HEAD_DIM), BF16)
        groups.append((r, q_ref, kx_ref, vx_ref, on_ref, ln_ref))

    def tile(t):
        for r, q_ref, kx_ref, vx_ref, on_ref, ln_ref in groups:
            tiles = DIL_BLOCK // r // BAND
            c, mb = (0, t) if tiles == steps else (t, 0) if tiles == 1 else (t % r, t // r)
            m0 = mb * BAND
            q = q_ref[c, pl.ds(m0, BAND), :]
            k = kx_ref[c, pl.ds(m0, 2 * BAND), :]
            v = vx_ref[c, pl.ds(m0, 2 * BAND), :]
            s = _dot_nt(q, k) + (start_bias if mb == 0 else band_bias)
            m = jnp.max(s, axis=-1, keepdims=True)
            p = jnp.exp(s - m)
            od = _dot(p.astype(BF16), v)
            den = od[:, HEAD_DIM:]
            dst = pl.ds(mb * (BAND * r) + c, BAND, stride=r) if r > 1 else pl.ds(m0, BAND)
            on_ref[dst, :] = od[:, :HEAD_DIM] / den
            ln_ref[dst, :] = m + jnp.log(den)

    def mix():
        lses = [grp[5][...] for grp in groups]
        top = jnp.maximum(jnp.maximum(lses[0], lses[1]), lses[2])
        ws = [jnp.exp(l - top) for l in lses]
        num = ws[0] * groups[0][4][...] + ws[1] * groups[1][4][...] + ws[2] * groups[2][4][...]
        o_ref[...] = (num / (ws[0] + ws[1] + ws[2])).astype(BF16)

    return tile, mix


def _dil_attn_specs(unit):
    in_specs, scratch = [], []
    for r in DILATIONS:
        nq = DIL_BLOCK // r
        per = nq // BAND

        def cur(*ids):
            b, h, i = unit(*ids)
            return (0, b, h, 0, i, 0)

        def prev(*ids, per=per):
            b, h, i = unit(*ids)
            return (0, b, h, 0, jnp.maximum(i * per - 1, 0), 0)

        in_specs += [pl.BlockSpec((3, None, None, r, nq, HEAD_DIM), cur),
                     pl.BlockSpec((2, None, None, r, BAND, HEAD_DIM), prev)]
        scratch += [pltpu.VMEM((r, BAND + nq, HEAD_DIM), BF16), pltpu.VMEM((r, BAND + nq, 2 * HEAD_DIM), BF16),
                    pltpu.VMEM((DIL_BLOCK, HEAD_DIM), F32), pltpu.VMEM((DIL_BLOCK, HEAD_DIM), F32)]

    def out_index(*ids):
        b, h, i = unit(*ids)
        return (b, i, h)

    return in_specs, pl.BlockSpec((None, DIL_BLOCK, HEAD_DIM), out_index), scratch


def _gate_dil_kernel(h_ref, *refs, kinds, chunk, n_i):
    n_parts = len(kinds) // PROJ_PART_HEADS
    n_dil = 2 * N_DIL_GROUPS
    w_refs, dil_ins = refs[:n_parts], refs[n_parts:n_parts + n_dil]
    vg_ref, od_ref, y_ref = refs[n_parts + n_dil:n_parts + n_dil + 3]
    dil_scr = refs[n_parts + n_dil + 3:]
    compute, finish = _proj_stages(h_ref, w_refs, None, None, None, vg_ref, y_ref, (), kinds=kinds, chunk=chunk,
                                   dilation=None)
    tile, mix = _dil_attn_stages(dil_ins, od_ref, dil_scr, pl.program_id(0) % n_i)
    n_chunks = h_ref.shape[0] // chunk
    per_chunk = (DIL_BLOCK // BAND) // n_chunks
    compute(0, 0)
    for m in range(n_chunks):
        for t in range(m * per_chunk, (m + 1) * per_chunk):
            tile(t)
        finish(m, m % 2)
        if m + 1 < n_chunks:
            compute(m + 1, (m + 1) % 2)
    mix()


def _gate_dil(h, w, qkvs, *, parts, kinds, batch, seq):
    T = h.shape[0]
    tm, chunk = PROJ_GATE_TM, PROJ_CHUNK
    n_i = seq // DIL_BLOCK
    steps = T // tm
    assert steps == batch * DIL_HEADS * n_i
    width = len(kinds) * HEAD_DIM
    unit = lambda s: (s // (DIL_HEADS * n_i), (s // n_i) % DIL_HEADS, s % n_i)
    dil_specs, od_spec, dil_scratch = _dil_attn_specs(unit)
    dil_args = [qkvs[g] for g in range(N_DIL_GROUPS) for _ in range(2)]
    return pl.pallas_call(
        functools.partial(_gate_dil_kernel, kinds=kinds, chunk=chunk, n_i=n_i),
        grid=(steps,),
        in_specs=[pl.BlockSpec((tm, D_MODEL), lambda s: (s, 0)), *_weight_specs(parts), *dil_specs],
        out_specs=[pl.BlockSpec((tm, width), lambda s: (s, 0)), od_spec],
        out_shape=[jax.ShapeDtypeStruct((T, width), BF16),
                   jax.ShapeDtypeStruct((batch, seq, DIL_GROUP_WIDTH), BF16)],
        scratch_shapes=[pltpu.VMEM((2, len(kinds), chunk, HEAD_DIM), F32), *dil_scratch],
        compiler_params=_params("arbitrary"),
        name="gate_dil",
    )(h, *([w] * len(parts)), *dil_args)


def _diff_attn_kernel(lq1_ref, lk1_ref, lq2_ref, lk2_ref, q_ref, k_ref, v_ref, sub_ref, *refs, lam_init, n_casts):
    cast_in, (o_ref, *cast_out) = refs[:n_casts], refs[n_casts:2 * n_casts + 1]
    vt_ref, acc_ref, s0_ref, s1_ref = refs[2 * n_casts + 1:]
    for src_ref, dst_ref in zip(cast_in, cast_out):
        dst_ref[...] = src_ref[...].astype(BF16)

    qi = pl.program_id(2)
    tq = q_ref.shape[0]
    tk = vt_ref.shape[2]
    dv = DIFF_HEAD_WIDTH
    per_q = tq // tk

    @pl.when(qi == 0)
    def _():
        for ch in range(vt_ref.shape[0]):
            vt_ref[ch, :dv, :] = v_ref[ch * tk:(ch + 1) * tk, :].astype(F32).T.astype(BF16)
            vt_ref[ch, dv:, :] = jnp.ones((vt_ref.shape[1] - dv, tk), BF16)

    q = q_ref[...]
    acc_ref[...] = jnp.zeros(acc_ref.shape, F32)

    def scores(kb, s_ref):
        k = k_ref[pl.ds(pl.multiple_of(kb * tk, tk), tk), :]
        for c in range(2):
            lanes = slice(c * HEAD_DIM, (c + 1) * HEAD_DIM)
            s_ref[c] = _dot_nt(k[:, lanes], q[:, lanes])

    def update(kb, s_ref, maxes, qcols=slice(None), mask=None):
        vt = vt_ref[kb]
        out = []
        for c in range(2):
            s = s_ref[c, :, qcols]
            if mask is not None:
                s = jnp.where(mask, s, NEG_BIG)
            m_new = jnp.maximum(maxes[c], jnp.max(s, axis=0, keepdims=True))
            alpha = jnp.exp2(maxes[c] - m_new)
            p = jnp.exp2(s - m_new)
            out.append(m_new)
            acc_ref[c, :, qcols] = alpha * acc_ref[c, :, qcols] + _dot(vt, p.astype(BF16))
        return tuple(out)

    def pair(j, maxes):
        kb = per_q * j
        scores(kb + 1, s1_ref)
        maxes = update(kb, s0_ref, maxes)
        scores(kb + 2, s0_ref)
        return update(kb + 1, s1_ref, maxes)

    row0 = jnp.full((1, tq), NEG_BIG, F32)
    scores(0, s0_ref)
    maxes = lax.fori_loop(0, qi, pair, (row0, row0))

    kd = per_q * qi
    lo, hi = slice(0, tk), slice(tk, tq)
    key = lax.broadcasted_iota(jnp.int32, (tk, tk), 0)
    qry = lax.broadcasted_iota(jnp.int32, (tk, tk), 1)
    tri = key <= qry
    k_last = k_ref[pl.ds(pl.multiple_of((kd + 1) * tk, tk), tk), :]
    for c in range(2):
        lanes = slice(c * HEAD_DIM, (c + 1) * HEAD_DIM)
        s1_ref[c, :, hi] = _dot_nt(k_last[:, lanes], q[tk:, lanes])
    update(kd, s0_ref, tuple(m[:, lo] for m in maxes), lo, tri)
    m_hi = update(kd, s0_ref, tuple(m[:, hi] for m in maxes), hi)
    update(kd + 1, s1_ref, m_hi, hi, tri)

    lam = (jnp.exp(jnp.sum(lq1_ref[...] * lk1_ref[...], axis=-1, keepdims=True))
           - jnp.exp(jnp.sum(lq2_ref[...] * lk2_ref[...], axis=-1, keepdims=True)) + lam_init)
    o = (acc_ref[0, :dv] / acc_ref[0, dv:dv + 1] - lam * (acc_ref[1, :dv] / acc_ref[1, dv:dv + 1])).T
    o_ref[...] = (o * _rms_scale(o) * sub_ref[...] * (1.0 - lam_init)).astype(BF16)


def _diff_attn(qk, vg, lq1, lk1, lq2, lk2, subln, casts, *, batch, seq, lam_init):
    tq, tk = DIFF_TQ, DIFF_TK
    assert tq == 2 * tk
    q_blk, k_blk, v_blk = 0, DIFF_HEADS, 0
    n_q = seq // tq
    steps = batch * DIFF_HEADS * n_q
    vec = pl.BlockSpec((1, HEAD_DIM), lambda b, h, i: (0, 0))

    cast_specs = []
    for w in casts:
        share = steps
        while w.shape[0] % (share * BF16_SUBLANES):
            share //= 2
        cast_specs.append(pl.BlockSpec(
            (w.shape[0] // share, w.shape[1]),
            functools.partial(lambda rep, b, h, i: (((b * DIFF_HEADS + h) * n_q + i) // rep, 0), steps // share)))
    o, *cast = pl.pallas_call(
        functools.partial(_diff_attn_kernel, lam_init=lam_init, n_casts=len(casts)),
        grid=(batch, DIFF_HEADS, n_q),
        in_specs=[
            vec, vec, vec, vec,
            pl.BlockSpec((None, tq, DIFF_HEAD_WIDTH), lambda b, h, i: (b, i, q_blk + h)),
            pl.BlockSpec((None, seq, DIFF_HEAD_WIDTH), lambda b, h, i: (b, 0, k_blk + h)),
            pl.BlockSpec((None, seq, DIFF_HEAD_WIDTH), lambda b, h, i: (b, 0, v_blk + h)),
            pl.BlockSpec((1, DIFF_HEAD_WIDTH), lambda b, h, i: (0, 0)),
            *cast_specs,
        ],
        out_specs=[pl.BlockSpec((None, tq, DIFF_HEAD_WIDTH), lambda b, h, i: (b, i, h)), *cast_specs],
        out_shape=[jax.ShapeDtypeStruct((batch, seq, DIFF_WIDTH), BF16),
                   *(jax.ShapeDtypeStruct(w.shape, BF16) for w in casts)],
        scratch_shapes=[pltpu.VMEM((seq // tk, DIFF_HEAD_WIDTH + DIFF_ONES_ROWS, tk), BF16),
                        pltpu.VMEM((2, DIFF_HEAD_WIDTH + DIFF_ONES_ROWS, tq), F32),
                        pltpu.VMEM((2, tk, tq), F32),
                        pltpu.VMEM((2, tk, tq), F32)],
        compiler_params=_params("arbitrary", "arbitrary", "arbitrary"),
        name="diff_attn",
    )(lq1, lk1, lq2, lk2, qk, qk, vg, subln, *casts)
    return o, cast


def _merge_kernel(x_ref, od_ref, of_ref, gd0_ref, gd1_ref, gf0_ref, gf1_ref, wa_ref, wb_ref, wo_ref, o_ref):
    gd = jnp.concatenate([gd0_ref[...], gd1_ref[...]], axis=1).astype(F32)
    gf = jnp.concatenate([gf0_ref[...], gf1_ref[...]], axis=1).astype(F32)
    y = gd * _dot(od_ref[...], wa_ref[...]) + gf * _dot(of_ref[...], wb_ref[...])
    o_ref[...] = x_ref[...] + _dot(y.astype(BF16), wo_ref[...])


def _merge(x, o_dil, o_diff, vg, wa, wb, wo):
    T = x.shape[0]
    tm = MERGE_TM
    half = D_MODEL // 2

    def tok(width, blk=0):
        return pl.BlockSpec((tm, width), lambda i: (i, blk))

    def resident(shape):
        return pl.BlockSpec(shape, lambda i: (0, 0), pipeline_mode=pl.Buffered(1))

    gate0 = DIFF_WIDTH // half
    return pl.pallas_call(
        _merge_kernel,
        grid=(T // tm,),
        in_specs=[tok(D_MODEL), tok(DIL_GROUP_WIDTH), tok(DIFF_WIDTH),
                  tok(half, gate0), tok(half, gate0 + 1), tok(half, gate0 + 2), tok(half, gate0 + 3),
                  resident(wa.shape), resident(wb.shape), resident(wo.shape)],
        out_specs=tok(D_MODEL),
        out_shape=jax.ShapeDtypeStruct((T, D_MODEL), F32),
        compiler_params=_params("parallel"),
        name="merge",
    )(x, o_dil, o_diff, vg, vg, vg, vg, wa, wb, wo)


def _rope_tables(positions):
    lane = jnp.arange(HEAD_DIM)
    freq = ROPE_THETA ** (-(2 * (lane % ROPE_HALF)).astype(F32) / ROPE_DIM)
    inv = jnp.where(lane < ROPE_DIM, freq, 0.0)
    sign = jnp.where(lane < ROPE_HALF, -1.0, jnp.where(lane < ROPE_DIM, 1.0, 0.0)).astype(F32)
    ang = positions.astype(F32).reshape(-1, 1) * inv
    return jnp.cos(ang), jnp.sin(ang) * sign


def _layer(x, tables, layer, ffn1_norm, ffn1_w_gate, ffn1_w_up, ffn1_w_down, mix_norm, w_in,
           dil_q_norm, dil_k_norm, diff_q_norm, diff_k_norm, diff_lq1, diff_lk1, diff_lq2, diff_lk2,
           diff_subln, w_dil_branch, w_diff_branch, w_out, ffn2_norm, ffn2_w_gate, ffn2_w_up, ffn2_w_down,
           *, batch, seq):
    cos, sin = tables
    row = lambda v: v.reshape(1, -1).astype(F32)
    qk_scale = HEAD_DIM ** -0.5
    lam_init = 0.8 - 0.6 * math.exp(-0.3 * layer)

    x1_0, h_0, wg1, wu1, wd1 = _ffn(x, row(ffn1_norm), ffn1_w_gate, ffn1_w_up, ffn1_w_down, row(mix_norm))
    x1, h = _ffn(x, row(ffn1_norm), wg1, wu1, wd1, row(mix_norm), head=(x1_0, h_0))

    blocks = lambda start, width: tuple(range(start // PROJ_PART_WIDTH, (start + width) // PROJ_PART_WIDTH))
    diff0 = 3 * DIL_WIDTH
    fv0 = diff0 + 2 * DIFF_WIDTH
    gains = lambda gq, gk: jnp.stack([gq * qk_scale, gk]).astype(F32)

    dil_kinds = (QUERY,) * DIL_HEADS + (KEY,) * DIL_HEADS + (PLAIN,) * DIL_HEADS
    qkvs, w_in_bf = [], w_in
    for g, r in enumerate(DILATIONS):
        out = _proj(h, w_in_bf, gains(dil_q_norm, dil_k_norm), cos, sin,
                    parts=tuple(sec * N_DIL_GROUPS + g for sec in range(3)),
                    kinds=dil_kinds, tm=PROJ_TM, name=f"dil_proj{g}", dil=(r, batch, seq))
        if g == 0:
            out, w_in_bf = out
        else:
            (out,) = out
        qkvs.append(out)

    n_diff = DIFF_WIDTH // HEAD_DIM
    (qk,) = _proj(h, w_in_bf, gains(diff_q_norm * LOG2_E, diff_k_norm), cos, sin, parts=blocks(diff0, 2 * DIFF_WIDTH),
                  kinds=(QUERY,) * n_diff + (KEY,) * n_diff, tm=PROJ_TM, name="diff_qk_proj")
    n_gate = 2 * D_MODEL // HEAD_DIM
    vg, o_dil = _gate_dil(h, w_in_bf, qkvs, parts=blocks(fv0, DIFF_WIDTH + 2 * D_MODEL),
                          kinds=(PLAIN,) * n_diff + (GATE,) * n_gate, batch=batch, seq=seq)

    later = [w_dil_branch, w_diff_branch, w_out, ffn2_w_gate, ffn2_w_up, ffn2_w_down]
    o_diff, (wa, wb, wo, wg2, wu2, wd2) = _diff_attn(
        qk.reshape(batch, seq, -1), vg.reshape(batch, seq, -1), row(diff_lq1), row(diff_lk1), row(diff_lq2),
        row(diff_lk2), row(diff_subln), later, batch=batch, seq=seq, lam_init=lam_init)

    x2 = _merge(x1, o_dil.reshape(batch * seq, -1), o_diff.reshape(batch * seq, -1), vg, wa, wb, wo)
    (out,) = _ffn(x2, row(ffn2_norm), wg2, wu2, wd2, None)
    return out


def kernel(x, positions, ffn1_norm, ffn1_w_gate, ffn1_w_up, ffn1_w_down, mix_norm, w_in, dil_q_norm, dil_k_norm, diff_q_norm, diff_k_norm, diff_lq1, diff_lk1, diff_lq2, diff_lk2, diff_subln, w_dil_branch, w_diff_branch, w_out, ffn2_norm, ffn2_w_gate, ffn2_w_up, ffn2_w_down):
    batch, seq, d_model = x.shape
    assert d_model == D_MODEL and seq % DIL_BLOCK == 0
    weights = (ffn1_norm, ffn1_w_gate, ffn1_w_up, ffn1_w_down, mix_norm, w_in, dil_q_norm, dil_k_norm,
               diff_q_norm, diff_k_norm, diff_lq1, diff_lk1, diff_lq2, diff_lk2, diff_subln, w_dil_branch,
               w_diff_branch, w_out, ffn2_norm, ffn2_w_gate, ffn2_w_up, ffn2_w_down)
    tables = _rope_tables(positions)
    y = x.reshape(batch * seq, d_model)
    for layer in range(ffn1_norm.shape[0]):
        y = _layer(y, tables, layer, *(w[layer] for w in weights), batch=batch, seq=seq)
    return y.reshape(batch, seq, d_model)
```
